```python
import math
import jax
import jax.numpy as jnp
from jax import lax
import numpy as np

D_MODEL = 1024
BATCH = 16
SEQ = 2048
DEPTH = 1

GRID_W = 64
N_META = 16
NA_HEADS = 8
NA_HEAD_DIM = 64
NA_W = NA_HEADS * NA_HEAD_DIM
NA_WIN_ROWS = 8
NA_WIN_COLS = 16
GDN_HEADS = 8
GDN_HEAD_DIM = 128
GDN_W = GDN_HEADS * GDN_HEAD_DIM
GDN_CONV = 5
GDN_CHUNK = 64
N_GROUPS = 4
EXPERTS_PER_GROUP = 8
N_EXPERTS = N_GROUPS * EXPERTS_PER_GROUP
TOP_K = 2
D_EXPERT = 256
EXPERT_BLOCK = 128
LN_EPS = 1e-5
NORM_EPS = 1e-6
DN_ALPHA = (2 * DEPTH) ** 0.25
DN_BETA = (8 * DEPTH) ** -0.25
IN_SPLITS = (NA_W, NA_W, NA_W, GDN_W, GDN_W, GDN_W, GDN_W, GDN_HEADS, GDN_HEADS, GDN_HEADS, GDN_HEADS, D_MODEL, D_MODEL)
D_IN = 3 * NA_W + 4 * GDN_W + 4 * GDN_HEADS + 2 * D_MODEL
VALUE_BLOCKS = (2, 5)

kernel_name = "hybrid_na_gdn_hmoe_encoder_block"


def layer_norm(x, g, b):
    xf = x.astype(jnp.float32)
    xc = xf - jnp.mean(xf, -1, keepdims=True)
    var = jnp.mean(xc * xc, -1, keepdims=True)
    y = xc * lax.rsqrt(var + LN_EPS) * g.astype(jnp.float32) + b.astype(jnp.float32)
    return y.astype(x.dtype)


def l2_normalize(t):
    return t * lax.rsqrt(jnp.sum(t * t, -1, keepdims=True) + NORM_EPS)


def neighbourhood_attention(q, k, v, rpb, keep_meta):
    B, L, H, dh = q.shape
    n = L - N_META
    rows = n // GRID_W
    kr = min(NA_WIN_ROWS, rows)
    kc = NA_WIN_COLS
    scale = dh ** -0.5
    qm, km, vm = q[:, :N_META], k[:, :N_META], v[:, :N_META]
    qg = q[:, N_META:].reshape(B, rows, GRID_W, H, dh)
    kg = k[:, N_META:].reshape(B, rows, GRID_W, H, dh)
    vg = v[:, N_META:].reshape(B, rows, GRID_W, H, dh)
    row_start = np.clip(np.arange(rows) - kr // 2, 0, rows - kr)
    col = np.arange(GRID_W)
    col_start = np.clip(col - kc // 2, 0, GRID_W - kc)
    col_mask = (col[None, :] >= col_start[:, None]) & (col[None, :] < col_start[:, None] + kc)
    dc_idx = np.clip(col[None, :] - col[:, None] + NA_WIN_COLS - 1, 0, 2 * NA_WIN_COLS - 2)

    def row_block(args):
        r, r0, q_r = args
        k_blk = lax.dynamic_slice_in_dim(kg, r0, kr, axis=1)
        v_blk = lax.dynamic_slice_in_dim(vg, r0, kr, axis=1)
        s_win = jnp.einsum('bqhd,bikhd->bhqik', q_r, k_blk, preferred_element_type=jnp.float32) * scale
        dr = r0 + jnp.arange(kr) - r + NA_WIN_ROWS - 1
        bias = rpb[:, dr][:, :, dc_idx].astype(jnp.float32)
        s_win = jnp.where(col_mask[None, None, :, None, :], s_win + bias.transpose(0, 2, 1, 3)[None], -jnp.inf)
        s_meta = jnp.einsum('bqhd,bmhd->bhqm', q_r, km, preferred_element_type=jnp.float32) * scale
        s = jnp.concatenate([s_meta, s_win.reshape(B, H, GRID_W, kr * GRID_W)], axis=-1)
        p = jax.nn.softmax(s, axis=-1).astype(v.dtype)
        p_win = p[..., N_META:].reshape(B, H, GRID_W, kr, GRID_W)
        return (jnp.einsum('bhqm,bmhd->bqhd', p[..., :N_META], vm)
                + jnp.einsum('bhqik,bikhd->bqhd', p_win, v_blk))

    o_rows = lax.map(row_block, (jnp.arange(rows), jnp.asarray(row_start, jnp.int32), qg.transpose(1, 0, 2, 3, 4)))
    o_real = o_rows.transpose(1, 0, 2, 3, 4).reshape(B, n, H, dh)
    if not keep_meta:
        return o_real
    s_mm = jnp.einsum('bqhd,bmhd->bhqm', qm, km, preferred_element_type=jnp.float32) * scale
    o_meta = jnp.einsum('bhqm,bmhd->bqhd', jax.nn.softmax(s_mm, axis=-1).astype(v.dtype), vm)
    return jnp.concatenate([o_meta, o_real], axis=1)


def centred_conv(x, w):
    pad = GDN_CONV // 2
    return lax.conv_general_dilated(x, w[:, None, :].astype(x.dtype), window_strides=(1,),
                                    padding=((pad, pad),), dimension_numbers=('NWC', 'WIO', 'NWC'),
                                    feature_group_count=x.shape[-1])


def chunk_gated_delta(q, k, v, g, beta):
    B, T, H, dk = q.shape
    dv = v.shape[-1]
    C = GDN_CHUNK
    nc = T // C

    def chunks(t):
        return jnp.moveaxis(t.reshape((B, nc, C, H) + t.shape[3:]), 3, 1)

    q, k, v, g, beta = chunks(q), chunks(k), chunks(v), chunks(g), chunks(beta)
    gc = jnp.cumsum(g, axis=-1)
    tri = np.tril(np.ones((C, C), bool))
    strict = np.tril(np.ones((C, C), bool), -1)
    decay = jnp.exp(jnp.where(tri, gc[..., :, None] - gc[..., None, :], -jnp.inf))
    kb = k * beta[..., None]
    m = jnp.where(strict, jnp.einsum('bhnid,bhnjd->bhnij', kb, k) * decay, 0.0)
    rhs = jnp.concatenate([v * beta[..., None], kb * jnp.exp(gc)[..., None]], axis=-1)
    sol = lax.linalg.triangular_solve(jnp.eye(C, dtype=q.dtype) + m, rhs, left_side=True,
                                      lower=True, unit_diagonal=True)
    u, w = sol[..., :dv], sol[..., dv:]
    a_qk = jnp.einsum('bhnid,bhnjd->bhnij', q, k) * decay
    q_dec = q * jnp.exp(gc)[..., None]
    k_dec = k * jnp.exp(gc[..., -1:] - gc)[..., None]
    g_last = jnp.exp(gc[..., -1])
    xs = tuple(jnp.moveaxis(t, 2, 0) for t in (u, w, a_qk, q_dec, k_dec, g_last))

    def step(S, inp):
        u_c, w_c, a_c, qd_c, kd_c, gl_c = inp
        v_new = u_c - jnp.einsum('bhcd,bhde->bhce', w_c, S)
        o_c = jnp.einsum('bhcd,bhde->bhce', qd_c, S) + jnp.einsum('bhij,bhje->bhie', a_c, v_new)
        S = S * gl_c[..., None, None] + jnp.einsum('bhcd,bhce->bhde', kd_c, v_new)
        return S, o_c

    _, o = lax.scan(step, jnp.zeros((B, H, dk, dv), q.dtype), xs)
    return jnp.transpose(o, (1, 0, 3, 2, 4)).reshape(B, T, H, dv)


def lead_meta(t, reverse):
    pad = (-N_META) % GDN_CHUNK
    real = t[:, N_META:]
    if reverse:
        real = jnp.flip(real, axis=1)
    zeros = jnp.zeros((t.shape[0], pad) + t.shape[2:], t.dtype)
    return jnp.concatenate([zeros, t[:, :N_META], real], axis=1)


def restore_order(o, reverse):
    o = o[:, (-N_META) % GDN_CHUNK:]
    if reverse:
        o = jnp.concatenate([o[:, :N_META], jnp.flip(o[:, N_META:], axis=1)], axis=1)
    return o


def gated_deltanet(q, k, v, z, a_f, a_b, b_f, b_b, conv_w, a_log, dt_bias, norm_w, keep_meta):
    B, L, _ = q.shape
    H, dh = GDN_HEADS, GDN_HEAD_DIM
    out_dtype = z.dtype
    qkv = jax.nn.silu(centred_conv(jnp.concatenate([q, k, v], axis=-1), conv_w)).astype(jnp.float32)
    q, k, v = jnp.split(qkv, 3, axis=-1)
    q = l2_normalize(q.reshape(B, L, H, dh)) * (dh ** -0.5)
    k = l2_normalize(k.reshape(B, L, H, dh))
    v = v.reshape(B, L, H, dh)
    o = None
    for d, (a_in, b_in, rev) in enumerate(((a_f, b_f, False), (a_b, b_b, True))):
        g = -jnp.exp(a_log[d].astype(jnp.float32)) * jax.nn.softplus(
            a_in.astype(jnp.float32) + dt_bias[d].astype(jnp.float32))
        beta = jax.nn.sigmoid(b_in.astype(jnp.float32))
        o_d = restore_order(chunk_gated_delta(*(lead_meta(t, rev) for t in (q, k, v, g, beta))), rev)
        o = o_d if o is None else o + o_d
    s = 0 if keep_meta else N_META
    o = o[:, s:]
    zh = z[:, s:].reshape(B, L - s, H, dh).astype(jnp.float32)
    o = o * lax.rsqrt(jnp.mean(o * o, -1, keepdims=True) + NORM_EPS) * norm_w.astype(jnp.float32)
    return (o * jax.nn.silu(zh)).reshape(B, L - s, GDN_W).astype(out_dtype)


def token_mixer(h, w_in, rpb, conv_w, a_log, dt_bias, norm_w, w_na, w_gdn, w_out, keep_meta):
    B, L, _ = h.shape
    offsets = np.cumsum(IN_SPLITS)[:-1].tolist()
    (na_q, na_k, na_v, g_q, g_k, g_v, g_z, a_f, a_b, b_f, b_b,
     gate_na, gate_gdn) = jnp.split(h @ w_in, offsets, axis=-1)
    heads = lambda t: t.reshape(B, L, NA_HEADS, NA_HEAD_DIM)
    o_na = neighbourhood_attention(heads(na_q), heads(na_k), heads(na_v), rpb, keep_meta)
    o_gdn = gated_deltanet(g_q, g_k, g_v, g_z, a_f, a_b, b_f, b_b, conv_w, a_log, dt_bias, norm_w, keep_meta)
    s = 0 if keep_meta else N_META
    y_na = o_na.reshape(B, L - s, NA_W) @ w_na
    y_gdn = o_gdn @ w_gdn
    merged = jax.nn.sigmoid(gate_na[:, s:]) * y_na + jax.nn.sigmoid(gate_gdn[:, s:]) * y_gdn
    return merged @ w_out


def expert_dispatch(xt, e_idx, weights, w_gate, w_up, w_down):
    T, D = xt.shape
    A = T * TOP_K
    flat_e = e_idx.reshape(-1)
    flat_tok = jnp.repeat(jnp.arange(T), TOP_K)
    order = jnp.argsort(flat_e)
    se, stok, sw = flat_e[order], flat_tok[order], weights.reshape(-1)[order]
    counts = jnp.bincount(flat_e, length=N_EXPERTS)
    padded = (counts + EXPERT_BLOCK - 1) // EXPERT_BLOCK * EXPERT_BLOCK
    pad_end = jnp.cumsum(padded)
    start = jnp.cumsum(counts) - counts
    dest = (pad_end - padded)[se] + jnp.arange(A) - start[se]
    n_blocks = -(-(A + N_EXPERTS * (EXPERT_BLOCK - 1)) // EXPERT_BLOCK)
    x_pad = jnp.zeros((n_blocks * EXPERT_BLOCK, D), xt.dtype).at[dest].set(xt[stok])
    block_e = jnp.minimum(jnp.searchsorted(pad_end, jnp.arange(n_blocks) * EXPERT_BLOCK, side='right'),
                          N_EXPERTS - 1)

    def run(args):
        xb, e = args
        hdn = jax.nn.silu(xb @ w_gate[e]) * (xb @ w_up[e])
        return hdn @ w_down[e]

    y_pad = lax.map(run, (x_pad.reshape(n_blocks, EXPERT_BLOCK, D), block_e)).reshape(-1, D)
    return jnp.zeros((T, D), jnp.float32).at[stok].add(y_pad[dest].astype(jnp.float32) * sw[:, None])


def hierarchical_moe(x, w_group, b_group, w_expert, b_expert, w_gate, w_up, w_down):
    B, L, D = x.shape
    xt = x.reshape(-1, D)
    gl = (xt @ w_group).astype(jnp.float32) + b_group.astype(jnp.float32)
    pg = jax.nn.softmax(gl, axis=-1)
    _, g_top = lax.top_k(gl, 1)
    g_idx = g_top[:, 0]
    g_w = jnp.take_along_axis(pg, g_top, axis=-1)[:, 0]
    el = ((xt @ w_expert).astype(jnp.float32) + b_expert.astype(jnp.float32)).reshape(-1, N_GROUPS, EXPERTS_PER_GROUP)
    el_sel = jnp.take_along_axis(el, g_idx[:, None, None], axis=1)[:, 0]
    top_l, top_i = lax.top_k(el_sel, TOP_K)
    top_w = jax.nn.softmax(top_l, axis=-1) * g_w[:, None]
    e_idx = g_idx[:, None] * EXPERTS_PER_GROUP + top_i
    y = expert_dispatch(xt, e_idx, top_w, w_gate, w_up, w_down)
    return y.astype(x.dtype).reshape(B, L, D)


def setup_inputs(seed: int = 0) -> dict:
    key = jax.random.key(seed)
    ks = jax.random.split(key, 24)
    f32 = jnp.float32
    D = D_MODEL
    nrm = lambda k, shape, s: jax.random.normal(k, shape, f32) * s
    col_scale = jnp.asarray(np.concatenate(
        [np.full((w,), DN_BETA if i in VALUE_BLOCKS else 1.0, np.float32) for i, w in enumerate(IN_SPLITS)]))
    dt = jnp.exp(jax.random.uniform(ks[8], (DEPTH, 2, GDN_HEADS), f32, math.log(1e-3), math.log(1e-1)))
    return {
        "x": jax.random.normal(ks[0], (BATCH, SEQ, D), f32),
        "meta_tokens": nrm(ks[1], (N_META, D), 1.0),
        "ln0_g": 1.0 + nrm(ks[2], (D,), 0.02),
        "ln0_b": nrm(ks[3], (D,), 0.02),
        "w_in": nrm(ks[4], (DEPTH, D, D_IN), D ** -0.5) * col_scale,
        "na_rpb": nrm(ks[5], (DEPTH, NA_HEADS, 2 * NA_WIN_ROWS - 1, 2 * NA_WIN_COLS - 1), 0.1),
        "gdn_conv_w": nrm(ks[6], (DEPTH, GDN_CONV, 3 * GDN_W), GDN_CONV ** -0.5),
        "gdn_a_log": jnp.log(jax.random.uniform(ks[7], (DEPTH, 2, GDN_HEADS), f32, 1.0, 16.0)),
        "gdn_dt_bias": dt + jnp.log(-jnp.expm1(-dt)),
        "gdn_norm_w": 1.0 + nrm(ks[9], (DEPTH, GDN_HEAD_DIM), 0.02),
        "w_branch_na": nrm(ks[10], (DEPTH, NA_W, D), NA_W ** -0.5 * DN_BETA),
        "w_branch_gdn": nrm(ks[11], (DEPTH, GDN_W, D), GDN_W ** -0.5 * DN_BETA),
        "w_out": nrm(ks[12], (DEPTH, D, D), D ** -0.5 * DN_BETA),
        "ln1_g": 1.0 + nrm(ks[13], (DEPTH, D), 0.02),
        "ln1_b": nrm(ks[14], (DEPTH, D), 0.02),
        "router_group_w": nrm(ks[15], (DEPTH, D, N_GROUPS), D ** -0.5),
        "router_group_b": nrm(ks[16], (DEPTH, N_GROUPS), 0.01),
        "router_expert_w": nrm(ks[17], (DEPTH, D, N_EXPERTS), D ** -0.5),
        "router_expert_b": nrm(ks[18], (DEPTH, N_EXPERTS), 0.01),
        "expert_w_gate": nrm(ks[19], (DEPTH, N_EXPERTS, D, D_EXPERT), D ** -0.5),
        "expert_w_up": nrm(ks[20], (DEPTH, N_EXPERTS, D, D_EXPERT), D ** -0.5 * DN_BETA),
        "expert_w_down": nrm(ks[21], (DEPTH, N_EXPERTS, D_EXPERT, D), D_EXPERT ** -0.5 * DN_BETA),
        "ln2_g": 1.0 + nrm(ks[22], (DEPTH, D), 0.02),
        "ln2_b": nrm(ks[23], (DEPTH, D), 0.02),
    }


def reference(x, meta_tokens, ln0_g, ln0_b, w_in, na_rpb, gdn_conv_w, gdn_a_log, gdn_dt_bias,
              gdn_norm_w, w_branch_na, w_branch_gdn, w_out, ln1_g, ln1_b, router_group_w,
              router_group_b, router_expert_w, router_expert_b, expert_w_gate, expert_w_up,
              expert_w_down, ln2_g, ln2_b):
    B = x.shape[0]
    meta = jnp.broadcast_to(meta_tokens.astype(x.dtype)[None], (B, N_META, D_MODEL))
    h = layer_norm(jnp.concatenate([meta, x], axis=1), ln0_g, ln0_b)
    for l in range(DEPTH):
        keep_meta = l < DEPTH - 1
        mix = token_mixer(h, w_in[l], na_rpb[l], gdn_conv_w[l], gdn_a_log[l], gdn_dt_bias[l],
                          gdn_norm_w[l], w_branch_na[l], w_branch_gdn[l], w_out[l], keep_meta)
        if not keep_meta:
            h = h[:, N_META:]
        h = layer_norm(DN_ALPHA * h + mix, ln1_g[l], ln1_b[l])
        ffn = hierarchical_moe(h, router_group_w[l], router_group_b[l], router_expert_w[l],
                               router_expert_b[l], expert_w_gate[l], expert_w_up[l], expert_w_down[l])
        h = layer_norm(DN_ALPHA * h + ffn, ln2_g[l], ln2_b[l])
    return h
```

```python
import functools

import numpy as np
import jax
import jax.numpy as jnp
from jax import lax
from jax.experimental import pallas as pl
from jax.experimental.pallas import tpu as pltpu
from jax.experimental.pallas import tpu_sc as plsc

D_MODEL = 1024
DEPTH = 1
GRID_W = 64
N_META = 16
NA_HEADS = 8
NA_HEAD_DIM = 64
NA_W = NA_HEADS * NA_HEAD_DIM
NA_WIN_ROWS = 8
NA_WIN_COLS = 16
GDN_HEADS = 8
GDN_HEAD_DIM = 128
GDN_W = GDN_HEADS * GDN_HEAD_DIM
GDN_CONV = 5
GDN_CHUNK = 64
N_GROUPS = 4
EXPERTS_PER_GROUP = 8
N_EXPERTS = N_GROUPS * EXPERTS_PER_GROUP
TOP_K = 2
D_EXPERT = 256
LN_EPS = 1e-5
NORM_EPS = 1e-6
DN_ALPHA = (2 * DEPTH) ** 0.25

LANES = 128
VMEM_LIMIT = 48 * 1024 * 1024
NEG = -1e30

CB_G_Q, CB_G_K, CB_G_V, CB_G_Z = 0, 8, 16, 24
CB_GATE_NA, CB_GATE_GDN = 32, 40
CB_NA_Q, CB_NA_K, CB_NA_V = 48, 52, 56
P_COLS = 60 * LANES
AB_OFF = 3 * NA_W + 4 * GDN_W
N_AB = 4 * GDN_HEADS

EXPERT_ROWS = 256
GATHER_WINDOW = 64

f32 = jnp.float32
bf16 = jnp.bfloat16


def _dot(a, b):
    return jnp.dot(a.astype(bf16), b.astype(bf16), preferred_element_type=f32)


def _dot_nt(a, b):
    return lax.dot_general(a.astype(bf16), b.astype(bf16), (((1,), (1,)), ((), ())),
                           preferred_element_type=f32)


def _dot_tn(a, b):
    return lax.dot_general(a.astype(bf16), b.astype(bf16), (((0,), (0,)), ((), ())),
                           preferred_element_type=f32)


def _split_bf16(x):
    hi = x.astype(bf16)
    lo = (x - hi.astype(f32)).astype(bf16)
    return hi, lo


def _dot3(x, w):
    xh, xl = _split_bf16(x)
    wh, wl = _split_bf16(w)
    d = lambda a, b: jnp.dot(a, b, preferred_element_type=f32)
    return d(xh, wh) + (d(xl, wh) + d(xh, wl))


def _layer_norm(x, g, b):
    xc = x - jnp.mean(x, -1, keepdims=True)
    var = jnp.mean(xc * xc, -1, keepdims=True)
    return xc * lax.rsqrt(var + LN_EPS) * g + b


def _sigmoid(x):
    return 1.0 / (1.0 + jnp.exp(-x))


def _silu(x):
    return x * _sigmoid(x)


def _params(*sem):
    return pltpu.CompilerParams(dimension_semantics=sem, vmem_limit_bytes=VMEM_LIMIT)


def _in_proj_kernel(x_ref, g_ref, b_ref, w_ref, wab_ref, p_ref, ab_ref, hn_ref):
    @pl.when(pl.program_id(1) == 0)
    def _():
        h = _layer_norm(x_ref[...], g_ref[...], b_ref[...])
        hn_ref[...] = h.astype(bf16)
        ab_ref[...] = _dot3(h, wab_ref[...])

    p_ref[...] = jnp.dot(hn_ref[...], w_ref[...], preferred_element_type=f32).astype(bf16)


def _in_proj(x2d, ln_g, ln_b, w_main, w_ab, tm, tn):
    t = x2d.shape[0]
    return pl.pallas_call(
        _in_proj_kernel,
        grid=(t // tm, P_COLS // tn),
        in_specs=[
            pl.BlockSpec((tm, D_MODEL), lambda i, j: (i, 0)),
            pl.BlockSpec((1, D_MODEL), lambda i, j: (0, 0)),
            pl.BlockSpec((1, D_MODEL), lambda i, j: (0, 0)),
            pl.BlockSpec((D_MODEL, tn), lambda i, j: (0, j)),
            pl.BlockSpec((D_MODEL, LANES), lambda i, j: (0, 0)),
        ],
        out_specs=[
            pl.BlockSpec((tm, tn), lambda i, j: (i, j)),
            pl.BlockSpec((tm, LANES), lambda i, j: (i, 0)),
        ],
        out_shape=[
            jax.ShapeDtypeStruct((t, P_COLS), bf16),
            jax.ShapeDtypeStruct((t, LANES), f32),
        ],
        scratch_shapes=[pltpu.VMEM((tm, D_MODEL), bf16)],
        compiler_params=_params("parallel", "arbitrary"),
        name="in_proj",
    )(x2d, ln_g, ln_b, w_main, w_ab)


def _na_kernel(q_ref, k_ref, v_ref, km_ref, vm_ref, bias_ref, o_ref, *, rows):
    scale = NA_HEAD_DIM ** -0.5
    kr = NA_WIN_ROWS
    lane = lax.broadcasted_iota(jnp.int32, (GRID_W, LANES), 1)
    low = lane < NA_HEAD_DIM
    half_mask = (jnp.where(low, 1.0, 0.0).astype(bf16), jnp.where(low, 0.0, 1.0).astype(bf16))

    def row_block(r, carry):
        r0 = jnp.clip(r - kr // 2, 0, rows - kr)
        d0 = r0 - r + (NA_WIN_ROWS - 1)
        qoff = pl.multiple_of(r * GRID_W, GRID_W)
        koff = pl.multiple_of(r0 * GRID_W, GRID_W)
        for p in range(NA_HEADS // 2):
            sl = slice(LANES * p, LANES * (p + 1))
            qp = q_ref[pl.ds(qoff, GRID_W), sl]
            kp = k_ref[pl.ds(koff, kr * GRID_W), sl]
            vp = v_ref[pl.ds(koff, kr * GRID_W), sl]
            kmp = km_ref[:, sl]
            vmp = vm_ref[:, sl]
            outs = []
            for half in range(2):
                qh = qp * half_mask[half]
                s = _dot_nt(qh, kp) * scale + bias_ref[2 * p + half, d0]
                sm = _dot_nt(qh, kmp) * scale
                m = jnp.maximum(jnp.max(s, -1, keepdims=True), jnp.max(sm, -1, keepdims=True))
                e = jnp.exp(s - m)
                em = jnp.exp(sm - m)
                den = jnp.sum(e, -1, keepdims=True) + jnp.sum(em, -1, keepdims=True)
                outs.append((_dot(e, vp) + _dot(em, vmp)) / den)
            o_ref[pl.ds(qoff, GRID_W), sl] = jnp.where(low, outs[0], outs[1]).astype(bf16)
        return carry

    lax.fori_loop(0, rows, row_block, 0)


def _na_bias_table(rpb):
    col = np.arange(GRID_W)
    col_start = np.clip(col - NA_WIN_COLS // 2, 0, GRID_W - NA_WIN_COLS)
    col_mask = (col[None, :] >= col_start[:, None]) & (col[None, :] < col_start[:, None] + NA_WIN_COLS)
    dc_idx = np.clip(col[None, :] - col[:, None] + NA_WIN_COLS - 1, 0, 2 * NA_WIN_COLS - 2)
    dr = np.arange(NA_WIN_ROWS)[:, None] + np.arange(NA_WIN_ROWS)[None, :]
    tbl = rpb.astype(f32)[:, dr][:, :, :, dc_idx]
    tbl = jnp.where(col_mask[None, None, None], tbl, NEG)
    return tbl.transpose(0, 1, 3, 2, 4).reshape(NA_HEADS, NA_WIN_ROWS, GRID_W, NA_WIN_ROWS * GRID_W)


def _na(p_real, p_meta, bias_tbl, batch, seq):
    rows = seq // GRID_W
    assert rows >= NA_WIN_ROWS
    col = lambda cb: (lambda b: (b, cb // 4))
    colm = lambda cb: (lambda b: (0, cb // 4))
    return pl.pallas_call(
        functools.partial(_na_kernel, rows=rows),
        grid=(batch,),
        in_specs=[
            pl.BlockSpec((seq, NA_W), col(CB_NA_Q)),
            pl.BlockSpec((seq, NA_W), col(CB_NA_K)),
            pl.BlockSpec((seq, NA_W), col(CB_NA_V)),
            pl.BlockSpec((N_META, NA_W), colm(CB_NA_K)),
            pl.BlockSpec((N_META, NA_W), colm(CB_NA_V)),
            pl.BlockSpec(bias_tbl.shape, lambda b: (0, 0, 0, 0)),
        ],
        out_specs=pl.BlockSpec((seq, NA_W), lambda b: (b, 0)),
        out_shape=jax.ShapeDtypeStruct((batch * seq, NA_W), bf16),
        compiler_params=_params("parallel"),
        name="na",
    )(p_real, p_real, p_real, p_meta, p_meta, bias_tbl)


GDN_LEAD = GDN_CHUNK - N_META
RAW_PAD = 8


def _gdn_chunk(qc, kc, vc, g_col, beta_col, s_state, lower):
    c = GDN_CHUNK
    ii = lax.broadcasted_iota(jnp.int32, (c, c), 0)
    jj = lax.broadcasted_iota(jnp.int32, (c, c), 1)
    eye = ii == jj
    if lower:
        incl, strict, incl_t = jj <= ii, jj < ii, ii <= jj
    else:
        incl, strict, incl_t = jj >= ii, jj > ii, ii >= jj
    g_row = jnp.sum(jnp.where(eye, g_col, 0.0), axis=0, keepdims=True)
    gc_col = jnp.sum(jnp.where(incl, g_row, 0.0), axis=1, keepdims=True)
    gc_row = jnp.sum(jnp.where(incl_t, g_col, 0.0), axis=0, keepdims=True)
    total = jnp.sum(g_col, axis=0, keepdims=True)
    decay = jnp.where(incl, jnp.exp(jnp.minimum(gc_col - gc_row, 0.0)), 0.0)
    e_col = jnp.exp(gc_col)
    kd_col = jnp.exp(total - gc_col)
    g_last = jnp.exp(total)

    kb = kc * beta_col
    n = jnp.where(strict, -(_dot_nt(kb, kc) * decay), 0.0)
    a = jnp.where(eye, 1.0, 0.0) + n
    pw = n
    for _ in range(5):
        pw = _dot(pw, pw)
        a = a + _dot(a, pw)
    sol = _dot(a, jnp.concatenate([vc * beta_col, kb * e_col], axis=1))
    u, w = sol[:, :GDN_HEAD_DIM], sol[:, GDN_HEAD_DIM:]
    a_qk = _dot_nt(qc, kc) * decay
    ws = _dot(jnp.concatenate([w, qc * e_col], axis=0), s_state)
    v_new = u - ws[:c]
    o = ws[c:] + _dot(a_qk, v_new)
    s_new = s_state * g_last + _dot_tn(kc * kd_col, v_new)
    return o, s_new


def _gdn_kernel(alog_ref, dtb_ref, q_ref, k_ref, v_ref, z_ref, qm_ref, km_ref, vm_ref,
                wq_ref, wk_ref, wv_ref, gt_ref, gtm_ref, nw_ref, o_ref,
                raw_q, raw_k, raw_v, cq, ck, cv, oacc, *, seq):
    c = GDN_CHUNK
    nreal = seq // c
    h = pl.program_id(1)
    raws = (raw_q, raw_k, raw_v)

    lead = RAW_PAD + GDN_LEAD
    for raw, m_ref in zip(raws, (qm_ref, km_ref, vm_ref)):
        raw[pl.ds(0, lead), :] = jnp.zeros((lead, LANES), f32)
        raw[pl.ds(lead, N_META), :] = m_ref[...].astype(f32)
        raw[pl.ds(lead + N_META + seq, RAW_PAD), :] = jnp.zeros((RAW_PAD, LANES), f32)

    def copy_in(j, carry):
        src = pl.multiple_of(j * c, c)
        dst = pl.multiple_of(lead + N_META + j * c, 8)
        for raw, r_ref in zip(raws, (q_ref, k_ref, v_ref)):
            raw[pl.ds(dst, c), :] = r_ref[pl.ds(src, c), :].astype(f32)
        return carry

    lax.fori_loop(0, nreal, copy_in, 0)

    row = lax.broadcasted_iota(jnp.int32, (c, LANES), 0)
    halo = GDN_CONV // 2

    def conv_chunk(j, carry):
        p0 = pl.multiple_of(j * c, c)
        live = (row + p0) >= GDN_LEAD
        for raw, w_ref, dst, kind in zip(raws, (wq_ref, wk_ref, wv_ref), (cq, ck, cv), "qkv"):
            win = raw[pl.ds(p0, c + 2 * RAW_PAD), :]
            w = w_ref[...]
            acc = None
            for t in range(GDN_CONV):
                off = RAW_PAD - halo + t
                term = win[off:off + c, :] * w[t:t + 1, :]
                acc = term if acc is None else acc + term
            y = _silu(acc)
            if kind != "v":
                y = y * lax.rsqrt(jnp.sum(y * y, -1, keepdims=True) + NORM_EPS)
            if kind == "q":
                y = y * (GDN_HEAD_DIM ** -0.5)
            dst[pl.ds(p0, c), :] = jnp.where(live, y, 0.0)
        return carry

    lax.fori_loop(0, nreal + 1, conv_chunk, 0)

    def gates(a_col, b_col, d):
        coef = -jnp.exp(jnp.full((1, 1), alog_ref[d, h], f32))
        x = a_col + dtb_ref[d, h]
        softplus = jnp.maximum(x, 0.0) + jnp.log1p(jnp.exp(-jnp.abs(x)))
        return coef * softplus, _sigmoid(b_col)

    def load_chunk(p0):
        return cq[pl.ds(p0, c), :], ck[pl.ds(p0, c), :], cv[pl.ds(p0, c), :]

    gm = gtm_ref[...]
    live_col = lax.broadcasted_iota(jnp.int32, (c, 1), 0) >= GDN_LEAD
    q0, k0, v0 = load_chunk(0)
    states = []
    for d in range(2):
        g_col, beta_col = gates(gm[:, d:d + 1], gm[:, 2 + d:3 + d], d)
        g_col = jnp.where(live_col, g_col, 0.0)
        beta_col = jnp.where(live_col, beta_col, 0.0)
        _, s0 = _gdn_chunk(q0, k0, v0, g_col, beta_col, jnp.zeros((GDN_HEAD_DIM, GDN_HEAD_DIM), f32), True)
        states.append(s0)

    def zero_acc(j, carry):
        oacc[pl.ds(pl.multiple_of(j * c, c), c), :] = jnp.zeros((c, LANES), f32)
        return carry

    lax.fori_loop(0, nreal, zero_acc, 0)

    def scan_step(j, carry):
        s_f, s_b = carry
        for d, s_state in ((0, s_f), (1, s_b)):
            jr = j if d == 0 else nreal - 1 - j
            r0 = pl.multiple_of(jr * c, c)
            gt = gt_ref[pl.ds(r0, c), :]
            g_col, beta_col = gates(gt[:, d:d + 1], gt[:, 2 + d:3 + d], d)
            qc, kc, vc = load_chunk(pl.multiple_of(r0 + c, c))
            o, s_new = _gdn_chunk(qc, kc, vc, g_col, beta_col, s_state, d == 0)
            oacc[pl.ds(r0, c), :] += o
            if d == 0:
                s_f = s_new
            else:
                s_b = s_new
        return s_f, s_b

    lax.fori_loop(0, nreal, scan_step, (states[0], states[1]))

    def finish(j, carry):
        r0 = pl.multiple_of(j * c, c)
        o = oacc[pl.ds(r0, c), :]
        o = o * lax.rsqrt(jnp.mean(o * o, -1, keepdims=True) + NORM_EPS) * nw_ref[...]
        o_ref[pl.ds(r0, c), :] = (o * _silu(z_ref[pl.ds(r0, c), :].astype(f32))).astype(bf16)
        return carry

    lax.fori_loop(0, nreal, finish, 0)


def _gdn(p_real, p_meta, conv_w, gates_real, gates_meta, a_log, dt_bias, norm_w, batch, seq):
    col = lambda cb: (lambda b, h: (b, cb + h))
    colm = lambda cb: (lambda b, h: (0, cb + h))
    colw = lambda k: (lambda b, h: (0, k * GDN_HEADS + h))
    smem = pl.BlockSpec(memory_space=pltpu.SMEM)
    padded = seq + GDN_CHUNK
    return pl.pallas_call(
        functools.partial(_gdn_kernel, seq=seq),
        grid=(batch, GDN_HEADS),
        in_specs=[
            smem, smem,
            pl.BlockSpec((seq, LANES), col(CB_G_Q)),
            pl.BlockSpec((seq, LANES), col(CB_G_K)),
            pl.BlockSpec((seq, LANES), col(CB_G_V)),
            pl.BlockSpec((seq, LANES), col(CB_G_Z)),
            pl.BlockSpec((N_META, LANES), colm(CB_G_Q)),
            pl.BlockSpec((N_META, LANES), colm(CB_G_K)),
            pl.BlockSpec((N_META, LANES), colm(CB_G_V)),
            pl.BlockSpec((GDN_CONV, LANES), colw(0)),
            pl.BlockSpec((GDN_CONV, LANES), colw(1)),
            pl.BlockSpec((GDN_CONV, LANES), colw(2)),
            pl.BlockSpec((None, None, seq, 4), lambda b, h: (b, h, 0, 0)),
            pl.BlockSpec((None, GDN_CHUNK, 4), lambda b, h: (h, 0, 0)),
            pl.BlockSpec((1, LANES), lambda b, h: (0, 0)),
        ],
        out_specs=pl.BlockSpec((seq, LANES), lambda b, h: (b, h)),
        out_shape=jax.ShapeDtypeStruct((batch * seq, GDN_W), bf16),
        scratch_shapes=[pltpu.VMEM((padded + 2 * RAW_PAD, LANES), f32)] * 3
        + [pltpu.VMEM((padded, LANES), f32)] * 3
        + [pltpu.VMEM((seq, LANES), f32)],
        compiler_params=_params("parallel", "parallel"),
        name="gdn",
    )(a_log, dt_bias, p_real, p_real, p_real, p_real, p_meta, p_meta, p_meta,
      conv_w, conv_w, conv_w, gates_real, gates_meta, norm_w)


def _route(logits):
    lane = lax.broadcasted_iota(jnp.int32, logits.shape, 1)
    big = jnp.int32(1 << 20)
    is_g = (lane >= N_EXPERTS) & (lane < N_EXPERTS + N_GROUPS)
    gl = jnp.where(is_g, logits, -jnp.inf)
    gmax = jnp.max(gl, -1, keepdims=True)
    g_idx = jnp.min(jnp.where(gl == gmax, lane - N_EXPERTS, big), -1, keepdims=True)
    g_w = 1.0 / jnp.sum(jnp.where(is_g, jnp.exp(gl - gmax), 0.0), -1, keepdims=True)
    in_grp = (lane < N_EXPERTS) & (jnp.right_shift(lane, 3) == g_idx)
    el = jnp.where(in_grp, logits, -jnp.inf)
    m1 = jnp.max(el, -1, keepdims=True)
    i1 = jnp.min(jnp.where(el == m1, lane, big), -1, keepdims=True)
    el2 = jnp.where(lane == i1, -jnp.inf, el)
    m2 = jnp.max(el2, -1, keepdims=True)
    i2 = jnp.min(jnp.where(el2 == m2, lane, big), -1, keepdims=True)
    e2 = jnp.exp(m2 - m1)
    w1 = g_w / (1.0 + e2)
    w2 = g_w * e2 / (1.0 + e2)
    out = jnp.where(lane == 0, i1.astype(f32), 0.0)
    out = jnp.where(lane == 1, i2.astype(f32), out)
    out = jnp.where(lane == 2, w1, out)
    return jnp.where(lane == 3, w2, out)


def _merge_kernel(x_ref, ona_ref, ogdn_ref, gna_ref, ggdn_ref, ln0g_ref, ln0b_ref, wna_ref, wgdn_ref,
                  wout_ref, ln1g_ref, ln1b_ref, wr_ref, br_ref, h1_ref, route_ref):
    h = _layer_norm(x_ref[...], ln0g_ref[...], ln0b_ref[...])
    y_na = jnp.dot(ona_ref[...], wna_ref[...], preferred_element_type=f32)
    y_gdn = jnp.dot(ogdn_ref[...], wgdn_ref[...], preferred_element_type=f32)
    merged = _sigmoid(gna_ref[...].astype(f32)) * y_na + _sigmoid(ggdn_ref[...].astype(f32)) * y_gdn
    mix = jnp.dot(merged.astype(bf16), wout_ref[...], preferred_element_type=f32)
    h1 = _layer_norm(DN_ALPHA * h + mix, ln1g_ref[...], ln1b_ref[...])
    h1_ref[...] = h1
    route_ref[...] = _route(_dot3(h1, wr_ref[...]) + br_ref[...])


def _merge(x2d, o_na, o_gdn, p_real, ln0_g, ln0_b, w_na, w_gdn, w_out, ln1_g, ln1_b, w_route, b_route, tm):
    t = x2d.shape[0]
    row = lambda i: (i, 0)
    const = lambda i: (0, 0)
    return pl.pallas_call(
        _merge_kernel,
        grid=(t // tm,),
        in_specs=[
            pl.BlockSpec((tm, D_MODEL), row),
            pl.BlockSpec((tm, NA_W), row),
            pl.BlockSpec((tm, GDN_W), row),
            pl.BlockSpec((tm, D_MODEL), lambda i: (i, CB_GATE_NA // 8)),
            pl.BlockSpec((tm, D_MODEL), lambda i: (i, CB_GATE_GDN // 8)),
            pl.BlockSpec((1, D_MODEL), const),
            pl.BlockSpec((1, D_MODEL), const),
            pl.BlockSpec((NA_W, D_MODEL), const),
            pl.BlockSpec((GDN_W, D_MODEL), const),
            pl.BlockSpec((D_MODEL, D_MODEL), const),
            pl.BlockSpec((1, D_MODEL), const),
            pl.BlockSpec((1, D_MODEL), const),
            pl.BlockSpec((D_MODEL, LANES), const),
            pl.BlockSpec((1, LANES), const),
        ],
        out_specs=[pl.BlockSpec((tm, D_MODEL), row), pl.BlockSpec((tm, LANES), row)],
        out_shape=[jax.ShapeDtypeStruct((t, D_MODEL), f32), jax.ShapeDtypeStruct((t, LANES), f32)],
        compiler_params=_params("parallel"),
        name="merge",
    )(x2d, o_na, o_gdn, p_real, p_real, ln0_g, ln0_b, w_na, w_gdn, w_out, ln1_g, ln1_b, w_route, b_route)


def _sc_gather(table, idx):
    n = idx.shape[0]
    width = table.shape[1]
    mesh = plsc.VectorSubcoreMesh(core_axis_name="core", subcore_axis_name="subcore")
    n_workers = mesh.num_cores * mesh.num_subcores
    per_worker = n // n_workers
    assert n % n_workers == 0 and per_worker % GATHER_WINDOW == 0

    @functools.partial(
        pl.kernel, out_type=jax.ShapeDtypeStruct((n, width), table.dtype), mesh=mesh,
        scratch_types=[pltpu.VMEM((GATHER_WINDOW,), jnp.int32), pltpu.VMEM((GATHER_WINDOW, width), table.dtype)],
        name="row_gather")
    def gather(tbl_hbm, idx_hbm, out_hbm, idx_vmem, rows_vmem):
        worker = lax.axis_index("subcore") * mesh.num_cores + lax.axis_index("core")

        @pl.loop(0, per_worker // GATHER_WINDOW)
        def _(step):
            base = pl.multiple_of(worker * per_worker + step * GATHER_WINDOW, GATHER_WINDOW)
            pltpu.sync_copy(idx_hbm.at[pl.ds(base, GATHER_WINDOW)], idx_vmem)
            pltpu.sync_copy(tbl_hbm.at[idx_vmem], rows_vmem)
            pltpu.sync_copy(rows_vmem, out_hbm.at[pl.ds(base, GATHER_WINDOW)])

    return gather(table, idx)


def _expert_kernel(be_ref, na_ref, x_ref, wg_ref, wu_ref, wd_ref, y_ref):
    active = pl.program_id(0) < na_ref[0]

    @pl.when(jnp.logical_not(active))
    def _():
        y_ref[...] = jnp.zeros_like(y_ref)

    @pl.when(active)
    def _():
        x = x_ref[...].astype(bf16)
        gate = jnp.dot(x, wg_ref[...].astype(bf16), preferred_element_type=f32)
        up = jnp.dot(x, wu_ref[...].astype(bf16), preferred_element_type=f32)
        hdn = (_silu(gate) * up).astype(bf16)
        y_ref[...] = jnp.dot(hdn, wd_ref[...].astype(bf16), preferred_element_type=f32)


def _experts(x_pad, block_e, n_active, w_gate, w_up, w_down):
    nb = x_pad.shape[0] // EXPERT_ROWS
    grid_spec = pltpu.PrefetchScalarGridSpec(
        num_scalar_prefetch=2,
        grid=(nb,),
        in_specs=[
            pl.BlockSpec((EXPERT_ROWS, D_MODEL), lambda i, be, na: (i, 0)),
            pl.BlockSpec((None, D_MODEL, D_EXPERT), lambda i, be, na: (be[i], 0, 0)),
            pl.BlockSpec((None, D_MODEL, D_EXPERT), lambda i, be, na: (be[i], 0, 0)),
            pl.BlockSpec((None, D_EXPERT, D_MODEL), lambda i, be, na: (be[i], 0, 0)),
        ],
        out_specs=pl.BlockSpec((EXPERT_ROWS, D_MODEL), lambda i, be, na: (i, 0)),
    )
    return pl.pallas_call(
        _expert_kernel,
        grid_spec=grid_spec,
        out_shape=jax.ShapeDtypeStruct(x_pad.shape, f32),
        compiler_params=_params("arbitrary"),
        name="experts",
    )(block_e, n_active, x_pad, w_gate, w_up, w_down)


def _combine_kernel(h1_ref, y_ref, route_ref, g_ref, b_ref, o_ref):
    r = route_ref[...]
    y = y_ref[...]
    ffn = y[:, :D_MODEL] * r[:, 2:3] + y[:, D_MODEL:] * r[:, 3:4]
    o_ref[...] = _layer_norm(DN_ALPHA * h1_ref[...] + ffn, g_ref[...], b_ref[...])


def _combine(h1, y_pairs, route, ln_g, ln_b, tm):
    t = h1.shape[0]
    row = lambda i: (i, 0)
    const = lambda i: (0, 0)
    return pl.pallas_call(
        _combine_kernel,
        grid=(t // tm,),
        in_specs=[
            pl.BlockSpec((tm, D_MODEL), row),
            pl.BlockSpec((tm, TOP_K * D_MODEL), row),
            pl.BlockSpec((tm, LANES), row),
            pl.BlockSpec((1, D_MODEL), const),
            pl.BlockSpec((1, D_MODEL), const),
        ],
        out_specs=pl.BlockSpec((tm, D_MODEL), row),
        out_shape=jax.ShapeDtypeStruct((t, D_MODEL), f32),
        compiler_params=_params("parallel"),
        name="combine",
    )(h1, y_pairs, route, ln_g, ln_b)


def _dispatch_plan(e_idx, t):
    a = t * TOP_K
    nb = a // EXPERT_ROWS + N_EXPERTS
    flat_e = e_idx.reshape(-1)
    onehot = (flat_e[:, None] == jnp.arange(N_EXPERTS, dtype=jnp.int32)[None, :]).astype(jnp.int32)
    csum = jnp.cumsum(onehot, axis=0)
    rank = jnp.sum(csum * onehot, axis=1) - 1
    counts = csum[-1]
    padded = (counts + EXPERT_ROWS - 1) // EXPERT_ROWS * EXPERT_ROWS
    pad_end = jnp.cumsum(padded)
    dest = (pad_end - padded)[flat_e] + rank
    flat_tok = jnp.arange(a, dtype=jnp.int32) // TOP_K
    src_tok = jnp.zeros((nb * EXPERT_ROWS,), jnp.int32).at[dest].set(flat_tok)
    block_start = jnp.arange(nb, dtype=jnp.int32) * EXPERT_ROWS
    block_e = jnp.minimum(jnp.searchsorted(pad_end, block_start, side="right"), N_EXPERTS - 1).astype(jnp.int32)
    n_active = (pad_end[-1:] // EXPERT_ROWS).astype(jnp.int32)
    return src_tok, dest.astype(jnp.int32), block_e, n_active


def _row_tile(t, want):
    tm = min(t, want)
    assert t % tm == 0
    return tm


def kernel(x, meta_tokens, ln0_g, ln0_b, w_in, na_rpb, gdn_conv_w, gdn_a_log, gdn_dt_bias, gdn_norm_w,
           w_branch_na, w_branch_gdn, w_out, ln1_g, ln1_b, router_group_w, router_group_b, router_expert_w,
           router_expert_b, expert_w_gate, expert_w_up, expert_w_down, ln2_g, ln2_b):
    batch, seq, d = x.shape
    assert d == D_MODEL and seq % GRID_W == 0 and DEPTH == 1
    t = batch * seq
    l = 0
    x2d = x.reshape(t, d)
    vec = lambda v: v.reshape(1, -1).astype(f32)

    w = w_in[l]
    w_main = jnp.concatenate([w[:, 3 * NA_W:AB_OFF], w[:, AB_OFF + N_AB:], w[:, :3 * NA_W]], axis=1).astype(bf16)
    w_ab = jnp.pad(w[:, AB_OFF:AB_OFF + N_AB].astype(f32), ((0, 0), (0, LANES - N_AB)))

    p_real, ab_real = _in_proj(x2d, vec(ln0_g), vec(ln0_b), w_main, w_ab, _row_tile(t, 1024), 1280)
    p_meta, ab_meta = _in_proj(meta_tokens.astype(f32), vec(ln0_g), vec(ln0_b), w_main, w_ab, N_META, 1280)

    o_na = _na(p_real, p_meta, _na_bias_table(na_rpb[l]), batch, seq)

    gates_real = ab_real[:, :N_AB].reshape(batch, seq, 4, GDN_HEADS).transpose(0, 3, 1, 2)
    gates_meta = ab_meta[:, :N_AB].reshape(N_META, 4, GDN_HEADS).transpose(2, 0, 1)
    gates_meta = jnp.pad(gates_meta, ((0, 0), (GDN_LEAD, 0), (0, 0)))
    o_gdn = _gdn(p_real, p_meta, gdn_conv_w[l].astype(f32), gates_real, gates_meta, gdn_a_log[l].astype(f32),
                 gdn_dt_bias[l].astype(f32), vec(gdn_norm_w[l]), batch, seq)

    w_route = jnp.pad(jnp.concatenate([router_expert_w[l], router_group_w[l]], axis=1).astype(f32),
                      ((0, 0), (0, LANES - N_EXPERTS - N_GROUPS)))
    b_route = jnp.pad(jnp.concatenate([router_expert_b[l], router_group_b[l]]).astype(f32),
                      (0, LANES - N_EXPERTS - N_GROUPS)).reshape(1, LANES)
    h1, route = _merge(x2d, o_na, o_gdn, p_real, vec(ln0_g), vec(ln0_b), w_branch_na[l].astype(bf16),
                       w_branch_gdn[l].astype(bf16), w_out[l].astype(bf16), vec(ln1_g[l]), vec(ln1_b[l]),
                       w_route, b_route, _row_tile(t, 512))

    e_idx = route[:, :TOP_K].astype(jnp.int32)
    src_tok, dest, block_e, n_active = _dispatch_plan(e_idx, t)
    x_pad = _sc_gather(h1, src_tok)
    y_pad = _experts(x_pad, block_e, n_active, expert_w_gate[l], expert_w_up[l], expert_w_down[l])
    y_pairs = _sc_gather(y_pad, dest).reshape(t, TOP_K * D_MODEL)
    out = _combine(h1, y_pairs, route, vec(ln2_g[l]), vec(ln2_b[l]), _row_tile(t, 512))
    return out.reshape(batch, seq, d)
```

```python
import functools

import numpy as np
import jax
import jax.numpy as jnp
from jax import lax
from jax.experimental import pallas as pl
from jax.experimental.pallas import tpu as pltpu
from jax.experimental.pallas import tpu_sc as plsc

D_MODEL = 1024
DEPTH = 1
GRID_W = 64
N_META = 16
NA_HEADS = 8
NA_HEAD_DIM = 64
NA_W = NA_HEADS * NA_HEAD_DIM
NA_WIN_ROWS = 8
NA_WIN_COLS = 16
GDN_HEADS = 8
GDN_HEAD_DIM = 128
GDN_W = GDN_HEADS * GDN_HEAD_DIM
GDN_CONV = 5
GDN_CHUNK = 64
N_GROUPS = 4
EXPERTS_PER_GROUP = 8
N_EXPERTS = N_GROUPS * EXPERTS_PER_GROUP
TOP_K = 2
D_EXPERT = 256
LN_EPS = 1e-5
NORM_EPS = 1e-6
DN_ALPHA = (2 * DEPTH) ** 0.25

LANES = 128
VMEM_LIMIT = 48 * 1024 * 1024
NEG = -1e30

CB_G_Q, CB_G_K, CB_G_V, CB_G_Z = 0, 8, 16, 24
CB_GATE_NA, CB_GATE_GDN = 32, 40
CB_NA_Q, CB_NA_K, CB_NA_V = 48, 52, 56
P_COLS = 60 * LANES
AB_OFF = 3 * NA_W + 4 * GDN_W
N_AB = 4 * GDN_HEADS

EXPERT_ROWS = 256
GATHER_WINDOW = 64

f32 = jnp.float32
bf16 = jnp.bfloat16


def _dot(a, b):
    return jnp.dot(a.astype(bf16), b.astype(bf16), preferred_element_type=f32)


def _dot_nt(a, b):
    return lax.dot_general(a.astype(bf16), b.astype(bf16), (((1,), (1,)), ((), ())),
                           preferred_element_type=f32)


def _dot_tn(a, b):
    return lax.dot_general(a.astype(bf16), b.astype(bf16), (((0,), (0,)), ((), ())),
                           preferred_element_type=f32)


def _split_bf16(x):
    hi = x.astype(bf16)
    lo = (x - hi.astype(f32)).astype(bf16)
    return hi, lo


def _dot3(x, w):
    xh, xl = _split_bf16(x)
    wh, wl = _split_bf16(w)
    d = lambda a, b: jnp.dot(a, b, preferred_element_type=f32)
    return d(xh, wh) + (d(xl, wh) + d(xh, wl))


def _layer_norm(x, g, b):
    xc = x - jnp.mean(x, -1, keepdims=True)
    var = jnp.mean(xc * xc, -1, keepdims=True)
    return xc * lax.rsqrt(var + LN_EPS) * g + b


def _sigmoid(x):
    return 1.0 / (1.0 + jnp.exp(-x))


def _silu(x):
    return x * _sigmoid(x)


def _params(*sem):
    return pltpu.CompilerParams(dimension_semantics=sem, vmem_limit_bytes=VMEM_LIMIT)


def _in_proj_kernel(x_ref, g_ref, b_ref, w_ref, wab_ref, p_ref, ab_ref, hn_ref):
    @pl.when(pl.program_id(1) == 0)
    def _():
        h = _layer_norm(x_ref[...], g_ref[...], b_ref[...])
        hn_ref[...] = h.astype(bf16)
        ab_ref[...] = _dot3(h, wab_ref[...])

    p_ref[...] = jnp.dot(hn_ref[...], w_ref[...], preferred_element_type=f32).astype(bf16)


def _in_proj(x2d, ln_g, ln_b, w_main, w_ab, tm, tn):
    t = x2d.shape[0]
    return pl.pallas_call(
        _in_proj_kernel,
        grid=(t // tm, P_COLS // tn),
        in_specs=[
            pl.BlockSpec((tm, D_MODEL), lambda i, j: (i, 0)),
            pl.BlockSpec((1, D_MODEL), lambda i, j: (0, 0)),
            pl.BlockSpec((1, D_MODEL), lambda i, j: (0, 0)),
            pl.BlockSpec((D_MODEL, tn), lambda i, j: (0, j)),
            pl.BlockSpec((D_MODEL, LANES), lambda i, j: (0, 0)),
        ],
        out_specs=[
            pl.BlockSpec((tm, tn), lambda i, j: (i, j)),
            pl.BlockSpec((tm, LANES), lambda i, j: (i, 0)),
        ],
        out_shape=[
            jax.ShapeDtypeStruct((t, P_COLS), bf16),
            jax.ShapeDtypeStruct((t, LANES), f32),
        ],
        scratch_shapes=[pltpu.VMEM((tm, D_MODEL), bf16)],
        compiler_params=_params("parallel", "arbitrary"),
        name="in_proj",
    )(x2d, ln_g, ln_b, w_main, w_ab)


def _na_kernel(q_ref, k_ref, v_ref, km_ref, vm_ref, bias_ref, o_ref, *, rows):
    scale = NA_HEAD_DIM ** -0.5
    kr = NA_WIN_ROWS
    lane = lax.broadcasted_iota(jnp.int32, (GRID_W, LANES), 1)
    low = lane < NA_HEAD_DIM
    half_mask = (jnp.where(low, 1.0, 0.0).astype(bf16), jnp.where(low, 0.0, 1.0).astype(bf16))

    def row_block(r, carry):
        r0 = jnp.clip(r - kr // 2, 0, rows - kr)
        d0 = r0 - r + (NA_WIN_ROWS - 1)
        qoff = pl.multiple_of(r * GRID_W, GRID_W)
        koff = pl.multiple_of(r0 * GRID_W, GRID_W)
        for p in range(NA_HEADS // 2):
            sl = slice(LANES * p, LANES * (p + 1))
            qp = q_ref[pl.ds(qoff, GRID_W), sl]
            kp = k_ref[pl.ds(koff, kr * GRID_W), sl]
            vp = v_ref[pl.ds(koff, kr * GRID_W), sl]
            kmp = km_ref[:, sl]
            vmp = vm_ref[:, sl]
            outs = []
            for half in range(2):
                qh = qp * half_mask[half]
                s = _dot_nt(qh, kp) * scale + bias_ref[2 * p + half, d0]
                sm = _dot_nt(qh, kmp) * scale
                m = jnp.maximum(jnp.max(s, -1, keepdims=True), jnp.max(sm, -1, keepdims=True))
                e = jnp.exp(s - m)
                em = jnp.exp(sm - m)
                den = jnp.sum(e, -1, keepdims=True) + jnp.sum(em, -1, keepdims=True)
                outs.append((_dot(e, vp) + _dot(em, vmp)) / den)
            o_ref[pl.ds(qoff, GRID_W), sl] = jnp.where(low, outs[0], outs[1]).astype(bf16)
        return carry

    lax.fori_loop(0, rows, row_block, 0)


def _na_bias_table(rpb):
    col = np.arange(GRID_W)
    col_start = np.clip(col - NA_WIN_COLS // 2, 0, GRID_W - NA_WIN_COLS)
    col_mask = (col[None, :] >= col_start[:, None]) & (col[None, :] < col_start[:, None] + NA_WIN_COLS)
    dc_idx = np.clip(col[None, :] - col[:, None] + NA_WIN_COLS - 1, 0, 2 * NA_WIN_COLS - 2)
    dr = np.arange(NA_WIN_ROWS)[:, None] + np.arange(NA_WIN_ROWS)[None, :]
    tbl = rpb.astype(f32)[:, dr][:, :, :, dc_idx]
    tbl = jnp.where(col_mask[None, None, None], tbl, NEG)
    return tbl.transpose(0, 1, 3, 2, 4).reshape(NA_HEADS, NA_WIN_ROWS, GRID_W, NA_WIN_ROWS * GRID_W)


def _na(p_real, p_meta, bias_tbl, batch, seq):
    rows = seq // GRID_W
    assert rows >= NA_WIN_ROWS
    col = lambda cb: (lambda b: (b, cb // 4))
    colm = lambda cb: (lambda b: (0, cb // 4))
    return pl.pallas_call(
        functools.partial(_na_kernel, rows=rows),
        grid=(batch,),
        in_specs=[
            pl.BlockSpec((seq, NA_W), col(CB_NA_Q)),
            pl.BlockSpec((seq, NA_W), col(CB_NA_K)),
            pl.BlockSpec((seq, NA_W), col(CB_NA_V)),
            pl.BlockSpec((N_META, NA_W), colm(CB_NA_K)),
            pl.BlockSpec((N_META, NA_W), colm(CB_NA_V)),
            pl.BlockSpec(bias_tbl.shape, lambda b: (0, 0, 0, 0)),
        ],
        out_specs=pl.BlockSpec((seq, NA_W), lambda b: (b, 0)),
        out_shape=jax.ShapeDtypeStruct((batch * seq, NA_W), bf16),
        compiler_params=_params("parallel"),
        name="na",
    )(p_real, p_real, p_real, p_meta, p_meta, bias_tbl)


GDN_LEAD = GDN_CHUNK - N_META
RAW_PAD = 8


GDN_BLOCK = 128
GDN_HEADS_PER_STEP = 2
GDN_BLOCKS_PER_ITER = 2


INV_BASE = 8


def _tri_masks(c, lower):
    ii = lax.broadcasted_iota(jnp.int32, (c, c), 0)
    jj = lax.broadcasted_iota(jnp.int32, (c, c), 1)
    same = lambda s: jnp.right_shift(ii, s.bit_length() - 1) == jnp.right_shift(jj, s.bit_length() - 1)
    levels = [same(INV_BASE)]
    s = INV_BASE
    while s < c:
        levels.append(same(2 * s) & jnp.logical_not(same(s)))
        s *= 2
    if lower:
        return ii == jj, jj <= ii, jj < ii, ii <= jj, levels
    return ii == jj, jj >= ii, jj > ii, ii >= jj, levels


def _unit_tri_inverse(ms, eye, levels):
    pws = [jnp.where(levels[0], -m, 0.0) for m in ms]
    inv = [jnp.where(eye, 1.0, 0.0) + n for n in pws]
    for _ in range(INV_BASE.bit_length() - 2):
        pws = [_dot(p, p) for p in pws]
        inv = [a + _dot(a, p) for a, p in zip(inv, pws)]
    for pair in levels[1:]:
        ts = [_dot(a, jnp.where(pair, m, 0.0)) for a, m in zip(inv, ms)]
        inv = [a - _dot(t, a) for a, t in zip(inv, ts)]
    return inv


def _gdn_prepare(chains):
    pre = []
    for qc, kc, vc, kk, qk, g_col, beta_col, masks in chains:
        eye, incl, strict, incl_t, levels = masks
        g_row = jnp.sum(jnp.where(eye, g_col, 0.0), axis=0, keepdims=True)
        gc_col = jnp.sum(jnp.where(incl, g_row, 0.0), axis=1, keepdims=True)
        gc_row = jnp.sum(jnp.where(incl_t, g_col, 0.0), axis=0, keepdims=True)
        total = jnp.sum(g_col, axis=0, keepdims=True)
        decay = jnp.where(incl, jnp.exp(jnp.minimum(gc_col - gc_row, 0.0)), 0.0)
        pre.append((gc_col, total, decay, jnp.where(strict, kk * beta_col * decay, 0.0)))
    eye, levels = chains[0][7][0], chains[0][7][4]
    inv = _unit_tri_inverse([p[3] for p in pre], eye, levels)
    out = []
    sols = []
    for (qc, kc, vc, kk, qk, g_col, beta_col, masks), (gc_col, total, decay, _), a in zip(chains, pre, inv):
        e_col = jnp.exp(gc_col)
        sols.append(_dot(a, jnp.concatenate([vc * beta_col, kc * (beta_col * e_col)], axis=1)))
    for (qc, kc, vc, kk, qk, g_col, beta_col, masks), (gc_col, total, decay, _), sol in zip(chains, pre, sols):
        out.append((sol[:, :GDN_HEAD_DIM], sol[:, GDN_HEAD_DIM:], qc * jnp.exp(gc_col), qk * decay,
                    kc * jnp.exp(total - gc_col), jnp.exp(total)))
    return out


def _gdn_kernel(alog_ref, dtb_ref, q_ref, k_ref, v_ref, z_ref, qm_ref, km_ref, vm_ref,
                wq_ref, wk_ref, wv_ref, gt_ref, gtm_ref, nw_ref, o_ref,
                raw, u_s, wqd_s, aqk_s, kd_s, gl_s, st_s, oacc, *, seq, hg):
    c = GDN_BLOCK
    cm = GDN_CHUNK
    nblk = seq // c
    h0 = pl.program_id(1) * hg
    in_refs = (q_ref, k_ref, v_ref)
    meta_refs = (qm_ref, km_ref, vm_ref)
    w_refs = (wq_ref, wk_ref, wv_ref)
    cols = lambda hh: slice(LANES * hh, LANES * (hh + 1))
    halo = GDN_CONV // 2
    lead = RAW_PAD + GDN_LEAD

    for hh in range(hg):
        for t in range(3):
            raw[hh, t, pl.ds(0, lead), :] = jnp.zeros((lead, LANES), f32)
            raw[hh, t, pl.ds(lead, N_META), :] = meta_refs[t][:, cols(hh)].astype(f32)
            raw[hh, t, pl.ds(lead + N_META + seq, RAW_PAD), :] = jnp.zeros((RAW_PAD, LANES), f32)

    def copy_in(j, carry):
        src = pl.multiple_of(j * c, c)
        dst = pl.multiple_of(lead + N_META + j * c, 8)
        for hh in range(hg):
            for t in range(3):
                raw[hh, t, pl.ds(dst, c), :] = in_refs[t][pl.ds(src, c), cols(hh)].astype(f32)
        return carry

    lax.fori_loop(0, nblk, copy_in, 0)

    def conv_block(hh, p0, rows):
        out = []
        for t, kind in enumerate("qkv"):
            win = raw[hh, t, pl.ds(p0, rows + 2 * RAW_PAD), :]
            w = w_refs[t][:, cols(hh)]
            acc = None
            for tap in range(GDN_CONV):
                off = RAW_PAD - halo + tap
                term = win[off:off + rows, :] * w[tap:tap + 1, :]
                acc = term if acc is None else acc + term
            y = _silu(acc)
            if kind != "v":
                y = y * lax.rsqrt(jnp.sum(y * y, -1, keepdims=True) + NORM_EPS)
            if kind == "q":
                y = y * (GDN_HEAD_DIM ** -0.5)
            out.append(y)
        return out

    def gates(a_col, b_col, d, hh):
        coef = -jnp.exp(jnp.full((1, 1), alog_ref[d, h0 + hh], f32))
        x = a_col + dtb_ref[d, h0 + hh]
        softplus = jnp.maximum(x, 0.0) + jnp.log1p(jnp.exp(-jnp.abs(x)))
        return coef * softplus, _sigmoid(b_col)

    masks_meta = _tri_masks(cm, True)
    live_col = lax.broadcasted_iota(jnp.int32, (cm, 1), 0) >= GDN_LEAD
    heads = range(hg)
    metas = [[jnp.where(live_col, y, 0.0) for y in conv_block(hh, 0, cm)] for hh in heads]
    qkk = [_dot_nt(jnp.concatenate([q0, k0], axis=0), k0) for q0, k0, _ in metas]
    chain_in = []
    for hh in heads:
        q0, k0, v0 = metas[hh]
        gm = gtm_ref[hh]
        for d in range(2):
            g_col, beta_col = gates(gm[:, d:d + 1], gm[:, 2 + d:3 + d], d, hh)
            chain_in.append((q0, k0, v0, qkk[hh][cm:], qkk[hh][:cm], jnp.where(live_col, g_col, 0.0),
                             jnp.where(live_col, beta_col, 0.0), masks_meta))
    s0 = [_dot_tn(r[4], r[0]) for r in _gdn_prepare(chain_in)]
    for i, (hh, d) in enumerate((hh, d) for hh in heads for d in range(2)):
        st_s[hh, d] = s0[i]

    masks = (_tri_masks(c, True), _tri_masks(c, False))
    nb = GDN_BLOCKS_PER_ITER
    assert nblk % nb == 0

    def prepare_blocks(jj, carry):
        units = [(hh, b) for hh in heads for b in range(nb)]
        rows = [pl.multiple_of((jj * nb + b) * c, c) for b in range(nb)]
        convs = [conv_block(hh, pl.multiple_of(cm + rows[j], cm), c) for hh, j in units]
        qkk = [_dot_nt(jnp.concatenate([qc, kc], axis=0), kc) for qc, kc, _ in convs]
        chain_in, chain_id = [], []
        for (hh, j), (qc, kc, vc), qk_kk in zip(units, convs, qkk):
            gt = gt_ref[hh, pl.ds(rows[j], c), :]
            for d in range(2):
                g_col, beta_col = gates(gt[:, d:d + 1], gt[:, 2 + d:3 + d], d, hh)
                chain_in.append((qc, kc, vc, qk_kk[c:], qk_kk[:c], g_col, beta_col, masks[d]))
                chain_id.append((hh, d, j))
        for (hh, d, j), (u, w, qd, aqk, kd, g_last) in zip(chain_id, _gdn_prepare(chain_in)):
            r0 = rows[j]
            u_s[hh, d, pl.ds(r0, c), :] = u
            wqd_s[hh, d, pl.ds(pl.multiple_of(2 * r0, 2 * c), c), :] = w.astype(bf16)
            wqd_s[hh, d, pl.ds(pl.multiple_of(2 * r0 + c, c), c), :] = qd.astype(bf16)
            aqk_s[hh, d, pl.ds(r0, c), :] = aqk.astype(bf16)
            kd_s[hh, d, pl.ds(r0, c), :] = kd.astype(bf16)
            gl_s[hh, d, pl.ds(pl.multiple_of((jj * nb + j) * 8, 8), 8), :] = jnp.broadcast_to(g_last, (8, LANES))
        for hh, j in units:
            oacc[hh, pl.ds(rows[j], c), :] = jnp.zeros((c, LANES), f32)
        return carry

    lax.fori_loop(0, nblk // nb, prepare_blocks, 0)

    def scan_step(j, carry):
        rows = (pl.multiple_of(j * c, c), pl.multiple_of((nblk - 1 - j) * c, c))
        grow = (pl.multiple_of(j * 8, 8), pl.multiple_of((nblk - 1 - j) * 8, 8))
        chains = [(hh, d) for hh in heads for d in range(2)]
        states = [st_s[hh, d] for hh, d in chains]
        ws = [jnp.dot(wqd_s[hh, d, pl.ds(pl.multiple_of(2 * rows[d], 2 * c), 2 * c), :], s.astype(bf16),
                      preferred_element_type=f32) for (hh, d), s in zip(chains, states)]
        v_new = [(u_s[hh, d, pl.ds(rows[d], c), :] - w[:c]).astype(bf16) for (hh, d), w in zip(chains, ws)]
        outs = [w[c:] + jnp.dot(aqk_s[hh, d, pl.ds(rows[d], c), :], v, preferred_element_type=f32)
                for (hh, d), w, v in zip(chains, ws, v_new)]
        new_states = [s * gl_s[hh, d, pl.ds(grow[d], 1), :] + _dot_tn(kd_s[hh, d, pl.ds(rows[d], c), :], v)
                      for (hh, d), s, v in zip(chains, states, v_new)]
        for (hh, d), s in zip(chains, new_states):
            st_s[hh, d] = s
        for (hh, d), o in zip(chains, outs):
            oacc[hh, pl.ds(rows[d], c), :] += o
        return carry

    lax.fori_loop(0, nblk, scan_step, 0)

    def finish(j, carry):
        r0 = pl.multiple_of(j * c, c)
        for hh in range(hg):
            o = oacc[hh, pl.ds(r0, c), :]
            o = o * lax.rsqrt(jnp.mean(o * o, -1, keepdims=True) + NORM_EPS) * nw_ref[...]
            o_ref[pl.ds(r0, c), cols(hh)] = (o * _silu(z_ref[pl.ds(r0, c), cols(hh)].astype(f32))).astype(bf16)
        return carry

    lax.fori_loop(0, nblk, finish, 0)


def _gdn(p_real, p_meta, conv_w, gates_real, gates_meta, a_log, dt_bias, norm_w, batch, seq):
    hg = GDN_HEADS_PER_STEP
    assert seq % GDN_BLOCK == 0 and GDN_HEADS % hg == 0
    nblk = seq // GDN_BLOCK
    wide = hg * LANES
    col = lambda cb: (lambda b, h: (b, cb // hg + h))
    colm = lambda cb: (lambda b, h: (0, cb // hg + h))
    colw = lambda k: (lambda b, h: (0, k * (GDN_HEADS // hg) + h))
    smem = pl.BlockSpec(memory_space=pltpu.SMEM)
    return pl.pallas_call(
        functools.partial(_gdn_kernel, seq=seq, hg=hg),
        grid=(batch, GDN_HEADS // hg),
        in_specs=[
            smem, smem,
            pl.BlockSpec((seq, wide), col(CB_G_Q)),
            pl.BlockSpec((seq, wide), col(CB_G_K)),
            pl.BlockSpec((seq, wide), col(CB_G_V)),
            pl.BlockSpec((seq, wide), col(CB_G_Z)),
            pl.BlockSpec((N_META, wide), colm(CB_G_Q)),
            pl.BlockSpec((N_META, wide), colm(CB_G_K)),
            pl.BlockSpec((N_META, wide), colm(CB_G_V)),
            pl.BlockSpec((GDN_CONV, wide), colw(0)),
            pl.BlockSpec((GDN_CONV, wide), colw(1)),
            pl.BlockSpec((GDN_CONV, wide), colw(2)),
            pl.BlockSpec((None, hg, seq, 4), lambda b, h: (b, h, 0, 0)),
            pl.BlockSpec((hg, GDN_CHUNK, 4), lambda b, h: (h, 0, 0)),
            pl.BlockSpec((1, LANES), lambda b, h: (0, 0)),
        ],
        out_specs=pl.BlockSpec((seq, wide), lambda b, h: (b, h)),
        out_shape=jax.ShapeDtypeStruct((batch * seq, GDN_W), bf16),
        scratch_shapes=[
            pltpu.VMEM((hg, 3, seq + GDN_CHUNK + 2 * RAW_PAD, LANES), f32),
            pltpu.VMEM((hg, 2, seq, LANES), f32),
            pltpu.VMEM((hg, 2, 2 * seq, LANES), bf16),
            pltpu.VMEM((hg, 2, seq, LANES), bf16),
            pltpu.VMEM((hg, 2, seq, LANES), bf16),
            pltpu.VMEM((hg, 2, 8 * nblk, LANES), f32),
            pltpu.VMEM((hg, 2, GDN_HEAD_DIM, GDN_HEAD_DIM), f32),
            pltpu.VMEM((hg, seq, LANES), f32),
        ],
        compiler_params=_params("parallel", "parallel"),
        name="gdn",
    )(a_log, dt_bias, p_real, p_real, p_real, p_real, p_meta, p_meta, p_meta,
      conv_w, conv_w, conv_w, gates_real, gates_meta, norm_w)


def _route(logits):
    lane = lax.broadcasted_iota(jnp.int32, logits.shape, 1)
    big = jnp.int32(1 << 20)
    is_g = (lane >= N_EXPERTS) & (lane < N_EXPERTS + N_GROUPS)
    gl = jnp.where(is_g, logits, -jnp.inf)
    gmax = jnp.max(gl, -1, keepdims=True)
    g_idx = jnp.min(jnp.where(gl == gmax, lane - N_EXPERTS, big), -1, keepdims=True)
    g_w = 1.0 / jnp.sum(jnp.where(is_g, jnp.exp(gl - gmax), 0.0), -1, keepdims=True)
    in_grp = (lane < N_EXPERTS) & (jnp.right_shift(lane, 3) == g_idx)
    el = jnp.where(in_grp, logits, -jnp.inf)
    m1 = jnp.max(el, -1, keepdims=True)
    i1 = jnp.min(jnp.where(el == m1, lane, big), -1, keepdims=True)
    el2 = jnp.where(lane == i1, -jnp.inf, el)
    m2 = jnp.max(el2, -1, keepdims=True)
    i2 = jnp.min(jnp.where(el2 == m2, lane, big), -1, keepdims=True)
    e2 = jnp.exp(m2 - m1)
    w1 = g_w / (1.0 + e2)
    w2 = g_w * e2 / (1.0 + e2)
    out = jnp.where(lane == 0, i1.astype(f32), 0.0)
    out = jnp.where(lane == 1, i2.astype(f32), out)
    out = jnp.where(lane == 2, w1, out)
    return jnp.where(lane == 3, w2, out)


def _merge_kernel(x_ref, ona_ref, ogdn_ref, gna_ref, ggdn_ref, ln0g_ref, ln0b_ref, wna_ref, wgdn_ref,
                  wout_ref, ln1g_ref, ln1b_ref, wr_ref, br_ref, h1_ref, route_ref):
    h = _layer_norm(x_ref[...], ln0g_ref[...], ln0b_ref[...])
    y_na = jnp.dot(ona_ref[...], wna_ref[...], preferred_element_type=f32)
    y_gdn = jnp.dot(ogdn_ref[...], wgdn_ref[...], preferred_element_type=f32)
    merged = _sigmoid(gna_ref[...].astype(f32)) * y_na + _sigmoid(ggdn_ref[...].astype(f32)) * y_gdn
    mix = jnp.dot(merged.astype(bf16), wout_ref[...], preferred_element_type=f32)
    h1 = _layer_norm(DN_ALPHA * h + mix, ln1g_ref[...], ln1b_ref[...])
    h1_ref[...] = h1
    route_ref[...] = _route(_dot3(h1, wr_ref[...]) + br_ref[...])


def _merge(x2d, o_na, o_gdn, p_real, ln0_g, ln0_b, w_na, w_gdn, w_out, ln1_g, ln1_b, w_route, b_route, tm):
    t = x2d.shape[0]
    row = lambda i: (i, 0)
    const = lambda i: (0, 0)
    return pl.pallas_call(
        _merge_kernel,
        grid=(t // tm,),
        in_specs=[
            pl.BlockSpec((tm, D_MODEL), row),
            pl.BlockSpec((tm, NA_W), row),
            pl.BlockSpec((tm, GDN_W), row),
            pl.BlockSpec((tm, D_MODEL), lambda i: (i, CB_GATE_NA // 8)),
            pl.BlockSpec((tm, D_MODEL), lambda i: (i, CB_GATE_GDN // 8)),
            pl.BlockSpec((1, D_MODEL), const),
            pl.BlockSpec((1, D_MODEL), const),
            pl.BlockSpec((NA_W, D_MODEL), const),
            pl.BlockSpec((GDN_W, D_MODEL), const),
            pl.BlockSpec((D_MODEL, D_MODEL), const),
            pl.BlockSpec((1, D_MODEL), const),
            pl.BlockSpec((1, D_MODEL), const),
            pl.BlockSpec((D_MODEL, LANES), const),
            pl.BlockSpec((1, LANES), const),
        ],
        out_specs=[pl.BlockSpec((tm, D_MODEL), row), pl.BlockSpec((tm, LANES), row)],
        out_shape=[jax.ShapeDtypeStruct((t, D_MODEL), f32), jax.ShapeDtypeStruct((t, LANES), f32)],
        compiler_params=_params("parallel"),
        name="merge",
    )(x2d, o_na, o_gdn, p_real, p_real, ln0_g, ln0_b, w_na, w_gdn, w_out, ln1_g, ln1_b, w_route, b_route)


def _sc_gather(table, idx):
    n = idx.shape[0]
    width = table.shape[1]
    mesh = plsc.VectorSubcoreMesh(core_axis_name="core", subcore_axis_name="subcore")
    n_workers = mesh.num_cores * mesh.num_subcores
    per_worker = n // n_workers
    assert n % n_workers == 0 and per_worker % GATHER_WINDOW == 0

    @functools.partial(
        pl.kernel, out_type=jax.ShapeDtypeStruct((n, width), table.dtype), mesh=mesh,
        scratch_types=[pltpu.VMEM((GATHER_WINDOW,), jnp.int32), pltpu.VMEM((GATHER_WINDOW, width), table.dtype)],
        name="row_gather")
    def gather(tbl_hbm, idx_hbm, out_hbm, idx_vmem, rows_vmem):
        worker = lax.axis_index("subcore") * mesh.num_cores + lax.axis_index("core")

        @pl.loop(0, per_worker // GATHER_WINDOW)
        def _(step):
            base = pl.multiple_of(worker * per_worker + step * GATHER_WINDOW, GATHER_WINDOW)
            pltpu.sync_copy(idx_hbm.at[pl.ds(base, GATHER_WINDOW)], idx_vmem)
            pltpu.sync_copy(tbl_hbm.at[idx_vmem], rows_vmem)
            pltpu.sync_copy(rows_vmem, out_hbm.at[pl.ds(base, GATHER_WINDOW)])

    return gather(table, idx)


def _expert_kernel(be_ref, na_ref, x_ref, wg_ref, wu_ref, wd_ref, y_ref):
    active = pl.program_id(0) < na_ref[0]

    @pl.when(jnp.logical_not(active))
    def _():
        y_ref[...] = jnp.zeros_like(y_ref)

    @pl.when(active)
    def _():
        x = x_ref[...].astype(bf16)
        gate = jnp.dot(x, wg_ref[...].astype(bf16), preferred_element_type=f32)
        up = jnp.dot(x, wu_ref[...].astype(bf16), preferred_element_type=f32)
        hdn = (_silu(gate) * up).astype(bf16)
        y_ref[...] = jnp.dot(hdn, wd_ref[...].astype(bf16), preferred_element_type=f32)


def _experts(x_pad, block_e, n_active, w_gate, w_up, w_down):
    nb = x_pad.shape[0] // EXPERT_ROWS
    grid_spec = pltpu.PrefetchScalarGridSpec(
        num_scalar_prefetch=2,
        grid=(nb,),
        in_specs=[
            pl.BlockSpec((EXPERT_ROWS, D_MODEL), lambda i, be, na: (i, 0)),
            pl.BlockSpec((None, D_MODEL, D_EXPERT), lambda i, be, na: (be[i], 0, 0)),
            pl.BlockSpec((None, D_MODEL, D_EXPERT), lambda i, be, na: (be[i], 0, 0)),
            pl.BlockSpec((None, D_EXPERT, D_MODEL), lambda i, be, na: (be[i], 0, 0)),
        ],
        out_specs=pl.BlockSpec((EXPERT_ROWS, D_MODEL), lambda i, be, na: (i, 0)),
    )
    return pl.pallas_call(
        _expert_kernel,
        grid_spec=grid_spec,
        out_shape=jax.ShapeDtypeStruct(x_pad.shape, f32),
        compiler_params=_params("arbitrary"),
        name="experts",
    )(block_e, n_active, x_pad, w_gate, w_up, w_down)


def _combine_kernel(h1_ref, y_ref, route_ref, g_ref, b_ref, o_ref):
    r = route_ref[...]
    y = y_ref[...]
    ffn = y[:, :D_MODEL] * r[:, 2:3] + y[:, D_MODEL:] * r[:, 3:4]
    o_ref[...] = _layer_norm(DN_ALPHA * h1_ref[...] + ffn, g_ref[...], b_ref[...])


def _combine(h1, y_pairs, route, ln_g, ln_b, tm):
    t = h1.shape[0]
    row = lambda i: (i, 0)
    const = lambda i: (0, 0)
    return pl.pallas_call(
        _combine_kernel,
        grid=(t // tm,),
        in_specs=[
            pl.BlockSpec((tm, D_MODEL), row),
            pl.BlockSpec((tm, TOP_K * D_MODEL), row),
            pl.BlockSpec((tm, LANES), row),
            pl.BlockSpec((1, D_MODEL), const),
            pl.BlockSpec((1, D_MODEL), const),
        ],
        out_specs=pl.BlockSpec((tm, D_MODEL), row),
        out_shape=jax.ShapeDtypeStruct((t, D_MODEL), f32),
        compiler_params=_params("parallel"),
        name="combine",
    )(h1, y_pairs, route, ln_g, ln_b)


def _dispatch_plan(e_idx, t):
    a = t * TOP_K
    nb = a // EXPERT_ROWS + N_EXPERTS
    flat_e = e_idx.reshape(-1)
    onehot = (flat_e[:, None] == jnp.arange(N_EXPERTS, dtype=jnp.int32)[None, :]).astype(jnp.int32)
    csum = jnp.cumsum(onehot, axis=0)
    rank = jnp.sum(csum * onehot, axis=1) - 1
    counts = csum[-1]
    padded = (counts + EXPERT_ROWS - 1) // EXPERT_ROWS * EXPERT_ROWS
    pad_end = jnp.cumsum(padded)
    dest = (pad_end - padded)[flat_e] + rank
    flat_tok = jnp.arange(a, dtype=jnp.int32) // TOP_K
    src_tok = jnp.zeros((nb * EXPERT_ROWS,), jnp.int32).at[dest].set(flat_tok)
    block_start = jnp.arange(nb, dtype=jnp.int32) * EXPERT_ROWS
    block_e = jnp.minimum(jnp.searchsorted(pad_end, block_start, side="right"), N_EXPERTS - 1).astype(jnp.int32)
    n_active = (pad_end[-1:] // EXPERT_ROWS).astype(jnp.int32)
    return src_tok, dest.astype(jnp.int32), block_e, n_active


def _row_tile(t, want):
    tm = min(t, want)
    assert t % tm == 0
    return tm


def kernel(x, meta_tokens, ln0_g, ln0_b, w_in, na_rpb, gdn_conv_w, gdn_a_log, gdn_dt_bias, gdn_norm_w,
           w_branch_na, w_branch_gdn, w_out, ln1_g, ln1_b, router_group_w, router_group_b, router_expert_w,
           router_expert_b, expert_w_gate, expert_w_up, expert_w_down, ln2_g, ln2_b):
    batch, seq, d = x.shape
    assert d == D_MODEL and seq % GRID_W == 0 and DEPTH == 1
    t = batch * seq
    l = 0
    x2d = x.reshape(t, d)
    vec = lambda v: v.reshape(1, -1).astype(f32)

    w = w_in[l]
    w_main = jnp.concatenate([w[:, 3 * NA_W:AB_OFF], w[:, AB_OFF + N_AB:], w[:, :3 * NA_W]], axis=1).astype(bf16)
    w_ab = jnp.pad(w[:, AB_OFF:AB_OFF + N_AB].astype(f32), ((0, 0), (0, LANES - N_AB)))

    p_real, ab_real = _in_proj(x2d, vec(ln0_g), vec(ln0_b), w_main, w_ab, _row_tile(t, 1024), 1280)
    p_meta, ab_meta = _in_proj(meta_tokens.astype(f32), vec(ln0_g), vec(ln0_b), w_main, w_ab, N_META, 1280)

    o_na = _na(p_real, p_meta, _na_bias_table(na_rpb[l]), batch, seq)

    gates_real = ab_real[:, :N_AB].reshape(batch, seq, 4, GDN_HEADS).transpose(0, 3, 1, 2)
    gates_meta = ab_meta[:, :N_AB].reshape(N_META, 4, GDN_HEADS).transpose(2, 0, 1)
    gates_meta = jnp.pad(gates_meta, ((0, 0), (GDN_LEAD, 0), (0, 0)))
    o_gdn = _gdn(p_real, p_meta, gdn_conv_w[l].astype(f32), gates_real, gates_meta, gdn_a_log[l].astype(f32),
                 gdn_dt_bias[l].astype(f32), vec(gdn_norm_w[l]), batch, seq)

    w_route = jnp.pad(jnp.concatenate([router_expert_w[l], router_group_w[l]], axis=1).astype(f32),
                      ((0, 0), (0, LANES - N_EXPERTS - N_GROUPS)))
    b_route = jnp.pad(jnp.concatenate([router_expert_b[l], router_group_b[l]]).astype(f32),
                      (0, LANES - N_EXPERTS - N_GROUPS)).reshape(1, LANES)
    h1, route = _merge(x2d, o_na, o_gdn, p_real, vec(ln0_g), vec(ln0_b), w_branch_na[l].astype(bf16),
                       w_branch_gdn[l].astype(bf16), w_out[l].astype(bf16), vec(ln1_g[l]), vec(ln1_b[l]),
                       w_route, b_route, _row_tile(t, 512))

    e_idx = route[:, :TOP_K].astype(jnp.int32)
    src_tok, dest, block_e, n_active = _dispatch_plan(e_idx, t)
    x_pad = _sc_gather(h1, src_tok)
    y_pad = _experts(x_pad, block_e, n_active, expert_w_gate[l], expert_w_up[l], expert_w_down[l])
    y_pairs = _sc_gather(y_pad, dest).reshape(t, TOP_K * D_MODEL)
    out = _combine(h1, y_pairs, route, vec(ln2_g[l]), vec(ln2_b[l]), _row_tile(t, 512))
    return out.reshape(batch, seq, d)
```

```python
import functools

import numpy as np
import jax
import jax.numpy as jnp
from jax import lax
from jax.experimental import pallas as pl
from jax.experimental.pallas import tpu as pltpu
from jax.experimental.pallas import tpu_sc as plsc

D_MODEL = 1024
DEPTH = 1
GRID_W = 64
N_META = 16
NA_HEADS = 8
NA_HEAD_DIM = 64
NA_W = NA_HEADS * NA_HEAD_DIM
NA_WIN_ROWS = 8
NA_WIN_COLS = 16
GDN_HEADS = 8
GDN_HEAD_DIM = 128
GDN_W = GDN_HEADS * GDN_HEAD_DIM
GDN_CONV = 5
GDN_CHUNK = 64
N_GROUPS = 4
EXPERTS_PER_GROUP = 8
N_EXPERTS = N_GROUPS * EXPERTS_PER_GROUP
TOP_K = 2
D_EXPERT = 256
LN_EPS = 1e-5
NORM_EPS = 1e-6
DN_ALPHA = (2 * DEPTH) ** 0.25

LANES = 128
VMEM_LIMIT = 48 * 1024 * 1024
NEG = -1e30

CB_G_Q, CB_G_K, CB_G_V, CB_G_Z = 0, 8, 16, 24
CB_GATE_NA, CB_GATE_GDN = 32, 40
CB_NA_Q, CB_NA_K, CB_NA_V = 48, 52, 56
P_COLS = 60 * LANES
AB_OFF = 3 * NA_W + 4 * GDN_W
N_AB = 4 * GDN_HEADS

EXPERT_ROWS = 256
GATHER_WINDOW = 64

f32 = jnp.float32
bf16 = jnp.bfloat16


def _dot(a, b):
    return jnp.dot(a.astype(bf16), b.astype(bf16), preferred_element_type=f32)


def _dot_nt(a, b):
    return lax.dot_general(a.astype(bf16), b.astype(bf16), (((1,), (1,)), ((), ())),
                           preferred_element_type=f32)


def _dot_tn(a, b):
    return lax.dot_general(a.astype(bf16), b.astype(bf16), (((0,), (0,)), ((), ())),
                           preferred_element_type=f32)


def _split_bf16(x):
    hi = x.astype(bf16)
    lo = (x - hi.astype(f32)).astype(bf16)
    return hi, lo


def _dot3(x, w):
    xh, xl = _split_bf16(x)
    wh, wl = _split_bf16(w)
    d = lambda a, b: jnp.dot(a, b, preferred_element_type=f32)
    return d(xh, wh) + (d(xl, wh) + d(xh, wl))


def _layer_norm(x, g, b):
    xc = x - jnp.mean(x, -1, keepdims=True)
    var = jnp.mean(xc * xc, -1, keepdims=True)
    return xc * lax.rsqrt(var + LN_EPS) * g + b


def _sigmoid(x):
    return 1.0 / (1.0 + jnp.exp(-x))


def _silu(x):
    return x * _sigmoid(x)


def _params(*sem):
    return pltpu.CompilerParams(dimension_semantics=sem, vmem_limit_bytes=VMEM_LIMIT)


def _in_proj_kernel(x_ref, g_ref, b_ref, w_ref, wab_ref, alog_ref, dtb_ref, p_ref, gb_ref, hn_ref):
    @pl.when(pl.program_id(1) == 0)
    def _():
        h = _layer_norm(x_ref[...], g_ref[...], b_ref[...])
        hn_ref[...] = h.astype(bf16)
        ab = _dot3(h, wab_ref[...])
        x = ab + dtb_ref[...]
        softplus = jnp.maximum(x, 0.0) + jnp.log1p(jnp.exp(-jnp.abs(x)))
        lane = lax.broadcasted_iota(jnp.int32, ab.shape, 1)
        gb_ref[...] = jnp.where((lane & 3) < 2, -jnp.exp(alog_ref[...]) * softplus, _sigmoid(ab))

    p_ref[...] = jnp.dot(hn_ref[...], w_ref[...], preferred_element_type=f32).astype(bf16)


def _gate_lane_layout(w_in, a_log, dt_bias):
    w_ab = w_in[:, AB_OFF:AB_OFF + N_AB].astype(f32).reshape(D_MODEL, 4, GDN_HEADS).transpose(0, 2, 1)
    w_ab = jnp.pad(w_ab.reshape(D_MODEL, N_AB), ((0, 0), (0, LANES - N_AB)))
    row = lambda p: jnp.pad(jnp.pad(p.astype(f32).T, ((0, 0), (0, 2))).reshape(1, N_AB), ((0, 0), (0, LANES - N_AB)))
    return w_ab, row(a_log), row(dt_bias)


def _in_proj(x2d, ln_g, ln_b, w_main, w_ab, alog_row, dtb_row, tm, tn):
    t = x2d.shape[0]
    return pl.pallas_call(
        _in_proj_kernel,
        grid=(t // tm, P_COLS // tn),
        in_specs=[
            pl.BlockSpec((tm, D_MODEL), lambda i, j: (i, 0)),
            pl.BlockSpec((1, D_MODEL), lambda i, j: (0, 0)),
            pl.BlockSpec((1, D_MODEL), lambda i, j: (0, 0)),
            pl.BlockSpec((D_MODEL, tn), lambda i, j: (0, j)),
            pl.BlockSpec((D_MODEL, LANES), lambda i, j: (0, 0)),
            pl.BlockSpec((1, LANES), lambda i, j: (0, 0)),
            pl.BlockSpec((1, LANES), lambda i, j: (0, 0)),
        ],
        out_specs=[
            pl.BlockSpec((tm, tn), lambda i, j: (i, j)),
            pl.BlockSpec((tm, LANES), lambda i, j: (i, 0)),
        ],
        out_shape=[
            jax.ShapeDtypeStruct((t, P_COLS), bf16),
            jax.ShapeDtypeStruct((t, LANES), f32),
        ],
        scratch_shapes=[pltpu.VMEM((tm, D_MODEL), bf16)],
        compiler_params=_params("parallel", "arbitrary"),
        name="in_proj",
    )(x2d, ln_g, ln_b, w_main, w_ab, alog_row, dtb_row)


def _na_kernel(q_ref, k_ref, v_ref, km_ref, vm_ref, bias_ref, o_ref, *, rows):
    scale = NA_HEAD_DIM ** -0.5
    kr = NA_WIN_ROWS
    lane = lax.broadcasted_iota(jnp.int32, (GRID_W, LANES), 1)
    low = lane < NA_HEAD_DIM
    half_mask = (jnp.where(low, scale, 0.0).astype(bf16), jnp.where(low, 0.0, scale).astype(bf16))
    heads = range(NA_HEADS)
    sl = [slice(LANES * (h // 2), LANES * (h // 2 + 1)) for h in heads]

    def row_block(r, carry):
        r0 = jnp.clip(r - kr // 2, 0, rows - kr)
        d0 = r0 - r + (NA_WIN_ROWS - 1)
        qoff = pl.multiple_of(r * GRID_W, GRID_W)
        koff = pl.multiple_of(r0 * GRID_W, GRID_W)
        qh = [q_ref[pl.ds(qoff, GRID_W), sl[h]] * half_mask[h % 2] for h in heads]
        s = [_dot_nt(qh[h], k_ref[pl.ds(koff, kr * GRID_W), sl[h]]) + bias_ref[h, d0] for h in heads]
        sm = [_dot_nt(qh[h], km_ref[:, sl[h]]) for h in heads]
        m = [jnp.maximum(jnp.max(s[h], -1, keepdims=True), jnp.max(sm[h], -1, keepdims=True)) for h in heads]
        e = [jnp.exp(s[h] - m[h]) for h in heads]
        em = [jnp.exp(sm[h] - m[h]) for h in heads]
        den = [jnp.sum(e[h], -1, keepdims=True) + jnp.sum(em[h], -1, keepdims=True) for h in heads]
        o = [_dot(e[h], v_ref[pl.ds(koff, kr * GRID_W), sl[h]]) + _dot(em[h], vm_ref[:, sl[h]]) for h in heads]
        o = [o[h] / den[h] for h in heads]
        for p in range(NA_HEADS // 2):
            o_ref[pl.ds(qoff, GRID_W), sl[2 * p]] = jnp.where(low, o[2 * p], o[2 * p + 1]).astype(bf16)
        return carry

    lax.fori_loop(0, rows, row_block, 0)


def _na_bias_table(rpb):
    col = np.arange(GRID_W)
    col_start = np.clip(col - NA_WIN_COLS // 2, 0, GRID_W - NA_WIN_COLS)
    col_mask = (col[None, :] >= col_start[:, None]) & (col[None, :] < col_start[:, None] + NA_WIN_COLS)
    dc_idx = np.clip(col[None, :] - col[:, None] + NA_WIN_COLS - 1, 0, 2 * NA_WIN_COLS - 2)
    pick = (dc_idx[:, :, None] == np.arange(2 * NA_WIN_COLS - 1)).astype(np.float32)
    by_col = jnp.einsum("hrc,qkc->hrqk", rpb.astype(f32), pick, precision=lax.Precision.HIGHEST)
    by_col = jnp.where(col_mask[None, None], by_col, NEG)
    tbl = jnp.stack([by_col[:, d0:d0 + NA_WIN_ROWS] for d0 in range(NA_WIN_ROWS)], axis=1)
    return tbl.transpose(0, 1, 3, 2, 4).reshape(NA_HEADS, NA_WIN_ROWS, GRID_W, NA_WIN_ROWS * GRID_W)


def _na(p_real, p_meta, bias_tbl, batch, seq):
    rows = seq // GRID_W
    assert rows >= NA_WIN_ROWS
    col = lambda cb: (lambda b: (b, cb // 4))
    colm = lambda cb: (lambda b: (0, cb // 4))
    return pl.pallas_call(
        functools.partial(_na_kernel, rows=rows),
        grid=(batch,),
        in_specs=[
            pl.BlockSpec((seq, NA_W), col(CB_NA_Q)),
            pl.BlockSpec((seq, NA_W), col(CB_NA_K)),
            pl.BlockSpec((seq, NA_W), col(CB_NA_V)),
            pl.BlockSpec((N_META, NA_W), colm(CB_NA_K)),
            pl.BlockSpec((N_META, NA_W), colm(CB_NA_V)),
            pl.BlockSpec(bias_tbl.shape, lambda b: (0, 0, 0, 0)),
        ],
        out_specs=pl.BlockSpec((seq, NA_W), lambda b: (b, 0)),
        out_shape=jax.ShapeDtypeStruct((batch * seq, NA_W), bf16),
        compiler_params=_params("parallel"),
        name="na",
    )(p_real, p_real, p_real, p_meta, p_meta, bias_tbl)


GDN_LEAD = GDN_CHUNK - N_META
RAW_PAD = 8


GDN_BLOCK = 128
GDN_HEADS_PER_STEP = 2
GDN_BLOCKS_PER_ITER = 2


INV_BASE = 8


def _tri_masks(c, lower):
    ii = lax.broadcasted_iota(jnp.int32, (c, c), 0)
    jj = lax.broadcasted_iota(jnp.int32, (c, c), 1)
    same = lambda s: jnp.right_shift(ii, s.bit_length() - 1) == jnp.right_shift(jj, s.bit_length() - 1)
    levels = [same(INV_BASE)]
    s = INV_BASE
    while s < c:
        levels.append(same(2 * s) & jnp.logical_not(same(s)))
        s *= 2
    if lower:
        return ii == jj, jj <= ii, jj < ii, ii <= jj, levels
    return ii == jj, jj >= ii, jj > ii, ii >= jj, levels


def _unit_tri_inverse(ms, eye, levels):
    pws = [jnp.where(levels[0], -m, 0.0) for m in ms]
    inv = [jnp.where(eye, 1.0, 0.0) + n for n in pws]
    for _ in range(INV_BASE.bit_length() - 2):
        pws = [_dot(p, p) for p in pws]
        inv = [a + _dot(a, p) for a, p in zip(inv, pws)]
    for pair in levels[1:]:
        ts = [_dot(a, jnp.where(pair, m, 0.0)) for a, m in zip(inv, ms)]
        inv = [a - _dot(t, a) for a, t in zip(inv, ts)]
    return inv


def _gdn_prepare(chains):
    pre = []
    for qc, kc, vc, kk, qk, g_col, beta_col, masks in chains:
        eye, incl, strict, incl_t, levels = masks
        g_row = jnp.sum(jnp.where(eye, g_col, 0.0), axis=0, keepdims=True)
        gc_col = jnp.sum(jnp.where(incl, g_row, 0.0), axis=1, keepdims=True)
        gc_row = jnp.sum(jnp.where(incl_t, g_col, 0.0), axis=0, keepdims=True)
        total = jnp.sum(g_col, axis=0, keepdims=True)
        decay = jnp.where(incl, jnp.exp(jnp.minimum(gc_col - gc_row, 0.0)), 0.0)
        pre.append((gc_col, total, decay, jnp.where(strict, kk * beta_col * decay, 0.0)))
    eye, levels = chains[0][7][0], chains[0][7][4]
    inv = _unit_tri_inverse([p[3] for p in pre], eye, levels)
    out = []
    sols = []
    for (qc, kc, vc, kk, qk, g_col, beta_col, masks), (gc_col, total, decay, _), a in zip(chains, pre, inv):
        e_col = jnp.exp(gc_col)
        sols.append(_dot(a, jnp.concatenate([vc * beta_col, kc * (beta_col * e_col)], axis=1)))
    for (qc, kc, vc, kk, qk, g_col, beta_col, masks), (gc_col, total, decay, _), sol in zip(chains, pre, sols):
        out.append((sol[:, :GDN_HEAD_DIM], sol[:, GDN_HEAD_DIM:], qc * jnp.exp(gc_col), qk * decay,
                    kc * jnp.exp(total - gc_col), jnp.exp(total)))
    return out


def _gdn_kernel(q_ref, k_ref, v_ref, z_ref, qm_ref, km_ref, vm_ref,
                wq_ref, wk_ref, wv_ref, gt_ref, gtm_ref, nw_ref, o_ref,
                raw, u_s, wqd_s, aqk_s, kd_s, gl_s, st_s, oacc, *, seq, hg):
    c = GDN_BLOCK
    cm = GDN_CHUNK
    nblk = seq // c
    h0 = pl.program_id(1) * hg
    in_refs = (q_ref, k_ref, v_ref)
    meta_refs = (qm_ref, km_ref, vm_ref)
    w_refs = (wq_ref, wk_ref, wv_ref)
    cols = lambda hh: slice(LANES * hh, LANES * (hh + 1))
    halo = GDN_CONV // 2
    lead = RAW_PAD + GDN_LEAD

    for hh in range(hg):
        for t in range(3):
            raw[hh, t, pl.ds(0, lead), :] = jnp.zeros((lead, LANES), f32)
            raw[hh, t, pl.ds(lead, N_META), :] = meta_refs[t][:, cols(hh)].astype(f32)
            raw[hh, t, pl.ds(lead + N_META + seq, RAW_PAD), :] = jnp.zeros((RAW_PAD, LANES), f32)

    def copy_in(j, carry):
        src = pl.multiple_of(j * c, c)
        dst = pl.multiple_of(lead + N_META + j * c, 8)
        for hh in range(hg):
            for t in range(3):
                raw[hh, t, pl.ds(dst, c), :] = in_refs[t][pl.ds(src, c), cols(hh)].astype(f32)
        return carry

    lax.fori_loop(0, nblk, copy_in, 0)

    def conv_block(hh, p0, rows):
        out = []
        for t, kind in enumerate("qkv"):
            win = raw[hh, t, pl.ds(p0, rows + 2 * RAW_PAD), :]
            w = w_refs[t][:, cols(hh)]
            acc = None
            for tap in range(GDN_CONV):
                off = RAW_PAD - halo + tap
                term = win[off:off + rows, :] * w[tap:tap + 1, :]
                acc = term if acc is None else acc + term
            y = _silu(acc)
            if kind != "v":
                y = y * lax.rsqrt(jnp.sum(y * y, -1, keepdims=True) + NORM_EPS)
            if kind == "q":
                y = y * (GDN_HEAD_DIM ** -0.5)
            out.append(y)
        return out

    def gates(gb, hh):
        return pltpu.roll(gb, jnp.bitwise_and(LANES - 4 * (h0 + hh), LANES - 1), 1)

    masks_meta = _tri_masks(cm, True)
    live_col = lax.broadcasted_iota(jnp.int32, (cm, 1), 0) >= GDN_LEAD
    heads = range(hg)
    metas = [[jnp.where(live_col, y, 0.0) for y in conv_block(hh, 0, cm)] for hh in heads]
    qkk = [_dot_nt(jnp.concatenate([q0, k0], axis=0), k0) for q0, k0, _ in metas]
    chain_in = []
    gb_meta = jnp.concatenate([jnp.zeros((GDN_LEAD, LANES), f32), gtm_ref[...]], axis=0)
    for hh in heads:
        q0, k0, v0 = metas[hh]
        gm = gates(gb_meta, hh)
        for d in range(2):
            chain_in.append((q0, k0, v0, qkk[hh][cm:], qkk[hh][:cm], gm[:, d:d + 1], gm[:, 2 + d:3 + d], masks_meta))
    s0 = [_dot_tn(r[4], r[0]) for r in _gdn_prepare(chain_in)]
    for i, (hh, d) in enumerate((hh, d) for hh in heads for d in range(2)):
        st_s[hh, d] = s0[i]

    masks = (_tri_masks(c, True), _tri_masks(c, False))
    nb = GDN_BLOCKS_PER_ITER
    assert nblk % nb == 0

    def prepare_blocks(jj, carry):
        units = [(hh, b) for hh in heads for b in range(nb)]
        rows = [pl.multiple_of((jj * nb + b) * c, c) for b in range(nb)]
        convs = [conv_block(hh, pl.multiple_of(cm + rows[j], cm), c) for hh, j in units]
        qkk = [_dot_nt(jnp.concatenate([qc, kc], axis=0), kc) for qc, kc, _ in convs]
        chain_in, chain_id = [], []
        for (hh, j), (qc, kc, vc), qk_kk in zip(units, convs, qkk):
            gt = gates(gt_ref[pl.ds(rows[j], c), :], hh)
            for d in range(2):
                chain_in.append((qc, kc, vc, qk_kk[c:], qk_kk[:c], gt[:, d:d + 1], gt[:, 2 + d:3 + d], masks[d]))
                chain_id.append((hh, d, j))
        for (hh, d, j), (u, w, qd, aqk, kd, g_last) in zip(chain_id, _gdn_prepare(chain_in)):
            r0 = rows[j]
            u_s[hh, d, pl.ds(r0, c), :] = u
            wqd_s[hh, d, pl.ds(pl.multiple_of(2 * r0, 2 * c), c), :] = w.astype(bf16)
            wqd_s[hh, d, pl.ds(pl.multiple_of(2 * r0 + c, c), c), :] = qd.astype(bf16)
            aqk_s[hh, d, pl.ds(r0, c), :] = aqk.astype(bf16)
            kd_s[hh, d, pl.ds(r0, c), :] = kd.astype(bf16)
            gl_s[hh, d, pl.ds(pl.multiple_of((jj * nb + j) * 8, 8), 8), :] = jnp.broadcast_to(g_last, (8, LANES))
        for hh, j in units:
            oacc[hh, pl.ds(rows[j], c), :] = jnp.zeros((c, LANES), f32)
        return carry

    lax.fori_loop(0, nblk // nb, prepare_blocks, 0)

    def scan_step(j, carry):
        rows = (pl.multiple_of(j * c, c), pl.multiple_of((nblk - 1 - j) * c, c))
        grow = (pl.multiple_of(j * 8, 8), pl.multiple_of((nblk - 1 - j) * 8, 8))
        chains = [(hh, d) for hh in heads for d in range(2)]
        states = [st_s[hh, d] for hh, d in chains]
        ws = [jnp.dot(wqd_s[hh, d, pl.ds(pl.multiple_of(2 * rows[d], 2 * c), 2 * c), :], s.astype(bf16),
                      preferred_element_type=f32) for (hh, d), s in zip(chains, states)]
        v_new = [(u_s[hh, d, pl.ds(rows[d], c), :] - w[:c]).astype(bf16) for (hh, d), w in zip(chains, ws)]
        outs = [w[c:] + jnp.dot(aqk_s[hh, d, pl.ds(rows[d], c), :], v, preferred_element_type=f32)
                for (hh, d), w, v in zip(chains, ws, v_new)]
        new_states = [s * gl_s[hh, d, pl.ds(grow[d], 1), :] + _dot_tn(kd_s[hh, d, pl.ds(rows[d], c), :], v)
                      for (hh, d), s, v in zip(chains, states, v_new)]
        for (hh, d), s in zip(chains, new_states):
            st_s[hh, d] = s
        for (hh, d), o in zip(chains, outs):
            oacc[hh, pl.ds(rows[d], c), :] += o
        return carry

    lax.fori_loop(0, nblk, scan_step, 0)

    def finish(j, carry):
        r0 = pl.multiple_of(j * c, c)
        for hh in range(hg):
            o = oacc[hh, pl.ds(r0, c), :]
            o = o * lax.rsqrt(jnp.mean(o * o, -1, keepdims=True) + NORM_EPS) * nw_ref[...]
            o_ref[pl.ds(r0, c), cols(hh)] = (o * _silu(z_ref[pl.ds(r0, c), cols(hh)].astype(f32))).astype(bf16)
        return carry

    lax.fori_loop(0, nblk, finish, 0)


def _gdn(p_real, p_meta, conv_w, gates_real, gates_meta, norm_w, batch, seq):
    hg = GDN_HEADS_PER_STEP
    assert seq % GDN_BLOCK == 0 and GDN_HEADS % hg == 0
    nblk = seq // GDN_BLOCK
    wide = hg * LANES
    col = lambda cb: (lambda b, h: (b, cb // hg + h))
    colm = lambda cb: (lambda b, h: (0, cb // hg + h))
    colw = lambda k: (lambda b, h: (0, k * (GDN_HEADS // hg) + h))
    return pl.pallas_call(
        functools.partial(_gdn_kernel, seq=seq, hg=hg),
        grid=(batch, GDN_HEADS // hg),
        in_specs=[
            pl.BlockSpec((seq, wide), col(CB_G_Q)),
            pl.BlockSpec((seq, wide), col(CB_G_K)),
            pl.BlockSpec((seq, wide), col(CB_G_V)),
            pl.BlockSpec((seq, wide), col(CB_G_Z)),
            pl.BlockSpec((N_META, wide), colm(CB_G_Q)),
            pl.BlockSpec((N_META, wide), colm(CB_G_K)),
            pl.BlockSpec((N_META, wide), colm(CB_G_V)),
            pl.BlockSpec((GDN_CONV, wide), colw(0)),
            pl.BlockSpec((GDN_CONV, wide), colw(1)),
            pl.BlockSpec((GDN_CONV, wide), colw(2)),
            pl.BlockSpec((seq, LANES), lambda b, h: (b, 0)),
            pl.BlockSpec((N_META, LANES), lambda b, h: (0, 0)),
            pl.BlockSpec((1, LANES), lambda b, h: (0, 0)),
        ],
        out_specs=pl.BlockSpec((seq, wide), lambda b, h: (b, h)),
        out_shape=jax.ShapeDtypeStruct((batch * seq, GDN_W), bf16),
        scratch_shapes=[
            pltpu.VMEM((hg, 3, seq + GDN_CHUNK + 2 * RAW_PAD, LANES), f32),
            pltpu.VMEM((hg, 2, seq, LANES), f32),
            pltpu.VMEM((hg, 2, 2 * seq, LANES), bf16),
            pltpu.VMEM((hg, 2, seq, LANES), bf16),
            pltpu.VMEM((hg, 2, seq, LANES), bf16),
            pltpu.VMEM((hg, 2, 8 * nblk, LANES), f32),
            pltpu.VMEM((hg, 2, GDN_HEAD_DIM, GDN_HEAD_DIM), f32),
            pltpu.VMEM((hg, seq, LANES), f32),
        ],
        compiler_params=_params("parallel", "parallel"),
        name="gdn",
    )(p_real, p_real, p_real, p_real, p_meta, p_meta, p_meta, conv_w, conv_w, conv_w, gates_real, gates_meta, norm_w)


def _route(logits):
    lane = lax.broadcasted_iota(jnp.int32, logits.shape, 1)
    big = jnp.int32(1 << 20)
    is_g = (lane >= N_EXPERTS) & (lane < N_EXPERTS + N_GROUPS)
    gl = jnp.where(is_g, logits, -jnp.inf)
    gmax = jnp.max(gl, -1, keepdims=True)
    g_idx = jnp.min(jnp.where(gl == gmax, lane - N_EXPERTS, big), -1, keepdims=True)
    g_w = 1.0 / jnp.sum(jnp.where(is_g, jnp.exp(gl - gmax), 0.0), -1, keepdims=True)
    in_grp = (lane < N_EXPERTS) & (jnp.right_shift(lane, 3) == g_idx)
    el = jnp.where(in_grp, logits, -jnp.inf)
    m1 = jnp.max(el, -1, keepdims=True)
    i1 = jnp.min(jnp.where(el == m1, lane, big), -1, keepdims=True)
    el2 = jnp.where(lane == i1, -jnp.inf, el)
    m2 = jnp.max(el2, -1, keepdims=True)
    i2 = jnp.min(jnp.where(el2 == m2, lane, big), -1, keepdims=True)
    e2 = jnp.exp(m2 - m1)
    w1 = g_w / (1.0 + e2)
    w2 = g_w * e2 / (1.0 + e2)
    out = jnp.where(lane == 0, i1.astype(f32), 0.0)
    out = jnp.where(lane == 1, i2.astype(f32), out)
    out = jnp.where(lane == 2, w1, out)
    return jnp.where(lane == 3, w2, out)


def _merge_kernel(x_ref, ona_ref, ogdn_ref, gna_ref, ggdn_ref, ln0g_ref, ln0b_ref, wna_ref, wgdn_ref,
                  wout_ref, ln1g_ref, ln1b_ref, wr_ref, br_ref, h1_ref, route_ref):
    h = _layer_norm(x_ref[...], ln0g_ref[...], ln0b_ref[...])
    y_na = jnp.dot(ona_ref[...], wna_ref[...], preferred_element_type=f32)
    y_gdn = jnp.dot(ogdn_ref[...], wgdn_ref[...], preferred_element_type=f32)
    merged = _sigmoid(gna_ref[...].astype(f32)) * y_na + _sigmoid(ggdn_ref[...].astype(f32)) * y_gdn
    mix = jnp.dot(merged.astype(bf16), wout_ref[...], preferred_element_type=f32)
    h1 = _layer_norm(DN_ALPHA * h + mix, ln1g_ref[...], ln1b_ref[...])
    h1_ref[...] = h1
    route_ref[...] = _route(_dot3(h1, wr_ref[...]) + br_ref[...])


def _merge(x2d, o_na, o_gdn, p_real, ln0_g, ln0_b, w_na, w_gdn, w_out, ln1_g, ln1_b, w_route, b_route, tm):
    t = x2d.shape[0]
    row = lambda i: (i, 0)
    const = lambda i: (0, 0)
    return pl.pallas_call(
        _merge_kernel,
        grid=(t // tm,),
        in_specs=[
            pl.BlockSpec((tm, D_MODEL), row),
            pl.BlockSpec((tm, NA_W), row),
            pl.BlockSpec((tm, GDN_W), row),
            pl.BlockSpec((tm, D_MODEL), lambda i: (i, CB_GATE_NA // 8)),
            pl.BlockSpec((tm, D_MODEL), lambda i: (i, CB_GATE_GDN // 8)),
            pl.BlockSpec((1, D_MODEL), const),
            pl.BlockSpec((1, D_MODEL), const),
            pl.BlockSpec((NA_W, D_MODEL), const),
            pl.BlockSpec((GDN_W, D_MODEL), const),
            pl.BlockSpec((D_MODEL, D_MODEL), const),
            pl.BlockSpec((1, D_MODEL), const),
            pl.BlockSpec((1, D_MODEL), const),
            pl.BlockSpec((D_MODEL, LANES), const),
            pl.BlockSpec((1, LANES), const),
        ],
        out_specs=[pl.BlockSpec((tm, D_MODEL), row), pl.BlockSpec((tm, LANES), row)],
        out_shape=[jax.ShapeDtypeStruct((t, D_MODEL), f32), jax.ShapeDtypeStruct((t, LANES), f32)],
        compiler_params=_params("parallel"),
        name="merge",
    )(x2d, o_na, o_gdn, p_real, p_real, ln0_g, ln0_b, w_na, w_gdn, w_out, ln1_g, ln1_b, w_route, b_route)


def _sc_gather(table, idx):
    n = idx.shape[0]
    width = table.shape[1]
    mesh = plsc.VectorSubcoreMesh(core_axis_name="core", subcore_axis_name="subcore")
    n_workers = mesh.num_cores * mesh.num_subcores
    per_worker = n // n_workers
    assert n % n_workers == 0 and per_worker % GATHER_WINDOW == 0

    @functools.partial(
        pl.kernel, out_type=jax.ShapeDtypeStruct((n, width), table.dtype), mesh=mesh,
        scratch_types=[pltpu.VMEM((GATHER_WINDOW,), jnp.int32), pltpu.VMEM((GATHER_WINDOW, width), table.dtype)],
        name="row_gather")
    def gather(tbl_hbm, idx_hbm, out_hbm, idx_vmem, rows_vmem):
        worker = lax.axis_index("subcore") * mesh.num_cores + lax.axis_index("core")

        @pl.loop(0, per_worker // GATHER_WINDOW)
        def _(step):
            base = pl.multiple_of(worker * per_worker + step * GATHER_WINDOW, GATHER_WINDOW)
            pltpu.sync_copy(idx_hbm.at[pl.ds(base, GATHER_WINDOW)], idx_vmem)
            pltpu.sync_copy(tbl_hbm.at[idx_vmem], rows_vmem)
            pltpu.sync_copy(rows_vmem, out_hbm.at[pl.ds(base, GATHER_WINDOW)])

    return gather(table, idx)


def _expert_kernel(be_ref, na_ref, x_ref, wg_ref, wu_ref, wd_ref, y_ref):
    active = pl.program_id(0) < na_ref[0]

    @pl.when(jnp.logical_not(active))
    def _():
        y_ref[...] = jnp.zeros_like(y_ref)

    @pl.when(active)
    def _():
        x = x_ref[...].astype(bf16)
        gate = jnp.dot(x, wg_ref[...].astype(bf16), preferred_element_type=f32)
        up = jnp.dot(x, wu_ref[...].astype(bf16), preferred_element_type=f32)
        hdn = (_silu(gate) * up).astype(bf16)
        y_ref[...] = jnp.dot(hdn, wd_ref[...].astype(bf16), preferred_element_type=f32)


def _experts(x_pad, block_e, n_active, w_gate, w_up, w_down):
    nb = x_pad.shape[0] // EXPERT_ROWS
    grid_spec = pltpu.PrefetchScalarGridSpec(
        num_scalar_prefetch=2,
        grid=(nb,),
        in_specs=[
            pl.BlockSpec((EXPERT_ROWS, D_MODEL), lambda i, be, na: (i, 0)),
            pl.BlockSpec((None, D_MODEL, D_EXPERT), lambda i, be, na: (be[i], 0, 0)),
            pl.BlockSpec((None, D_MODEL, D_EXPERT), lambda i, be, na: (be[i], 0, 0)),
            pl.BlockSpec((None, D_EXPERT, D_MODEL), lambda i, be, na: (be[i], 0, 0)),
        ],
        out_specs=pl.BlockSpec((EXPERT_ROWS, D_MODEL), lambda i, be, na: (i, 0)),
    )
    return pl.pallas_call(
        _expert_kernel,
        grid_spec=grid_spec,
        out_shape=jax.ShapeDtypeStruct(x_pad.shape, f32),
        compiler_params=_params("arbitrary"),
        name="experts",
    )(block_e, n_active, x_pad, w_gate, w_up, w_down)


def _combine_kernel(h1_ref, y1_ref, y2_ref, route_ref, g_ref, b_ref, o_ref):
    r = route_ref[...]
    ffn = y1_ref[...] * r[:, 2:3] + y2_ref[...] * r[:, 3:4]
    o_ref[...] = _layer_norm(DN_ALPHA * h1_ref[...] + ffn, g_ref[...], b_ref[...])


def _combine(h1, y_pairs, route, ln_g, ln_b, tm):
    t = h1.shape[0]
    row = lambda i: (i, 0)
    const = lambda i: (0, 0)
    return pl.pallas_call(
        _combine_kernel,
        grid=(t // tm,),
        in_specs=[
            pl.BlockSpec((tm, D_MODEL), row),
            pl.BlockSpec((None, tm, D_MODEL), lambda i: (0, i, 0)),
            pl.BlockSpec((None, tm, D_MODEL), lambda i: (1, i, 0)),
            pl.BlockSpec((tm, LANES), row),
            pl.BlockSpec((1, D_MODEL), const),
            pl.BlockSpec((1, D_MODEL), const),
        ],
        out_specs=pl.BlockSpec((tm, D_MODEL), row),
        out_shape=jax.ShapeDtypeStruct((t, D_MODEL), f32),
        compiler_params=_params("parallel"),
        name="combine",
    )(h1, y_pairs, y_pairs, route, ln_g, ln_b)


def _dispatch_plan(e_idx, t):
    a = t * TOP_K
    nb = a // EXPERT_ROWS + N_EXPERTS
    flat_e = e_idx.reshape(-1)
    onehot = (flat_e[:, None] == jnp.arange(N_EXPERTS, dtype=jnp.int32)[None, :]).astype(jnp.int32)
    csum = jnp.cumsum(onehot, axis=0)
    rank = jnp.sum(csum * onehot, axis=1) - 1
    counts = csum[-1]
    padded = (counts + EXPERT_ROWS - 1) // EXPERT_ROWS * EXPERT_ROWS
    pad_end = jnp.cumsum(padded)
    dest = (pad_end - padded)[flat_e] + rank
    flat_tok = jnp.arange(a, dtype=jnp.int32) // TOP_K
    src_tok = (jnp.arange(nb * EXPERT_ROWS, dtype=jnp.int32) % t).at[dest].set(flat_tok)
    block_start = jnp.arange(nb, dtype=jnp.int32) * EXPERT_ROWS
    block_e = jnp.sum((pad_end[None, :] <= block_start[:, None]).astype(jnp.int32), axis=1)
    block_e = jnp.minimum(block_e, N_EXPERTS - 1)
    n_active = (pad_end[-1:] // EXPERT_ROWS).astype(jnp.int32)
    dest_by_choice = dest.astype(jnp.int32).reshape(t, TOP_K).T.reshape(-1)
    return src_tok, dest_by_choice, block_e, n_active


def _row_tile(t, want):
    tm = min(t, want)
    assert t % tm == 0
    return tm


def kernel(x, meta_tokens, ln0_g, ln0_b, w_in, na_rpb, gdn_conv_w, gdn_a_log, gdn_dt_bias, gdn_norm_w,
           w_branch_na, w_branch_gdn, w_out, ln1_g, ln1_b, router_group_w, router_group_b, router_expert_w,
           router_expert_b, expert_w_gate, expert_w_up, expert_w_down, ln2_g, ln2_b):
    batch, seq, d = x.shape
    assert d == D_MODEL and seq % GRID_W == 0 and DEPTH == 1
    t = batch * seq
    l = 0
    x2d = x.reshape(t, d)
    vec = lambda v: v.reshape(1, -1).astype(f32)

    w = w_in[l]
    w_main = jnp.concatenate([w[:, 3 * NA_W:AB_OFF], w[:, AB_OFF + N_AB:], w[:, :3 * NA_W]], axis=1).astype(bf16)
    w_ab, alog_row, dtb_row = _gate_lane_layout(w, gdn_a_log[l], gdn_dt_bias[l])

    p_real, gb_real = _in_proj(x2d, vec(ln0_g), vec(ln0_b), w_main, w_ab, alog_row, dtb_row, _row_tile(t, 1024), 1280)
    p_meta, gb_meta = _in_proj(meta_tokens.astype(f32), vec(ln0_g), vec(ln0_b), w_main, w_ab, alog_row, dtb_row,
                               N_META, 1280)

    o_na = _na(p_real, p_meta, _na_bias_table(na_rpb[l]), batch, seq)
    o_gdn = _gdn(p_real, p_meta, gdn_conv_w[l].astype(f32), gb_real, gb_meta, vec(gdn_norm_w[l]), batch, seq)

    w_route = jnp.pad(jnp.concatenate([router_expert_w[l], router_group_w[l]], axis=1).astype(f32),
                      ((0, 0), (0, LANES - N_EXPERTS - N_GROUPS)))
    b_route = jnp.pad(jnp.concatenate([router_expert_b[l], router_group_b[l]]).astype(f32),
                      (0, LANES - N_EXPERTS - N_GROUPS)).reshape(1, LANES)
    h1, route = _merge(x2d, o_na, o_gdn, p_real, vec(ln0_g), vec(ln0_b), w_branch_na[l].astype(bf16),
                       w_branch_gdn[l].astype(bf16), w_out[l].astype(bf16), vec(ln1_g[l]), vec(ln1_b[l]),
                       w_route, b_route, _row_tile(t, 512))

    e_idx = route[:, :TOP_K].astype(jnp.int32)
    src_tok, dest, block_e, n_active = _dispatch_plan(e_idx, t)
    x_pad = _sc_gather(h1, src_tok)
    y_pad = _experts(x_pad, block_e, n_active, expert_w_gate[l], expert_w_up[l], expert_w_down[l])
    y_pairs = _sc_gather(y_pad, dest).reshape(TOP_K, t, D_MODEL)
    out = _combine(h1, y_pairs, route, vec(ln2_g[l]), vec(ln2_b[l]), _row_tile(t, 512))
    return out.reshape(batch, seq, d)
```

```python
import functools

import numpy as np
import jax
import jax.numpy as jnp
from jax import lax
from jax.experimental import pallas as pl
from jax.experimental.pallas import tpu as pltpu
from jax.experimental.pallas import tpu_sc as plsc

D_MODEL = 1024
DEPTH = 1
GRID_W = 64
N_META = 16
NA_HEADS = 8
NA_HEAD_DIM = 64
NA_W = NA_HEADS * NA_HEAD_DIM
NA_WIN_ROWS = 8
NA_WIN_COLS = 16
GDN_HEADS = 8
GDN_HEAD_DIM = 128
GDN_W = GDN_HEADS * GDN_HEAD_DIM
GDN_CONV = 5
GDN_CHUNK = 64
N_GROUPS = 4
EXPERTS_PER_GROUP = 8
N_EXPERTS = N_GROUPS * EXPERTS_PER_GROUP
TOP_K = 2
D_EXPERT = 256
LN_EPS = 1e-5
NORM_EPS = 1e-6
DN_ALPHA = (2 * DEPTH) ** 0.25

LANES = 128
VMEM_LIMIT = 48 * 1024 * 1024
NEG = -1e30

CB_G_Q, CB_G_K, CB_G_V, CB_G_Z = 0, 8, 16, 24
CB_GATE_NA, CB_GATE_GDN = 32, 40
CB_NA_Q, CB_NA_K, CB_NA_V = 48, 52, 56
P_COLS = 60 * LANES
AB_OFF = 3 * NA_W + 4 * GDN_W
N_AB = 4 * GDN_HEADS

IN_PROJ_COLS = 2560
NA_ROWS_PER_ITER = 4
EXPERT_ROWS = 256
GATHER_WINDOW = 128

f32 = jnp.float32
bf16 = jnp.bfloat16


def _dot(a, b):
    return jnp.dot(a.astype(bf16), b.astype(bf16), preferred_element_type=f32)


def _dot_nt(a, b):
    return lax.dot_general(a.astype(bf16), b.astype(bf16), (((1,), (1,)), ((), ())),
                           preferred_element_type=f32)


def _dot_tn(a, b):
    return lax.dot_general(a.astype(bf16), b.astype(bf16), (((0,), (0,)), ((), ())),
                           preferred_element_type=f32)


def _split_bf16(x):
    hi = x.astype(bf16)
    lo = (x - hi.astype(f32)).astype(bf16)
    return hi, lo


def _dot3(x, w):
    xh, xl = _split_bf16(x)
    wh, wl = _split_bf16(w)
    d = lambda a, b: jnp.dot(a, b, preferred_element_type=f32)
    return d(xh, wh) + (d(xl, wh) + d(xh, wl))


def _layer_norm(x, g, b):
    xc = x - jnp.mean(x, -1, keepdims=True)
    var = jnp.mean(xc * xc, -1, keepdims=True)
    return xc * lax.rsqrt(var + LN_EPS) * g + b


def _sigmoid(x):
    return 1.0 / (1.0 + jnp.exp(-x))


def _silu(x):
    return x * _sigmoid(x)


HALF = D_MODEL // 2


def _pack_halves(x):
    def rounded(v):
        u = lax.bitcast_convert_type(v, jnp.uint32)
        return u + (jnp.uint32(0x7FFF) + ((u >> 16) & jnp.uint32(1)))

    word = (rounded(x[:, :HALF]) >> 16) | (rounded(x[:, HALF:]) & jnp.uint32(0xFFFF0000))
    return lax.bitcast_convert_type(word, f32)


def _unpack_halves(p):
    u = lax.bitcast_convert_type(p, jnp.uint32)
    return lax.bitcast_convert_type(u << 16, f32), lax.bitcast_convert_type(u & jnp.uint32(0xFFFF0000), f32)


def _params(*sem):
    return pltpu.CompilerParams(dimension_semantics=sem, vmem_limit_bytes=VMEM_LIMIT)


def _in_proj_kernel(x_ref, g_ref, b_ref, w_ref, wab_ref, alog_ref, dtb_ref, p_ref, gb_ref, hn_ref):
    @pl.when(pl.program_id(1) == 0)
    def _():
        h = _layer_norm(x_ref[...], g_ref[...], b_ref[...])
        hn_ref[...] = h.astype(bf16)
        ab = _dot3(h, wab_ref[...])
        x = ab + dtb_ref[...]
        softplus = jnp.maximum(x, 0.0) + jnp.log1p(jnp.exp(-jnp.abs(x)))
        lane = lax.broadcasted_iota(jnp.int32, ab.shape, 1)
        gb_ref[...] = jnp.where((lane & 3) < 2, -jnp.exp(alog_ref[...]) * softplus, _sigmoid(ab))

    p_ref[...] = jnp.dot(hn_ref[...], w_ref[...], preferred_element_type=f32).astype(bf16)


def _gate_lane_layout(w_in, a_log, dt_bias):
    w_ab = w_in[:, AB_OFF:AB_OFF + N_AB].astype(f32).reshape(D_MODEL, 4, GDN_HEADS).transpose(0, 2, 1)
    w_ab = jnp.pad(w_ab.reshape(D_MODEL, N_AB), ((0, 0), (0, LANES - N_AB)))
    row = lambda p: jnp.pad(jnp.pad(p.astype(f32).T, ((0, 0), (0, 2))).reshape(1, N_AB), ((0, 0), (0, LANES - N_AB)))
    return w_ab, row(a_log), row(dt_bias)


def _in_proj(x2d, ln_g, ln_b, w_main, w_ab, alog_row, dtb_row, tm, tn):
    t = x2d.shape[0]
    return pl.pallas_call(
        _in_proj_kernel,
        grid=(t // tm, P_COLS // tn),
        in_specs=[
            pl.BlockSpec((tm, D_MODEL), lambda i, j: (i, 0)),
            pl.BlockSpec((1, D_MODEL), lambda i, j: (0, 0)),
            pl.BlockSpec((1, D_MODEL), lambda i, j: (0, 0)),
            pl.BlockSpec((D_MODEL, tn), lambda i, j: (0, j)),
            pl.BlockSpec((D_MODEL, LANES), lambda i, j: (0, 0)),
            pl.BlockSpec((1, LANES), lambda i, j: (0, 0)),
            pl.BlockSpec((1, LANES), lambda i, j: (0, 0)),
        ],
        out_specs=[
            pl.BlockSpec((tm, tn), lambda i, j: (i, j)),
            pl.BlockSpec((tm, LANES), lambda i, j: (i, 0)),
        ],
        out_shape=[
            jax.ShapeDtypeStruct((t, P_COLS), bf16),
            jax.ShapeDtypeStruct((t, LANES), f32),
        ],
        scratch_shapes=[pltpu.VMEM((tm, D_MODEL), bf16)],
        compiler_params=_params("parallel", "arbitrary"),
        name="in_proj",
    )(x2d, ln_g, ln_b, w_main, w_ab, alog_row, dtb_row)


def _na_kernel(q_ref, k_ref, v_ref, km_ref, vm_ref, bias_ref, o_ref, *, rows):
    scale = NA_HEAD_DIM ** -0.5
    kr = NA_WIN_ROWS
    lane = lax.broadcasted_iota(jnp.int32, (GRID_W, LANES), 1)
    low = lane < NA_HEAD_DIM
    first = lax.broadcasted_iota(jnp.int32, (2 * GRID_W, LANES), 0) < GRID_W
    lane2 = lax.broadcasted_iota(jnp.int32, (2 * GRID_W, LANES), 1)
    pair_mask = jnp.where(first == (lane2 < NA_HEAD_DIM), scale, 0.0).astype(bf16)
    pairs = range(NA_HEADS // 2)
    sl = [slice(LANES * p, LANES * (p + 1)) for p in pairs]

    nr = NA_ROWS_PER_ITER
    assert rows % nr == 0

    def row_block(it, carry):
        units = [(i, p) for i in range(nr) for p in pairs]
        r = [it * nr + i for i in range(nr)]
        r0 = [jnp.clip(r[i] - kr // 2, 0, rows - kr) for i in range(nr)]
        d0 = [r0[i] - r[i] + (NA_WIN_ROWS - 1) for i in range(nr)]
        qoff = [pl.multiple_of(r[i] * GRID_W, GRID_W) for i in range(nr)]
        koff = [pl.multiple_of(r0[i] * GRID_W, GRID_W) for i in range(nr)]
        q2 = [jnp.concatenate([q_ref[pl.ds(qoff[i], GRID_W), sl[p]]] * 2, axis=0) * pair_mask for i, p in units]
        s = [_dot_nt(q, k_ref[pl.ds(koff[i], kr * GRID_W), sl[p]]) + bias_ref[p, d0[i]] for (i, p), q in zip(units, q2)]
        sm = [_dot_nt(q, km_ref[:, sl[p]]) for (i, p), q in zip(units, q2)]
        m = [jnp.maximum(jnp.max(a, -1, keepdims=True), jnp.max(b, -1, keepdims=True)) for a, b in zip(s, sm)]
        e = [jnp.exp(a - c) for a, c in zip(s, m)]
        em = [jnp.exp(b - c) for b, c in zip(sm, m)]
        den = [jnp.sum(a, -1, keepdims=True) + jnp.sum(b, -1, keepdims=True) for a, b in zip(e, em)]
        o = [_dot(a, v_ref[pl.ds(koff[i], kr * GRID_W), sl[p]]) + _dot(b, vm_ref[:, sl[p]])
             for (i, p), a, b in zip(units, e, em)]
        o = [a / c for a, c in zip(o, den)]
        for (i, p), a in zip(units, o):
            o_ref[pl.ds(qoff[i], GRID_W), sl[p]] = jnp.where(low, a[:GRID_W], a[GRID_W:]).astype(bf16)
        return carry

    lax.fori_loop(0, rows // nr, row_block, 0)


def _na_bias_table(rpb):
    col = np.arange(GRID_W)
    col_start = np.clip(col - NA_WIN_COLS // 2, 0, GRID_W - NA_WIN_COLS)
    col_mask = (col[None, :] >= col_start[:, None]) & (col[None, :] < col_start[:, None] + NA_WIN_COLS)
    dc_idx = np.clip(col[None, :] - col[:, None] + NA_WIN_COLS - 1, 0, 2 * NA_WIN_COLS - 2)
    pick = (dc_idx[:, :, None] == np.arange(2 * NA_WIN_COLS - 1)).astype(np.float32)
    by_col = jnp.einsum("hrc,qkc->hrqk", rpb.astype(f32), pick, precision=lax.Precision.HIGHEST)
    by_col = jnp.where(col_mask[None, None], by_col, NEG)
    tbl = jnp.stack([by_col[:, d0:d0 + NA_WIN_ROWS] for d0 in range(NA_WIN_ROWS)], axis=1)
    tbl = tbl.reshape(NA_HEADS // 2, 2, NA_WIN_ROWS, NA_WIN_ROWS, GRID_W, GRID_W)
    return tbl.transpose(0, 2, 1, 4, 3, 5).reshape(NA_HEADS // 2, NA_WIN_ROWS, 2 * GRID_W, NA_WIN_ROWS * GRID_W)


def _na(p_real, p_meta, bias_tbl, batch, seq):
    rows = seq // GRID_W
    assert rows >= NA_WIN_ROWS
    col = lambda cb: (lambda b: (b, cb // 4))
    colm = lambda cb: (lambda b: (0, cb // 4))
    return pl.pallas_call(
        functools.partial(_na_kernel, rows=rows),
        grid=(batch,),
        in_specs=[
            pl.BlockSpec((seq, NA_W), col(CB_NA_Q)),
            pl.BlockSpec((seq, NA_W), col(CB_NA_K)),
            pl.BlockSpec((seq, NA_W), col(CB_NA_V)),
            pl.BlockSpec((N_META, NA_W), colm(CB_NA_K)),
            pl.BlockSpec((N_META, NA_W), colm(CB_NA_V)),
            pl.BlockSpec(bias_tbl.shape, lambda b: (0, 0, 0, 0)),
        ],
        out_specs=pl.BlockSpec((seq, NA_W), lambda b: (b, 0)),
        out_shape=jax.ShapeDtypeStruct((batch * seq, NA_W), bf16),
        compiler_params=_params("parallel"),
        name="na",
    )(p_real, p_real, p_real, p_meta, p_meta, bias_tbl)


GDN_LEAD = GDN_CHUNK - N_META
RAW_PAD = 8


GDN_BLOCK = 128
GDN_HEADS_PER_STEP = 2
GDN_BLOCKS_PER_ITER = 4


INV_BASE = 8


def _tri_masks(c, lower):
    ii = lax.broadcasted_iota(jnp.int32, (c, c), 0)
    jj = lax.broadcasted_iota(jnp.int32, (c, c), 1)
    same = lambda s: jnp.right_shift(ii, s.bit_length() - 1) == jnp.right_shift(jj, s.bit_length() - 1)
    levels = [same(INV_BASE)]
    s = INV_BASE
    while s < c:
        levels.append(same(2 * s) & jnp.logical_not(same(s)))
        s *= 2
    if lower:
        return ii == jj, jj <= ii, jj < ii, ii <= jj, levels
    return ii == jj, jj >= ii, jj > ii, ii >= jj, levels


def _unit_tri_inverse(ms, eye, levels):
    pws = [jnp.where(levels[0], -m, 0.0) for m in ms]
    inv = [jnp.where(eye, 1.0, 0.0) + n for n in pws]
    for _ in range(INV_BASE.bit_length() - 2):
        pws = [_dot(p, p) for p in pws]
        inv = [a + _dot(a, p) for a, p in zip(inv, pws)]
    for pair in levels[1:]:
        ts = [_dot(a, jnp.where(pair, m, 0.0)) for a, m in zip(inv, ms)]
        inv = [a - _dot(t, a) for a, t in zip(inv, ts)]
    return inv


def _gdn_prepare(chains):
    pre = []
    for qc, kc, vc, kk, qk, g_col, beta_col, masks in chains:
        eye, incl, strict, incl_t, levels = masks
        g_row = jnp.sum(jnp.where(eye, g_col, 0.0), axis=0, keepdims=True)
        gc_col = jnp.sum(jnp.where(incl, g_row, 0.0), axis=1, keepdims=True)
        gc_row = jnp.sum(jnp.where(incl_t, g_col, 0.0), axis=0, keepdims=True)
        total = jnp.sum(g_col, axis=0, keepdims=True)
        decay = jnp.where(incl, jnp.exp(jnp.minimum(gc_col - gc_row, 0.0)), 0.0)
        pre.append((gc_col, total, decay, jnp.where(strict, kk * beta_col * decay, 0.0)))
    eye, levels = chains[0][7][0], chains[0][7][4]
    inv = _unit_tri_inverse([p[3] for p in pre], eye, levels)
    out = []
    sols = []
    for (qc, kc, vc, kk, qk, g_col, beta_col, masks), (gc_col, total, decay, _), a in zip(chains, pre, inv):
        e_col = jnp.exp(gc_col)
        sols.append(_dot(a, jnp.concatenate([vc * beta_col, kc * (beta_col * e_col)], axis=1)))
    for (qc, kc, vc, kk, qk, g_col, beta_col, masks), (gc_col, total, decay, _), sol in zip(chains, pre, sols):
        out.append((sol[:, :GDN_HEAD_DIM], sol[:, GDN_HEAD_DIM:], qc * jnp.exp(gc_col), qk * decay,
                    kc * jnp.exp(total - gc_col), jnp.exp(total)))
    return out


def _gdn_kernel(q_ref, k_ref, v_ref, z_ref, qm_ref, km_ref, vm_ref,
                wq_ref, wk_ref, wv_ref, gt_ref, gtm_ref, nw_ref, o_ref,
                raw, u_s, wqd_s, aqk_s, kd_s, gl_s, st_s, oacc, *, seq, hg):
    c = GDN_BLOCK
    cm = GDN_CHUNK
    nblk = seq // c
    h0 = pl.program_id(1) * hg
    in_refs = (q_ref, k_ref, v_ref)
    meta_refs = (qm_ref, km_ref, vm_ref)
    w_refs = (wq_ref, wk_ref, wv_ref)
    cols = lambda hh: slice(LANES * hh, LANES * (hh + 1))
    halo = GDN_CONV // 2
    lead = RAW_PAD + GDN_LEAD

    for hh in range(hg):
        for t in range(3):
            raw[hh, t, pl.ds(0, lead), :] = jnp.zeros((lead, LANES), f32)
            raw[hh, t, pl.ds(lead, N_META), :] = meta_refs[t][:, cols(hh)].astype(f32)
            raw[hh, t, pl.ds(lead + N_META + seq, RAW_PAD), :] = jnp.zeros((RAW_PAD, LANES), f32)

    def copy_in(j, carry):
        src = pl.multiple_of(j * c, c)
        dst = pl.multiple_of(lead + N_META + j * c, 8)
        for hh in range(hg):
            for t in range(3):
                raw[hh, t, pl.ds(dst, c), :] = in_refs[t][pl.ds(src, c), cols(hh)].astype(f32)
        return carry

    lax.fori_loop(0, nblk, copy_in, 0)

    def conv_block(hh, p0, rows):
        out = []
        for t, kind in enumerate("qkv"):
            win = raw[hh, t, pl.ds(p0, rows + 2 * RAW_PAD), :]
            w = w_refs[t][:, cols(hh)]
            acc = None
            for tap in range(GDN_CONV):
                off = RAW_PAD - halo + tap
                term = win[off:off + rows, :] * w[tap:tap + 1, :]
                acc = term if acc is None else acc + term
            y = _silu(acc)
            if kind != "v":
                y = y * lax.rsqrt(jnp.sum(y * y, -1, keepdims=True) + NORM_EPS)
            if kind == "q":
                y = y * (GDN_HEAD_DIM ** -0.5)
            out.append(y)
        return out

    def gates(gb, hh):
        return pltpu.roll(gb, jnp.bitwise_and(LANES - 4 * (h0 + hh), LANES - 1), 1)

    masks_meta = _tri_masks(cm, True)
    live_col = lax.broadcasted_iota(jnp.int32, (cm, 1), 0) >= GDN_LEAD
    heads = range(hg)
    metas = [[jnp.where(live_col, y, 0.0) for y in conv_block(hh, 0, cm)] for hh in heads]
    qkk = [_dot_nt(jnp.concatenate([q0, k0], axis=0), k0) for q0, k0, _ in metas]
    chain_in = []
    gb_meta = jnp.concatenate([jnp.zeros((GDN_LEAD, LANES), f32), gtm_ref[...]], axis=0)
    for hh in heads:
        q0, k0, v0 = metas[hh]
        gm = gates(gb_meta, hh)
        for d in range(2):
            chain_in.append((q0, k0, v0, qkk[hh][cm:], qkk[hh][:cm], gm[:, d:d + 1], gm[:, 2 + d:3 + d], masks_meta))
    s0 = [_dot_tn(r[4], r[0]) for r in _gdn_prepare(chain_in)]
    for i, (hh, d) in enumerate((hh, d) for hh in heads for d in range(2)):
        st_s[hh, d] = s0[i]

    masks = (_tri_masks(c, True), _tri_masks(c, False))
    nb = GDN_BLOCKS_PER_ITER
    assert nblk % nb == 0

    def prepare_blocks(jj, carry):
        units = [(hh, b) for hh in heads for b in range(nb)]
        rows = [pl.multiple_of((jj * nb + b) * c, c) for b in range(nb)]
        convs = [conv_block(hh, pl.multiple_of(cm + rows[j], cm), c) for hh, j in units]
        qkk = [_dot_nt(jnp.concatenate([qc, kc], axis=0), kc) for qc, kc, _ in convs]
        chain_in, chain_id = [], []
        for (hh, j), (qc, kc, vc), qk_kk in zip(units, convs, qkk):
            gt = gates(gt_ref[pl.ds(rows[j], c), :], hh)
            for d in range(2):
                chain_in.append((qc, kc, vc, qk_kk[c:], qk_kk[:c], gt[:, d:d + 1], gt[:, 2 + d:3 + d], masks[d]))
                chain_id.append((hh, d, j))
        for (hh, d, j), (u, w, qd, aqk, kd, g_last) in zip(chain_id, _gdn_prepare(chain_in)):
            r0 = rows[j]
            u_s[hh, d, pl.ds(r0, c), :] = u
            wqd_s[hh, d, pl.ds(pl.multiple_of(2 * r0, 2 * c), c), :] = w.astype(bf16)
            wqd_s[hh, d, pl.ds(pl.multiple_of(2 * r0 + c, c), c), :] = qd.astype(bf16)
            aqk_s[hh, d, pl.ds(r0, c), :] = aqk.astype(bf16)
            kd_s[hh, d, pl.ds(r0, c), :] = kd.astype(bf16)
            gl_s[hh, d, pl.ds(pl.multiple_of((jj * nb + j) * 8, 8), 8), :] = jnp.broadcast_to(g_last, (8, LANES))
        for hh, j in units:
            oacc[hh, pl.ds(rows[j], c), :] = jnp.zeros((c, LANES), f32)
        return carry

    lax.fori_loop(0, nblk // nb, prepare_blocks, 0)

    def scan_step(j, carry):
        rows = (pl.multiple_of(j * c, c), pl.multiple_of((nblk - 1 - j) * c, c))
        grow = (pl.multiple_of(j * 8, 8), pl.multiple_of((nblk - 1 - j) * 8, 8))
        chains = [(hh, d) for hh in heads for d in range(2)]
        states = [st_s[hh, d] for hh, d in chains]
        ws = [jnp.dot(wqd_s[hh, d, pl.ds(pl.multiple_of(2 * rows[d], 2 * c), 2 * c), :], s.astype(bf16),
                      preferred_element_type=f32) for (hh, d), s in zip(chains, states)]
        v_new = [(u_s[hh, d, pl.ds(rows[d], c), :] - w[:c]).astype(bf16) for (hh, d), w in zip(chains, ws)]
        outs = [w[c:] + jnp.dot(aqk_s[hh, d, pl.ds(rows[d], c), :], v, preferred_element_type=f32)
                for (hh, d), w, v in zip(chains, ws, v_new)]
        new_states = [s * gl_s[hh, d, pl.ds(grow[d], 1), :] + _dot_tn(kd_s[hh, d, pl.ds(rows[d], c), :], v)
                      for (hh, d), s, v in zip(chains, states, v_new)]
        for (hh, d), s in zip(chains, new_states):
            st_s[hh, d] = s
        for (hh, d), o in zip(chains, outs):
            oacc[hh, pl.ds(rows[d], c), :] += o
        return carry

    lax.fori_loop(0, nblk, scan_step, 0)

    def finish(j, carry):
        r0 = pl.multiple_of(j * c, c)
        for hh in range(hg):
            o = oacc[hh, pl.ds(r0, c), :]
            o = o * lax.rsqrt(jnp.mean(o * o, -1, keepdims=True) + NORM_EPS) * nw_ref[...]
            o_ref[pl.ds(r0, c), cols(hh)] = (o * _silu(z_ref[pl.ds(r0, c), cols(hh)].astype(f32))).astype(bf16)
        return carry

    lax.fori_loop(0, nblk, finish, 0)


def _gdn(p_real, p_meta, conv_w, gates_real, gates_meta, norm_w, batch, seq):
    hg = GDN_HEADS_PER_STEP
    assert seq % GDN_BLOCK == 0 and GDN_HEADS % hg == 0
    nblk = seq // GDN_BLOCK
    wide = hg * LANES
    col = lambda cb: (lambda b, h: (b, cb // hg + h))
    colm = lambda cb: (lambda b, h: (0, cb // hg + h))
    colw = lambda k: (lambda b, h: (0, k * (GDN_HEADS // hg) + h))
    return pl.pallas_call(
        functools.partial(_gdn_kernel, seq=seq, hg=hg),
        grid=(batch, GDN_HEADS // hg),
        in_specs=[
            pl.BlockSpec((seq, wide), col(CB_G_Q)),
            pl.BlockSpec((seq, wide), col(CB_G_K)),
            pl.BlockSpec((seq, wide), col(CB_G_V)),
            pl.BlockSpec((seq, wide), col(CB_G_Z)),
            pl.BlockSpec((N_META, wide), colm(CB_G_Q)),
            pl.BlockSpec((N_META, wide), colm(CB_G_K)),
            pl.BlockSpec((N_META, wide), colm(CB_G_V)),
            pl.BlockSpec((GDN_CONV, wide), colw(0)),
            pl.BlockSpec((GDN_CONV, wide), colw(1)),
            pl.BlockSpec((GDN_CONV, wide), colw(2)),
            pl.BlockSpec((seq, LANES), lambda b, h: (b, 0)),
            pl.BlockSpec((N_META, LANES), lambda b, h: (0, 0)),
            pl.BlockSpec((1, LANES), lambda b, h: (0, 0)),
        ],
        out_specs=pl.BlockSpec((seq, wide), lambda b, h: (b, h)),
        out_shape=jax.ShapeDtypeStruct((batch * seq, GDN_W), bf16),
        scratch_shapes=[
            pltpu.VMEM((hg, 3, seq + GDN_CHUNK + 2 * RAW_PAD, LANES), f32),
            pltpu.VMEM((hg, 2, seq, LANES), f32),
            pltpu.VMEM((hg, 2, 2 * seq, LANES), bf16),
            pltpu.VMEM((hg, 2, seq, LANES), bf16),
            pltpu.VMEM((hg, 2, seq, LANES), bf16),
            pltpu.VMEM((hg, 2, 8 * nblk, LANES), f32),
            pltpu.VMEM((hg, 2, GDN_HEAD_DIM, GDN_HEAD_DIM), f32),
            pltpu.VMEM((hg, seq, LANES), f32),
        ],
        compiler_params=_params("parallel", "parallel"),
        name="gdn",
    )(p_real, p_real, p_real, p_real, p_meta, p_meta, p_meta, conv_w, conv_w, conv_w, gates_real, gates_meta, norm_w)


def _route(logits):
    lane = lax.broadcasted_iota(jnp.int32, logits.shape, 1)
    big = jnp.int32(1 << 20)
    is_g = (lane >= N_EXPERTS) & (lane < N_EXPERTS + N_GROUPS)
    gl = jnp.where(is_g, logits, -jnp.inf)
    gmax = jnp.max(gl, -1, keepdims=True)
    g_idx = jnp.min(jnp.where(gl == gmax, lane - N_EXPERTS, big), -1, keepdims=True)
    g_w = 1.0 / jnp.sum(jnp.where(is_g, jnp.exp(gl - gmax), 0.0), -1, keepdims=True)
    in_grp = (lane < N_EXPERTS) & (jnp.right_shift(lane, 3) == g_idx)
    el = jnp.where(in_grp, logits, -jnp.inf)
    m1 = jnp.max(el, -1, keepdims=True)
    i1 = jnp.min(jnp.where(el == m1, lane, big), -1, keepdims=True)
    el2 = jnp.where(lane == i1, -jnp.inf, el)
    m2 = jnp.max(el2, -1, keepdims=True)
    i2 = jnp.min(jnp.where(el2 == m2, lane, big), -1, keepdims=True)
    e2 = jnp.exp(m2 - m1)
    w1 = g_w / (1.0 + e2)
    w2 = g_w * e2 / (1.0 + e2)
    out = jnp.where(lane == 0, i1.astype(f32), 0.0)
    out = jnp.where(lane == 1, i2.astype(f32), out)
    out = jnp.where(lane == 2, w1, out)
    return jnp.where(lane == 3, w2, out)


def _merge_kernel(x_ref, ona_ref, ogdn_ref, gna_ref, ggdn_ref, ln0g_ref, ln0b_ref, wna_ref, wgdn_ref,
                  wout_ref, ln1g_ref, ln1b_ref, wr_ref, br_ref, h1_ref, h1p_ref, route_ref):
    h = _layer_norm(x_ref[...], ln0g_ref[...], ln0b_ref[...])
    y_na = jnp.dot(ona_ref[...], wna_ref[...], preferred_element_type=f32)
    y_gdn = jnp.dot(ogdn_ref[...], wgdn_ref[...], preferred_element_type=f32)
    merged = _sigmoid(gna_ref[...].astype(f32)) * y_na + _sigmoid(ggdn_ref[...].astype(f32)) * y_gdn
    mix = jnp.dot(merged.astype(bf16), wout_ref[...], preferred_element_type=f32)
    h1 = _layer_norm(DN_ALPHA * h + mix, ln1g_ref[...], ln1b_ref[...])
    h1_ref[...] = h1
    h1p_ref[...] = _pack_halves(h1)
    route_ref[...] = _route(_dot3(h1, wr_ref[...]) + br_ref[...])


def _merge(x2d, o_na, o_gdn, p_real, ln0_g, ln0_b, w_na, w_gdn, w_out, ln1_g, ln1_b, w_route, b_route, tm):
    t = x2d.shape[0]
    row = lambda i: (i, 0)
    const = lambda i: (0, 0)
    return pl.pallas_call(
        _merge_kernel,
        grid=(t // tm,),
        in_specs=[
            pl.BlockSpec((tm, D_MODEL), row),
            pl.BlockSpec((tm, NA_W), row),
            pl.BlockSpec((tm, GDN_W), row),
            pl.BlockSpec((tm, D_MODEL), lambda i: (i, CB_GATE_NA // 8)),
            pl.BlockSpec((tm, D_MODEL), lambda i: (i, CB_GATE_GDN // 8)),
            pl.BlockSpec((1, D_MODEL), const),
            pl.BlockSpec((1, D_MODEL), const),
            pl.BlockSpec((NA_W, D_MODEL), const),
            pl.BlockSpec((GDN_W, D_MODEL), const),
            pl.BlockSpec((D_MODEL, D_MODEL), const),
            pl.BlockSpec((1, D_MODEL), const),
            pl.BlockSpec((1, D_MODEL), const),
            pl.BlockSpec((D_MODEL, LANES), const),
            pl.BlockSpec((1, LANES), const),
        ],
        out_specs=[pl.BlockSpec((tm, D_MODEL), row), pl.BlockSpec((tm, HALF), row), pl.BlockSpec((tm, LANES), row)],
        out_shape=[jax.ShapeDtypeStruct((t, D_MODEL), f32), jax.ShapeDtypeStruct((t, HALF), f32),
                   jax.ShapeDtypeStruct((t, LANES), f32)],
        compiler_params=_params("parallel"),
        name="merge",
    )(x2d, o_na, o_gdn, p_real, p_real, ln0_g, ln0_b, w_na, w_gdn, w_out, ln1_g, ln1_b, w_route, b_route)


def _sc_dispatch(rows, dest0, dest1, n_slots):
    t, width = rows.shape
    mesh = plsc.VectorSubcoreMesh(core_axis_name="core", subcore_axis_name="subcore")
    n_workers = mesh.num_cores * mesh.num_subcores
    per_worker = t // n_workers
    assert t % n_workers == 0 and per_worker % GATHER_WINDOW == 0

    @functools.partial(
        pl.kernel, out_type=jax.ShapeDtypeStruct((n_slots, width), rows.dtype), mesh=mesh,
        scratch_types=[pltpu.VMEM((GATHER_WINDOW,), jnp.int32), pltpu.VMEM((GATHER_WINDOW,), jnp.int32),
                       pltpu.VMEM((GATHER_WINDOW, width), rows.dtype)],
        name="row_dispatch")
    def dispatch(rows_hbm, d0_hbm, d1_hbm, out_hbm, d0_vmem, d1_vmem, rows_vmem):
        worker = lax.axis_index("subcore") * mesh.num_cores + lax.axis_index("core")

        @pl.loop(0, per_worker // GATHER_WINDOW)
        def _(step):
            base = pl.multiple_of(worker * per_worker + step * GATHER_WINDOW, GATHER_WINDOW)
            pltpu.sync_copy(d0_hbm.at[pl.ds(base, GATHER_WINDOW)], d0_vmem)
            pltpu.sync_copy(d1_hbm.at[pl.ds(base, GATHER_WINDOW)], d1_vmem)
            pltpu.sync_copy(rows_hbm.at[pl.ds(base, GATHER_WINDOW)], rows_vmem)
            pltpu.sync_copy(rows_vmem, out_hbm.at[d0_vmem])
            pltpu.sync_copy(rows_vmem, out_hbm.at[d1_vmem])

    return dispatch(rows, dest0, dest1)


def _sc_gather(table, idx):
    n = idx.shape[0]
    width = table.shape[1]
    mesh = plsc.VectorSubcoreMesh(core_axis_name="core", subcore_axis_name="subcore")
    n_workers = mesh.num_cores * mesh.num_subcores
    per_worker = n // n_workers
    assert n % n_workers == 0 and per_worker % GATHER_WINDOW == 0

    @functools.partial(
        pl.kernel, out_type=jax.ShapeDtypeStruct((n, width), table.dtype), mesh=mesh,
        scratch_types=[pltpu.VMEM((GATHER_WINDOW,), jnp.int32), pltpu.VMEM((GATHER_WINDOW, width), table.dtype)],
        name="row_gather")
    def gather(tbl_hbm, idx_hbm, out_hbm, idx_vmem, rows_vmem):
        worker = lax.axis_index("subcore") * mesh.num_cores + lax.axis_index("core")

        @pl.loop(0, per_worker // GATHER_WINDOW)
        def _(step):
            base = pl.multiple_of(worker * per_worker + step * GATHER_WINDOW, GATHER_WINDOW)
            pltpu.sync_copy(idx_hbm.at[pl.ds(base, GATHER_WINDOW)], idx_vmem)
            pltpu.sync_copy(tbl_hbm.at[idx_vmem], rows_vmem)
            pltpu.sync_copy(rows_vmem, out_hbm.at[pl.ds(base, GATHER_WINDOW)])

    return gather(table, idx)


def _expert_kernel(be_ref, valid_ref, x_ref, wg_ref, wu_ref, wd_ref, y_ref, wgu_s, wd_s):
    i = pl.program_id(0)
    valid = valid_ref[i]

    @pl.when(valid == 0)
    def _():
        y_ref[...] = jnp.zeros_like(y_ref)

    @pl.when(valid > 0)
    def _():
        prev = jnp.maximum(i, 1) - 1
        fresh = jnp.logical_or(i == 0, jnp.logical_or(be_ref[i] != be_ref[prev], valid_ref[prev] == 0))

        @pl.when(fresh)
        def _():
            wgu_s[:, :D_EXPERT] = wg_ref[...].astype(bf16)
            wgu_s[:, D_EXPERT:] = wu_ref[...].astype(bf16)
            wd_s[...] = wd_ref[...].astype(bf16)

        row = lax.broadcasted_iota(jnp.int32, x_ref.shape, 0)
        xa, xb = _unpack_halves(jnp.where(row < valid, x_ref[...], 0.0))
        gu = (jnp.dot(xa.astype(bf16), wgu_s[:HALF, :], preferred_element_type=f32)
              + jnp.dot(xb.astype(bf16), wgu_s[HALF:, :], preferred_element_type=f32))
        hdn = (_silu(gu[:, :D_EXPERT]) * gu[:, D_EXPERT:]).astype(bf16)
        y_ref[...] = _pack_halves(jnp.dot(hdn, wd_s[...], preferred_element_type=f32))


def _experts(x_pad, block_e, block_valid, w_gate, w_up, w_down):
    nb = x_pad.shape[0] // EXPERT_ROWS
    grid_spec = pltpu.PrefetchScalarGridSpec(
        num_scalar_prefetch=2,
        grid=(nb,),
        in_specs=[
            pl.BlockSpec((EXPERT_ROWS, HALF), lambda i, be, nv: (i, 0)),
            pl.BlockSpec((None, D_MODEL, D_EXPERT), lambda i, be, nv: (be[i], 0, 0)),
            pl.BlockSpec((None, D_MODEL, D_EXPERT), lambda i, be, nv: (be[i], 0, 0)),
            pl.BlockSpec((None, D_EXPERT, D_MODEL), lambda i, be, nv: (be[i], 0, 0)),
        ],
        out_specs=pl.BlockSpec((EXPERT_ROWS, HALF), lambda i, be, nv: (i, 0)),
        scratch_shapes=[pltpu.VMEM((D_MODEL, 2 * D_EXPERT), bf16), pltpu.VMEM((D_EXPERT, D_MODEL), bf16)],
    )
    return pl.pallas_call(
        _expert_kernel,
        grid_spec=grid_spec,
        out_shape=jax.ShapeDtypeStruct(x_pad.shape, f32),
        compiler_params=_params("arbitrary"),
        name="experts",
    )(block_e, block_valid, x_pad, w_gate, w_up, w_down)


def _combine_kernel(h1_ref, y1_ref, y2_ref, route_ref, g_ref, b_ref, o_ref):
    r = route_ref[...]
    y1a, y1b = _unpack_halves(y1_ref[...])
    y2a, y2b = _unpack_halves(y2_ref[...])
    w1, w2 = r[:, 2:3], r[:, 3:4]
    ffn = jnp.concatenate([y1a * w1 + y2a * w2, y1b * w1 + y2b * w2], axis=1)
    o_ref[...] = _layer_norm(DN_ALPHA * h1_ref[...] + ffn, g_ref[...], b_ref[...])


def _combine(h1, y_pairs, route, ln_g, ln_b, tm):
    t = h1.shape[0]
    row = lambda i: (i, 0)
    const = lambda i: (0, 0)
    return pl.pallas_call(
        _combine_kernel,
        grid=(t // tm,),
        in_specs=[
            pl.BlockSpec((tm, D_MODEL), row),
            pl.BlockSpec((None, tm, HALF), lambda i: (0, i, 0)),
            pl.BlockSpec((None, tm, HALF), lambda i: (1, i, 0)),
            pl.BlockSpec((tm, LANES), row),
            pl.BlockSpec((1, D_MODEL), const),
            pl.BlockSpec((1, D_MODEL), const),
        ],
        out_specs=pl.BlockSpec((tm, D_MODEL), row),
        out_shape=jax.ShapeDtypeStruct((t, D_MODEL), f32),
        compiler_params=_params("parallel"),
        name="combine",
    )(h1, y_pairs, y_pairs, route, ln_g, ln_b)


def _dispatch_plan(e_idx, t):
    a = t * TOP_K
    nb = a // EXPERT_ROWS + N_EXPERTS
    flat_e = e_idx.reshape(-1)
    onehot = (flat_e[:, None] == jnp.arange(N_EXPERTS, dtype=jnp.int32)[None, :]).astype(jnp.int32)
    csum = jnp.cumsum(onehot, axis=0)
    rank = jnp.sum(csum * onehot, axis=1) - 1
    counts = csum[-1]
    padded = (counts + EXPERT_ROWS - 1) // EXPERT_ROWS * EXPERT_ROWS
    pad_end = jnp.cumsum(padded)
    pad_start = pad_end - padded
    dest = pad_start[flat_e] + rank
    block_start = jnp.arange(nb, dtype=jnp.int32) * EXPERT_ROWS
    block_e = jnp.sum((pad_end[None, :] <= block_start[:, None]).astype(jnp.int32), axis=1)
    block_e = jnp.minimum(block_e, N_EXPERTS - 1)
    block_valid = jnp.clip(counts[block_e] - (block_start - pad_start[block_e]), 0, EXPERT_ROWS)
    return dest.astype(jnp.int32).reshape(t, TOP_K).T, block_e.astype(jnp.int32), block_valid.astype(jnp.int32)


def _row_tile(t, want):
    tm = min(t, want)
    assert t % tm == 0
    return tm


def kernel(x, meta_tokens, ln0_g, ln0_b, w_in, na_rpb, gdn_conv_w, gdn_a_log, gdn_dt_bias, gdn_norm_w,
           w_branch_na, w_branch_gdn, w_out, ln1_g, ln1_b, router_group_w, router_group_b, router_expert_w,
           router_expert_b, expert_w_gate, expert_w_up, expert_w_down, ln2_g, ln2_b):
    batch, seq, d = x.shape
    assert d == D_MODEL and seq % GRID_W == 0 and DEPTH == 1
    t = batch * seq
    l = 0
    x2d = x.reshape(t, d)
    vec = lambda v: v.reshape(1, -1).astype(f32)

    w = w_in[l]
    w_main = jnp.concatenate([w[:, 3 * NA_W:AB_OFF], w[:, AB_OFF + N_AB:], w[:, :3 * NA_W]], axis=1).astype(bf16)
    w_ab, alog_row, dtb_row = _gate_lane_layout(w, gdn_a_log[l], gdn_dt_bias[l])

    p_real, gb_real = _in_proj(x2d, vec(ln0_g), vec(ln0_b), w_main, w_ab, alog_row, dtb_row, _row_tile(t, 1024),
                               IN_PROJ_COLS)
    p_meta, gb_meta = _in_proj(meta_tokens.astype(f32), vec(ln0_g), vec(ln0_b), w_main, w_ab, alog_row, dtb_row,
                               N_META, IN_PROJ_COLS)

    o_na = _na(p_real, p_meta, _na_bias_table(na_rpb[l]), batch, seq)
    o_gdn = _gdn(p_real, p_meta, gdn_conv_w[l].astype(f32), gb_real, gb_meta, vec(gdn_norm_w[l]), batch, seq)

    w_route = jnp.pad(jnp.concatenate([router_expert_w[l], router_group_w[l]], axis=1).astype(f32),
                      ((0, 0), (0, LANES - N_EXPERTS - N_GROUPS)))
    b_route = jnp.pad(jnp.concatenate([router_expert_b[l], router_group_b[l]]).astype(f32),
                      (0, LANES - N_EXPERTS - N_GROUPS)).reshape(1, LANES)
    h1, h1_packed, route = _merge(x2d, o_na, o_gdn, p_real, vec(ln0_g), vec(ln0_b), w_branch_na[l].astype(bf16),
                       w_branch_gdn[l].astype(bf16), w_out[l].astype(bf16), vec(ln1_g[l]), vec(ln1_b[l]),
                       w_route, b_route, _row_tile(t, 512))

    e_idx = route[:, :TOP_K].astype(jnp.int32)
    dest, block_e, block_valid = _dispatch_plan(e_idx, t)
    x_pad = _sc_dispatch(h1_packed, dest[0], dest[1], block_e.shape[0] * EXPERT_ROWS)
    y_pad = _experts(x_pad, block_e, block_valid, expert_w_gate[l], expert_w_up[l], expert_w_down[l])
    y_pairs = _sc_gather(y_pad, dest.reshape(-1)).reshape(TOP_K, t, HALF)
    out = _combine(h1, y_pairs, route, vec(ln2_g[l]), vec(ln2_b[l]), _row_tile(t, 512))
    return out.reshape(batch, seq, d)
```

```python
import functools

import numpy as np
import jax
import jax.numpy as jnp
from jax import lax
from jax.experimental import pallas as pl
from jax.experimental.pallas import tpu as pltpu
from jax.experimental.pallas import tpu_sc as plsc

D_MODEL = 1024
DEPTH = 1
GRID_W = 64
N_META = 16
NA_HEADS = 8
NA_HEAD_DIM = 64
NA_W = NA_HEADS * NA_HEAD_DIM
NA_WIN_ROWS = 8
NA_WIN_COLS = 16
GDN_HEADS = 8
GDN_HEAD_DIM = 128
GDN_W = GDN_HEADS * GDN_HEAD_DIM
GDN_CONV = 5
GDN_CHUNK = 64
N_GROUPS = 4
EXPERTS_PER_GROUP = 8
N_EXPERTS = N_GROUPS * EXPERTS_PER_GROUP
TOP_K = 2
D_EXPERT = 256
LN_EPS = 1e-5
NORM_EPS = 1e-6
DN_ALPHA = (2 * DEPTH) ** 0.25

LANES = 128
VMEM_LIMIT = 48 * 1024 * 1024
NEG = -1e30

CB_G_Q, CB_G_K, CB_G_V, CB_G_Z = 0, 8, 16, 24
CB_GATE_NA, CB_GATE_GDN = 32, 40
CB_NA_Q, CB_NA_K, CB_NA_V = 48, 52, 56
P_COLS = 60 * LANES
AB_OFF = 3 * NA_W + 4 * GDN_W
N_AB = 4 * GDN_HEADS

IN_PROJ_COLS = 2560
NA_ROWS_PER_ITER = 4
MERGE_ROWS = 1024
MERGE_SPLIT = 2
EXPERT_ROWS = 256
GATHER_WINDOW = 128

f32 = jnp.float32
bf16 = jnp.bfloat16


def _dot(a, b):
    return jnp.dot(a.astype(bf16), b.astype(bf16), preferred_element_type=f32)


def _dot_nt(a, b):
    return lax.dot_general(a.astype(bf16), b.astype(bf16), (((1,), (1,)), ((), ())),
                           preferred_element_type=f32)


def _dot_tn(a, b):
    return lax.dot_general(a.astype(bf16), b.astype(bf16), (((0,), (0,)), ((), ())),
                           preferred_element_type=f32)


def _split_bf16(x):
    hi = x.astype(bf16)
    lo = (x - hi.astype(f32)).astype(bf16)
    return hi, lo


def _dot3(x, w):
    xh, xl = _split_bf16(x)
    wh, wl = _split_bf16(w)
    d = lambda a, b: jnp.dot(a, b, preferred_element_type=f32)
    return d(xh, wh) + (d(xl, wh) + d(xh, wl))


def _layer_norm(x, g, b):
    xc = x - jnp.mean(x, -1, keepdims=True)
    var = jnp.mean(xc * xc, -1, keepdims=True)
    return xc * lax.rsqrt(var + LN_EPS) * g + b


def _sigmoid(x):
    return 0.5 * jnp.tanh(0.5 * x) + 0.5


def _silu(x):
    return x * _sigmoid(x)


HALF = D_MODEL // 2


def _pack_halves(x):
    def rounded(v):
        u = lax.bitcast_convert_type(v, jnp.uint32)
        return u + (jnp.uint32(0x7FFF) + ((u >> 16) & jnp.uint32(1)))

    word = (rounded(x[:, :HALF]) >> 16) | (rounded(x[:, HALF:]) & jnp.uint32(0xFFFF0000))
    return lax.bitcast_convert_type(word, f32)


def _unpack_halves(p):
    u = lax.bitcast_convert_type(p, jnp.uint32)
    return lax.bitcast_convert_type(u << 16, f32), lax.bitcast_convert_type(u & jnp.uint32(0xFFFF0000), f32)


def _params(*sem):
    return pltpu.CompilerParams(dimension_semantics=sem, vmem_limit_bytes=VMEM_LIMIT)


def _in_proj_kernel(x_ref, g_ref, b_ref, w_ref, wab_ref, alog_ref, dtb_ref, p_ref, gb_ref, hn_ref):
    @pl.when(pl.program_id(1) == 0)
    def _():
        h = _layer_norm(x_ref[...], g_ref[...], b_ref[...])
        hn_ref[...] = h.astype(bf16)
        ab = jnp.dot(hn_ref[...], wab_ref[...].astype(bf16), preferred_element_type=f32)
        x = ab + dtb_ref[...]
        softplus = jnp.maximum(x, 0.0) + jnp.log1p(jnp.exp(-jnp.abs(x)))
        lane = lax.broadcasted_iota(jnp.int32, ab.shape, 1)
        gb_ref[...] = jnp.where((lane & 3) < 2, -jnp.exp(alog_ref[...]) * softplus, _sigmoid(ab))

    p_ref[...] = jnp.dot(hn_ref[...], w_ref[...], preferred_element_type=f32).astype(bf16)


def _gate_lane_layout(w_in, a_log, dt_bias):
    w_ab = w_in[:, AB_OFF:AB_OFF + N_AB].astype(f32).reshape(D_MODEL, 4, GDN_HEADS).transpose(0, 2, 1)
    w_ab = jnp.pad(w_ab.reshape(D_MODEL, N_AB), ((0, 0), (0, LANES - N_AB)))
    row = lambda p: jnp.pad(jnp.pad(p.astype(f32).T, ((0, 0), (0, 2))).reshape(1, N_AB), ((0, 0), (0, LANES - N_AB)))
    return w_ab, row(a_log), row(dt_bias)


def _in_proj(x2d, ln_g, ln_b, w_main, w_ab, alog_row, dtb_row, tm, tn):
    t = x2d.shape[0]
    return pl.pallas_call(
        _in_proj_kernel,
        grid=(t // tm, P_COLS // tn),
        in_specs=[
            pl.BlockSpec((tm, D_MODEL), lambda i, j: (i, 0)),
            pl.BlockSpec((1, D_MODEL), lambda i, j: (0, 0)),
            pl.BlockSpec((1, D_MODEL), lambda i, j: (0, 0)),
            pl.BlockSpec((D_MODEL, tn), lambda i, j: (0, j)),
            pl.BlockSpec((D_MODEL, LANES), lambda i, j: (0, 0)),
            pl.BlockSpec((1, LANES), lambda i, j: (0, 0)),
            pl.BlockSpec((1, LANES), lambda i, j: (0, 0)),
        ],
        out_specs=[
            pl.BlockSpec((tm, tn), lambda i, j: (i, j)),
            pl.BlockSpec((tm, LANES), lambda i, j: (i, 0)),
        ],
        out_shape=[
            jax.ShapeDtypeStruct((t, P_COLS), bf16),
            jax.ShapeDtypeStruct((t, LANES), f32),
        ],
        scratch_shapes=[pltpu.VMEM((tm, D_MODEL), bf16)],
        compiler_params=_params("parallel", "arbitrary"),
        name="in_proj",
    )(x2d, ln_g, ln_b, w_main, w_ab, alog_row, dtb_row)


def _na_kernel(q_ref, k_ref, v_ref, km_ref, vm_ref, bias_ref, o_ref, *, rows):
    scale = NA_HEAD_DIM ** -0.5
    kr = NA_WIN_ROWS
    lane = lax.broadcasted_iota(jnp.int32, (GRID_W, LANES), 1)
    low = lane < NA_HEAD_DIM
    first = lax.broadcasted_iota(jnp.int32, (2 * GRID_W, LANES), 0) < GRID_W
    lane2 = lax.broadcasted_iota(jnp.int32, (2 * GRID_W, LANES), 1)
    pair_mask = jnp.where(first == (lane2 < NA_HEAD_DIM), scale, 0.0).astype(bf16)
    pairs = range(NA_HEADS // 2)
    sl = [slice(LANES * p, LANES * (p + 1)) for p in pairs]

    nr = NA_ROWS_PER_ITER
    assert rows % nr == 0

    def row_block(it, carry):
        units = [(i, p) for i in range(nr) for p in pairs]
        r = [it * nr + i for i in range(nr)]
        r0 = [jnp.clip(r[i] - kr // 2, 0, rows - kr) for i in range(nr)]
        d0 = [r0[i] - r[i] + (NA_WIN_ROWS - 1) for i in range(nr)]
        qoff = [pl.multiple_of(r[i] * GRID_W, GRID_W) for i in range(nr)]
        koff = [pl.multiple_of(r0[i] * GRID_W, GRID_W) for i in range(nr)]
        q2 = [jnp.concatenate([q_ref[pl.ds(qoff[i], GRID_W), sl[p]]] * 2, axis=0) * pair_mask for i, p in units]
        s = [_dot_nt(q, k_ref[pl.ds(koff[i], kr * GRID_W), sl[p]]) + bias_ref[p, d0[i]] for (i, p), q in zip(units, q2)]
        sm = [_dot_nt(q, km_ref[:, sl[p]]) for (i, p), q in zip(units, q2)]
        m = [jnp.maximum(jnp.max(a, -1, keepdims=True), jnp.max(b, -1, keepdims=True)) for a, b in zip(s, sm)]
        e = [jnp.exp(a - c) for a, c in zip(s, m)]
        em = [jnp.exp(b - c) for b, c in zip(sm, m)]
        den = [jnp.sum(a, -1, keepdims=True) + jnp.sum(b, -1, keepdims=True) for a, b in zip(e, em)]
        o = [_dot(a, v_ref[pl.ds(koff[i], kr * GRID_W), sl[p]]) + _dot(b, vm_ref[:, sl[p]])
             for (i, p), a, b in zip(units, e, em)]
        o = [a / c for a, c in zip(o, den)]
        for (i, p), a in zip(units, o):
            o_ref[pl.ds(qoff[i], GRID_W), sl[p]] = jnp.where(low, a[:GRID_W], a[GRID_W:]).astype(bf16)
        return carry

    lax.fori_loop(0, rows // nr, row_block, 0)


def _na_bias_table(rpb):
    col = np.arange(GRID_W)
    col_start = np.clip(col - NA_WIN_COLS // 2, 0, GRID_W - NA_WIN_COLS)
    col_mask = (col[None, :] >= col_start[:, None]) & (col[None, :] < col_start[:, None] + NA_WIN_COLS)
    dc_idx = np.clip(col[None, :] - col[:, None] + NA_WIN_COLS - 1, 0, 2 * NA_WIN_COLS - 2)
    pick = (dc_idx[:, :, None] == np.arange(2 * NA_WIN_COLS - 1)).astype(np.float32)
    by_col = jnp.einsum("hrc,qkc->hrqk", rpb.astype(f32), pick, precision=lax.Precision.HIGHEST)
    by_col = jnp.where(col_mask[None, None], by_col, NEG)
    tbl = jnp.stack([by_col[:, d0:d0 + NA_WIN_ROWS] for d0 in range(NA_WIN_ROWS)], axis=1)
    tbl = tbl.reshape(NA_HEADS // 2, 2, NA_WIN_ROWS, NA_WIN_ROWS, GRID_W, GRID_W)
    return tbl.transpose(0, 2, 1, 4, 3, 5).reshape(NA_HEADS // 2, NA_WIN_ROWS, 2 * GRID_W, NA_WIN_ROWS * GRID_W)


def _na(p_real, p_meta, bias_tbl, batch, seq):
    rows = seq // GRID_W
    assert rows >= NA_WIN_ROWS
    col = lambda cb: (lambda b: (b, cb // 4))
    colm = lambda cb: (lambda b: (0, cb // 4))
    return pl.pallas_call(
        functools.partial(_na_kernel, rows=rows),
        grid=(batch,),
        in_specs=[
            pl.BlockSpec((seq, NA_W), col(CB_NA_Q)),
            pl.BlockSpec((seq, NA_W), col(CB_NA_K)),
            pl.BlockSpec((seq, NA_W), col(CB_NA_V)),
            pl.BlockSpec((N_META, NA_W), colm(CB_NA_K)),
            pl.BlockSpec((N_META, NA_W), colm(CB_NA_V)),
            pl.BlockSpec(bias_tbl.shape, lambda b: (0, 0, 0, 0)),
        ],
        out_specs=pl.BlockSpec((seq, NA_W), lambda b: (b, 0)),
        out_shape=jax.ShapeDtypeStruct((batch * seq, NA_W), bf16),
        compiler_params=_params("parallel"),
        name="na",
    )(p_real, p_real, p_real, p_meta, p_meta, bias_tbl)


GDN_LEAD = GDN_CHUNK - N_META
RAW_PAD = 8


GDN_BLOCK = 128
GDN_HEADS_PER_STEP = 2
GDN_BLOCKS_PER_ITER = 4


INV_BASE = 8


def _tri_masks(c, lower):
    ii = lax.broadcasted_iota(jnp.int32, (c, c), 0)
    jj = lax.broadcasted_iota(jnp.int32, (c, c), 1)
    same = lambda s: jnp.right_shift(ii, s.bit_length() - 1) == jnp.right_shift(jj, s.bit_length() - 1)
    levels = [same(INV_BASE)]
    s = INV_BASE
    while s < c:
        levels.append(same(2 * s) & jnp.logical_not(same(s)))
        s *= 2
    if lower:
        return ii == jj, jj <= ii, jj < ii, ii <= jj, levels
    return ii == jj, jj >= ii, jj > ii, ii >= jj, levels


def _unit_tri_inverse(ms, eye, levels):
    pws = [jnp.where(levels[0], -m, 0.0) for m in ms]
    inv = [jnp.where(eye, 1.0, 0.0) + n for n in pws]
    for _ in range(INV_BASE.bit_length() - 2):
        pws = [_dot(p, p) for p in pws]
        inv = [a + _dot(a, p) for a, p in zip(inv, pws)]
    for pair in levels[1:]:
        ts = [_dot(a, jnp.where(pair, m, 0.0)) for a, m in zip(inv, ms)]
        inv = [a - _dot(t, a) for a, t in zip(inv, ts)]
    return inv


def _gdn_prepare(chains):
    pre = []
    for qc, kc, vc, kk, qk, g_col, beta_col, masks in chains:
        eye, incl, strict, incl_t, levels = masks
        g_row = jnp.sum(jnp.where(eye, g_col, 0.0), axis=0, keepdims=True)
        gc_col = jnp.sum(jnp.where(incl, g_row, 0.0), axis=1, keepdims=True)
        gc_row = jnp.sum(jnp.where(incl_t, g_col, 0.0), axis=0, keepdims=True)
        total = jnp.sum(g_col, axis=0, keepdims=True)
        decay = jnp.where(incl, jnp.exp(jnp.minimum(gc_col - gc_row, 0.0)), 0.0)
        pre.append((gc_col, beta_col, total, decay, jnp.where(strict, kk * beta_col * decay, 0.0)))
    eye, levels = chains[0][7][0], chains[0][7][4]
    inv = _unit_tri_inverse([p[4] for p in pre], eye, levels)
    sols = [_dot(a, jnp.concatenate([vc * beta, kc * (beta * jnp.exp(gc))], axis=1))
            for (qc, kc, vc, *_), (gc, beta, *_), a in zip(chains, pre, inv)]
    return [(sol[:, :GDN_HEAD_DIM], sol[:, GDN_HEAD_DIM:], qc * jnp.exp(gc), qk * decay, kc * jnp.exp(total - gc),
             jnp.exp(total))
            for (qc, kc, vc, kk, qk, *_), (gc, beta, total, decay, _), sol in zip(chains, pre, sols)]


def _gdn_kernel(q_ref, k_ref, v_ref, z_ref, qm_ref, km_ref, vm_ref,
                wq_ref, wk_ref, wv_ref, gt_ref, gtm_ref, nw_ref, o_ref,
                raw, u_s, wqd_s, aqk_s, kd_s, gl_s, st_s, *, seq, hg):
    c = GDN_BLOCK
    cm = GDN_CHUNK
    nblk = seq // c
    h0 = pl.program_id(1) * hg
    in_refs = (q_ref, k_ref, v_ref)
    meta_refs = (qm_ref, km_ref, vm_ref)
    w_refs = (wq_ref, wk_ref, wv_ref)
    cols = lambda hh: slice(LANES * hh, LANES * (hh + 1))
    halo = GDN_CONV // 2
    lead = RAW_PAD + GDN_LEAD

    for hh in range(hg):
        for t in range(3):
            raw[hh, t, pl.ds(0, lead), :] = jnp.zeros((lead, LANES), f32)
            raw[hh, t, pl.ds(lead, N_META), :] = meta_refs[t][:, cols(hh)].astype(f32)
            raw[hh, t, pl.ds(lead + N_META + seq, RAW_PAD), :] = jnp.zeros((RAW_PAD, LANES), f32)

    def copy_in(j, carry):
        src = pl.multiple_of(j * c, c)
        dst = pl.multiple_of(lead + N_META + j * c, 8)
        for hh in range(hg):
            for t in range(3):
                raw[hh, t, pl.ds(dst, c), :] = in_refs[t][pl.ds(src, c), cols(hh)].astype(f32)
        return carry

    lax.fori_loop(0, nblk, copy_in, 0)

    def conv_block(hh, p0, rows):
        out = []
        for t, kind in enumerate("qkv"):
            w = w_refs[t][:, cols(hh)]
            acc = None
            for tap in range(GDN_CONV):
                off = RAW_PAD - halo + tap
                term = raw[hh, t, pl.ds(p0 + off, rows), :] * w[tap:tap + 1, :]
                acc = term if acc is None else acc + term
            y = _silu(acc)
            if kind != "v":
                c = float(GDN_HEAD_DIM) if kind == "q" else 1.0
                ss = jnp.dot((y * y).astype(bf16), jnp.full((LANES, LANES), c, bf16), preferred_element_type=f32)
                y = y * lax.rsqrt(ss + c * NORM_EPS)
            out.append(y)
        return out

    def gates(gb, hh):
        return pltpu.roll(gb, jnp.bitwise_and(LANES - 4 * (h0 + hh), LANES - 1), 1)

    masks_meta = _tri_masks(cm, True)
    live_col = lax.broadcasted_iota(jnp.int32, (cm, 1), 0) >= GDN_LEAD
    heads = range(hg)
    metas = [[jnp.where(live_col, y, 0.0) for y in conv_block(hh, 0, cm)] for hh in heads]
    qkk = [_dot_nt(jnp.concatenate([q0, k0], axis=0), k0) for q0, k0, _ in metas]
    chain_in = []
    gb_meta = jnp.concatenate([jnp.zeros((GDN_LEAD, LANES), f32), gtm_ref[...]], axis=0)
    for hh in heads:
        q0, k0, v0 = metas[hh]
        gm = gates(gb_meta, hh)
        for d in range(2):
            chain_in.append((q0, k0, v0, qkk[hh][cm:], qkk[hh][:cm], gm[:, d:d + 1], gm[:, 2 + d:3 + d], masks_meta))
    s0 = [_dot_tn(r[4], r[0]) for r in _gdn_prepare(chain_in)]
    for i, (hh, d) in enumerate((hh, d) for hh in heads for d in range(2)):
        st_s[hh, d] = s0[i]

    masks = (_tri_masks(c, True), _tri_masks(c, False))
    nb = GDN_BLOCKS_PER_ITER
    assert nblk % nb == 0

    def prepare_blocks(jj, carry):
        units = [(hh, b) for hh in heads for b in range(nb)]
        rows = [pl.multiple_of((jj * nb + b) * c, c) for b in range(nb)]
        convs = [conv_block(hh, pl.multiple_of(cm + rows[j], cm), c) for hh, j in units]
        qkk = [_dot_nt(jnp.concatenate([qc, kc], axis=0), kc) for qc, kc, _ in convs]
        chain_in, chain_id = [], []
        for (hh, j), (qc, kc, vc), qk_kk in zip(units, convs, qkk):
            gt = gates(gt_ref[pl.ds(rows[j], c), :], hh)
            for d in range(2):
                chain_in.append((qc, kc, vc, qk_kk[c:], qk_kk[:c], gt[:, d:d + 1], gt[:, 2 + d:3 + d], masks[d]))
                chain_id.append((hh, d, j))
        for (hh, d, j), (u, w, qd, aqk, kd, g_last) in zip(chain_id, _gdn_prepare(chain_in)):
            r0 = rows[j]
            u_s[hh, d, pl.ds(r0, c), :] = u
            wqd_s[hh, d, pl.ds(pl.multiple_of(2 * r0, 2 * c), c), :] = w.astype(bf16)
            wqd_s[hh, d, pl.ds(pl.multiple_of(2 * r0 + c, c), c), :] = qd.astype(bf16)
            aqk_s[hh, d, pl.ds(r0, c), :] = aqk.astype(bf16)
            kd_s[hh, d, pl.ds(r0, c), :] = kd.astype(bf16)
            gl_s[hh, d, pl.ds(pl.multiple_of((jj * nb + j) * 8, 8), 8), :] = jnp.broadcast_to(g_last, (8, LANES))
        return carry

    lax.fori_loop(0, nblk // nb, prepare_blocks, 0)

    def scan_step(j, carry):
        rows = (pl.multiple_of(j * c, c), pl.multiple_of((nblk - 1 - j) * c, c))
        grow = (pl.multiple_of(j * 8, 8), pl.multiple_of((nblk - 1 - j) * 8, 8))
        chains = [(hh, d) for hh in heads for d in range(2)]
        states = [st_s[hh, d] for hh, d in chains]
        ws = [jnp.dot(wqd_s[hh, d, pl.ds(pl.multiple_of(2 * rows[d], 2 * c), 2 * c), :], s.astype(bf16),
                      preferred_element_type=f32) for (hh, d), s in zip(chains, states)]
        v_new = [(u_s[hh, d, pl.ds(rows[d], c), :] - w[:c]).astype(bf16) for (hh, d), w in zip(chains, ws)]
        outs = [w[c:] + jnp.dot(aqk_s[hh, d, pl.ds(rows[d], c), :], v, preferred_element_type=f32)
                for (hh, d), w, v in zip(chains, ws, v_new)]
        new_states = [s * gl_s[hh, d, pl.ds(grow[d], 1), :] + _dot_tn(kd_s[hh, d, pl.ds(rows[d], c), :], v)
                      for (hh, d), s, v in zip(chains, states, v_new)]
        for (hh, d), s in zip(chains, new_states):
            st_s[hh, d] = s
        for (hh, d), o in zip(chains, outs):
            u_s[hh, d, pl.ds(rows[d], c), :] = o
        return carry

    lax.fori_loop(0, nblk, scan_step, 0)

    def finish(j, carry):
        r0 = pl.multiple_of(j * c, c)
        for hh in range(hg):
            o = u_s[hh, 0, pl.ds(r0, c), :] + u_s[hh, 1, pl.ds(r0, c), :]
            o = o * lax.rsqrt(jnp.mean(o * o, -1, keepdims=True) + NORM_EPS) * nw_ref[...]
            o_ref[pl.ds(r0, c), cols(hh)] = (o * _silu(z_ref[pl.ds(r0, c), cols(hh)].astype(f32))).astype(bf16)
        return carry

    lax.fori_loop(0, nblk, finish, 0)


def _gdn(p_real, p_meta, conv_w, gates_real, gates_meta, norm_w, batch, seq):
    hg = GDN_HEADS_PER_STEP
    assert seq % GDN_BLOCK == 0 and GDN_HEADS % hg == 0
    nblk = seq // GDN_BLOCK
    wide = hg * LANES
    col = lambda cb: (lambda b, h: (b, cb // hg + h))
    colm = lambda cb: (lambda b, h: (0, cb // hg + h))
    colw = lambda k: (lambda b, h: (0, k * (GDN_HEADS // hg) + h))
    return pl.pallas_call(
        functools.partial(_gdn_kernel, seq=seq, hg=hg),
        grid=(batch, GDN_HEADS // hg),
        in_specs=[
            pl.BlockSpec((seq, wide), col(CB_G_Q)),
            pl.BlockSpec((seq, wide), col(CB_G_K)),
            pl.BlockSpec((seq, wide), col(CB_G_V)),
            pl.BlockSpec((seq, wide), col(CB_G_Z)),
            pl.BlockSpec((N_META, wide), colm(CB_G_Q)),
            pl.BlockSpec((N_META, wide), colm(CB_G_K)),
            pl.BlockSpec((N_META, wide), colm(CB_G_V)),
            pl.BlockSpec((GDN_CONV, wide), colw(0)),
            pl.BlockSpec((GDN_CONV, wide), colw(1)),
            pl.BlockSpec((GDN_CONV, wide), colw(2)),
            pl.BlockSpec((seq, LANES), lambda b, h: (b, 0)),
            pl.BlockSpec((N_META, LANES), lambda b, h: (0, 0)),
            pl.BlockSpec((1, LANES), lambda b, h: (0, 0)),
        ],
        out_specs=pl.BlockSpec((seq, wide), lambda b, h: (b, h)),
        out_shape=jax.ShapeDtypeStruct((batch * seq, GDN_W), bf16),
        scratch_shapes=[
            pltpu.VMEM((hg, 3, seq + GDN_CHUNK + 2 * RAW_PAD, LANES), f32),
            pltpu.VMEM((hg, 2, seq, LANES), f32),
            pltpu.VMEM((hg, 2, 2 * seq, LANES), bf16),
            pltpu.VMEM((hg, 2, seq, LANES), bf16),
            pltpu.VMEM((hg, 2, seq, LANES), bf16),
            pltpu.VMEM((hg, 2, 8 * nblk, LANES), f32),
            pltpu.VMEM((hg, 2, GDN_HEAD_DIM, GDN_HEAD_DIM), f32),
        ],
        compiler_params=_params("parallel", "parallel"),
        name="gdn",
    )(p_real, p_real, p_real, p_real, p_meta, p_meta, p_meta, conv_w, conv_w, conv_w, gates_real, gates_meta, norm_w)


def _route(logits):
    lane = lax.broadcasted_iota(jnp.int32, logits.shape, 1)
    big = jnp.int32(1 << 20)
    is_g = (lane >= N_EXPERTS) & (lane < N_EXPERTS + N_GROUPS)
    gl = jnp.where(is_g, logits, -jnp.inf)
    gmax = jnp.max(gl, -1, keepdims=True)
    g_idx = jnp.min(jnp.where(gl == gmax, lane - N_EXPERTS, big), -1, keepdims=True)
    g_w = 1.0 / jnp.sum(jnp.where(is_g, jnp.exp(gl - gmax), 0.0), -1, keepdims=True)
    in_grp = (lane < N_EXPERTS) & (jnp.right_shift(lane, 3) == g_idx)
    el = jnp.where(in_grp, logits, -jnp.inf)
    m1 = jnp.max(el, -1, keepdims=True)
    i1 = jnp.min(jnp.where(el == m1, lane, big), -1, keepdims=True)
    el2 = jnp.where(lane == i1, -jnp.inf, el)
    m2 = jnp.max(el2, -1, keepdims=True)
    i2 = jnp.min(jnp.where(el2 == m2, lane, big), -1, keepdims=True)
    e2 = jnp.exp(m2 - m1)
    w1 = g_w / (1.0 + e2)
    w2 = g_w * e2 / (1.0 + e2)
    out = jnp.where(lane == 0, i1.astype(f32), 0.0)
    out = jnp.where(lane == 1, i2.astype(f32), out)
    out = jnp.where(lane == 2, w1, out)
    return jnp.where(lane == 3, w2, out)


def _merge_kernel(x_ref, ona_ref, ogdn_ref, gna_ref, ggdn_ref, ln0g_ref, ln0b_ref, wna_ref, wgdn_ref,
                  wout_ref, ln1g_ref, ln1b_ref, wr_ref, br_ref, h1_ref, h1p_ref, route_ref):
    tm = x_ref.shape[0]
    rows = [pl.ds(i * (tm // MERGE_SPLIT), tm // MERGE_SPLIT) for i in range(MERGE_SPLIT)]
    y_na = [jnp.dot(ona_ref[r, :], wna_ref[...], preferred_element_type=f32) for r in rows]
    y_gdn = [jnp.dot(ogdn_ref[r, :], wgdn_ref[...], preferred_element_type=f32) for r in rows]
    merged = [_sigmoid(gna_ref[r, :].astype(f32)) * a + _sigmoid(ggdn_ref[r, :].astype(f32)) * b
              for r, a, b in zip(rows, y_na, y_gdn)]
    mix = [jnp.dot(m.astype(bf16), wout_ref[...], preferred_element_type=f32) for m in merged]
    h = [_layer_norm(x_ref[r, :], ln0g_ref[...], ln0b_ref[...]) for r in rows]
    h1 = [_layer_norm(DN_ALPHA * a + b, ln1g_ref[...], ln1b_ref[...]) for a, b in zip(h, mix)]
    logits = [_dot3(a, wr_ref[...]) + br_ref[...] for a in h1]
    for r, a, lg in zip(rows, h1, logits):
        h1_ref[r, :] = a
        h1p_ref[r, :] = _pack_halves(a)
        route_ref[r, :] = _route(lg)


def _merge(x2d, o_na, o_gdn, p_real, ln0_g, ln0_b, w_na, w_gdn, w_out, ln1_g, ln1_b, w_route, b_route, tm):
    t = x2d.shape[0]
    row = lambda i: (i, 0)
    const = lambda i: (0, 0)
    return pl.pallas_call(
        _merge_kernel,
        grid=(t // tm,),
        in_specs=[
            pl.BlockSpec((tm, D_MODEL), row),
            pl.BlockSpec((tm, NA_W), row),
            pl.BlockSpec((tm, GDN_W), row),
            pl.BlockSpec((tm, D_MODEL), lambda i: (i, CB_GATE_NA // 8)),
            pl.BlockSpec((tm, D_MODEL), lambda i: (i, CB_GATE_GDN // 8)),
            pl.BlockSpec((1, D_MODEL), const),
            pl.BlockSpec((1, D_MODEL), const),
            pl.BlockSpec((NA_W, D_MODEL), const),
            pl.BlockSpec((GDN_W, D_MODEL), const),
            pl.BlockSpec((D_MODEL, D_MODEL), const),
            pl.BlockSpec((1, D_MODEL), const),
            pl.BlockSpec((1, D_MODEL), const),
            pl.BlockSpec((D_MODEL, LANES), const),
            pl.BlockSpec((1, LANES), const),
        ],
        out_specs=[pl.BlockSpec((tm, D_MODEL), row), pl.BlockSpec((tm, HALF), row), pl.BlockSpec((tm, LANES), row)],
        out_shape=[jax.ShapeDtypeStruct((t, D_MODEL), f32), jax.ShapeDtypeStruct((t, HALF), f32),
                   jax.ShapeDtypeStruct((t, LANES), f32)],
        compiler_params=_params("parallel"),
        name="merge",
    )(x2d, o_na, o_gdn, p_real, p_real, ln0_g, ln0_b, w_na, w_gdn, w_out, ln1_g, ln1_b, w_route, b_route)


def _sc_dispatch(rows, dest0, dest1, n_slots):
    t, width = rows.shape
    mesh = plsc.VectorSubcoreMesh(core_axis_name="core", subcore_axis_name="subcore")
    n_workers = mesh.num_cores * mesh.num_subcores
    per_worker = t // n_workers
    assert t % n_workers == 0 and per_worker % GATHER_WINDOW == 0

    @functools.partial(
        pl.kernel, out_type=jax.ShapeDtypeStruct((n_slots, width), rows.dtype), mesh=mesh,
        scratch_types=[pltpu.VMEM((GATHER_WINDOW,), jnp.int32), pltpu.VMEM((GATHER_WINDOW,), jnp.int32),
                       pltpu.VMEM((GATHER_WINDOW, width), rows.dtype)],
        name="row_dispatch")
    def dispatch(rows_hbm, d0_hbm, d1_hbm, out_hbm, d0_vmem, d1_vmem, rows_vmem):
        worker = lax.axis_index("subcore") * mesh.num_cores + lax.axis_index("core")

        @pl.loop(0, per_worker // GATHER_WINDOW)
        def _(step):
            base = pl.multiple_of(worker * per_worker + step * GATHER_WINDOW, GATHER_WINDOW)
            pltpu.sync_copy(d0_hbm.at[pl.ds(base, GATHER_WINDOW)], d0_vmem)
            pltpu.sync_copy(d1_hbm.at[pl.ds(base, GATHER_WINDOW)], d1_vmem)
            pltpu.sync_copy(rows_hbm.at[pl.ds(base, GATHER_WINDOW)], rows_vmem)
            pltpu.sync_copy(rows_vmem, out_hbm.at[d0_vmem])
            pltpu.sync_copy(rows_vmem, out_hbm.at[d1_vmem])

    return dispatch(rows, dest0, dest1)


def _sc_gather(table, idx):
    n = idx.shape[0]
    width = table.shape[1]
    mesh = plsc.VectorSubcoreMesh(core_axis_name="core", subcore_axis_name="subcore")
    n_workers = mesh.num_cores * mesh.num_subcores
    per_worker = n // n_workers
    assert n % n_workers == 0 and per_worker % GATHER_WINDOW == 0

    @functools.partial(
        pl.kernel, out_type=jax.ShapeDtypeStruct((n, width), table.dtype), mesh=mesh,
        scratch_types=[pltpu.VMEM((GATHER_WINDOW,), jnp.int32), pltpu.VMEM((GATHER_WINDOW, width), table.dtype)],
        name="row_gather")
    def gather(tbl_hbm, idx_hbm, out_hbm, idx_vmem, rows_vmem):
        worker = lax.axis_index("subcore") * mesh.num_cores + lax.axis_index("core")

        @pl.loop(0, per_worker // GATHER_WINDOW)
        def _(step):
            base = pl.multiple_of(worker * per_worker + step * GATHER_WINDOW, GATHER_WINDOW)
            pltpu.sync_copy(idx_hbm.at[pl.ds(base, GATHER_WINDOW)], idx_vmem)
            pltpu.sync_copy(tbl_hbm.at[idx_vmem], rows_vmem)
            pltpu.sync_copy(rows_vmem, out_hbm.at[pl.ds(base, GATHER_WINDOW)])

    return gather(table, idx)


def _expert_kernel(be_ref, valid_ref, x_ref, wg_ref, wu_ref, wd_ref, y_ref, wgu_s, wd_s):
    i = pl.program_id(0)
    valid = valid_ref[i]

    @pl.when(valid == 0)
    def _():
        y_ref[...] = jnp.zeros_like(y_ref)

    @pl.when(valid > 0)
    def _():
        prev = jnp.maximum(i, 1) - 1
        fresh = jnp.logical_or(i == 0, jnp.logical_or(be_ref[i] != be_ref[prev], valid_ref[prev] == 0))

        @pl.when(fresh)
        def _():
            wgu_s[:, :D_EXPERT] = wg_ref[...].astype(bf16)
            wgu_s[:, D_EXPERT:] = wu_ref[...].astype(bf16)
            wd_s[...] = wd_ref[...].astype(bf16)

        row = lax.broadcasted_iota(jnp.int32, x_ref.shape, 0)
        xa, xb = _unpack_halves(jnp.where(row < valid, x_ref[...], 0.0))
        gu = (jnp.dot(xa.astype(bf16), wgu_s[:HALF, :], preferred_element_type=f32)
              + jnp.dot(xb.astype(bf16), wgu_s[HALF:, :], preferred_element_type=f32))
        hdn = (_silu(gu[:, :D_EXPERT]) * gu[:, D_EXPERT:]).astype(bf16)
        y_ref[...] = _pack_halves(jnp.dot(hdn, wd_s[...], preferred_element_type=f32))


def _experts(x_pad, block_e, block_valid, w_gate, w_up, w_down):
    nb = x_pad.shape[0] // EXPERT_ROWS
    grid_spec = pltpu.PrefetchScalarGridSpec(
        num_scalar_prefetch=2,
        grid=(nb,),
        in_specs=[
            pl.BlockSpec((EXPERT_ROWS, HALF), lambda i, be, nv: (i, 0)),
            pl.BlockSpec((None, D_MODEL, D_EXPERT), lambda i, be, nv: (be[i], 0, 0)),
            pl.BlockSpec((None, D_MODEL, D_EXPERT), lambda i, be, nv: (be[i], 0, 0)),
            pl.BlockSpec((None, D_EXPERT, D_MODEL), lambda i, be, nv: (be[i], 0, 0)),
        ],
        out_specs=pl.BlockSpec((EXPERT_ROWS, HALF), lambda i, be, nv: (i, 0)),
        scratch_shapes=[pltpu.VMEM((D_MODEL, 2 * D_EXPERT), bf16), pltpu.VMEM((D_EXPERT, D_MODEL), bf16)],
    )
    return pl.pallas_call(
        _expert_kernel,
        grid_spec=grid_spec,
        out_shape=jax.ShapeDtypeStruct(x_pad.shape, f32),
        compiler_params=_params("arbitrary"),
        name="experts",
    )(block_e, block_valid, x_pad, w_gate, w_up, w_down)


def _combine_kernel(h1_ref, y1_ref, y2_ref, route_ref, g_ref, b_ref, o_ref):
    r = route_ref[...]
    y1a, y1b = _unpack_halves(y1_ref[...])
    y2a, y2b = _unpack_halves(y2_ref[...])
    w1, w2 = r[:, 2:3], r[:, 3:4]
    ffn = jnp.concatenate([y1a * w1 + y2a * w2, y1b * w1 + y2b * w2], axis=1)
    o_ref[...] = _layer_norm(DN_ALPHA * h1_ref[...] + ffn, g_ref[...], b_ref[...])


def _combine(h1, y_pairs, route, ln_g, ln_b, tm):
    t = h1.shape[0]
    row = lambda i: (i, 0)
    const = lambda i: (0, 0)
    return pl.pallas_call(
        _combine_kernel,
        grid=(t // tm,),
        in_specs=[
            pl.BlockSpec((tm, D_MODEL), row),
            pl.BlockSpec((None, tm, HALF), lambda i: (0, i, 0)),
            pl.BlockSpec((None, tm, HALF), lambda i: (1, i, 0)),
            pl.BlockSpec((tm, LANES), row),
            pl.BlockSpec((1, D_MODEL), const),
            pl.BlockSpec((1, D_MODEL), const),
        ],
        out_specs=pl.BlockSpec((tm, D_MODEL), row),
        out_shape=jax.ShapeDtypeStruct((t, D_MODEL), f32),
        compiler_params=_params("parallel"),
        name="combine",
    )(h1, y_pairs, y_pairs, route, ln_g, ln_b)


def _dispatch_plan(e_idx, t):
    a = t * TOP_K
    nb = a // EXPERT_ROWS + N_EXPERTS
    flat_e = e_idx.reshape(-1)
    onehot = (flat_e[:, None] == jnp.arange(N_EXPERTS, dtype=jnp.int32)[None, :]).astype(jnp.int32)
    csum = jnp.cumsum(onehot, axis=0)
    rank = jnp.sum(csum * onehot, axis=1) - 1
    counts = csum[-1]
    padded = (counts + EXPERT_ROWS - 1) // EXPERT_ROWS * EXPERT_ROWS
    pad_end = jnp.cumsum(padded)
    pad_start = pad_end - padded
    dest = pad_start[flat_e] + rank
    block_start = jnp.arange(nb, dtype=jnp.int32) * EXPERT_ROWS
    block_e = jnp.sum((pad_end[None, :] <= block_start[:, None]).astype(jnp.int32), axis=1)
    block_e = jnp.minimum(block_e, N_EXPERTS - 1)
    block_valid = jnp.clip(counts[block_e] - (block_start - pad_start[block_e]), 0, EXPERT_ROWS)
    return dest.astype(jnp.int32).reshape(t, TOP_K).T, block_e.astype(jnp.int32), block_valid.astype(jnp.int32)


def _row_tile(t, want):
    tm = min(t, want)
    assert t % tm == 0
    return tm


def kernel(x, meta_tokens, ln0_g, ln0_b, w_in, na_rpb, gdn_conv_w, gdn_a_log, gdn_dt_bias, gdn_norm_w,
           w_branch_na, w_branch_gdn, w_out, ln1_g, ln1_b, router_group_w, router_group_b, router_expert_w,
           router_expert_b, expert_w_gate, expert_w_up, expert_w_down, ln2_g, ln2_b):
    batch, seq, d = x.shape
    assert d == D_MODEL and seq % GRID_W == 0 and DEPTH == 1
    t = batch * seq
    l = 0
    x2d = x.reshape(t, d)
    vec = lambda v: v.reshape(1, -1).astype(f32)

    w = w_in[l]
    w_main = jnp.concatenate([w[:, 3 * NA_W:AB_OFF], w[:, AB_OFF + N_AB:], w[:, :3 * NA_W]], axis=1).astype(bf16)
    w_ab, alog_row, dtb_row = _gate_lane_layout(w, gdn_a_log[l], gdn_dt_bias[l])

    p_real, gb_real = _in_proj(x2d, vec(ln0_g), vec(ln0_b), w_main, w_ab, alog_row, dtb_row, _row_tile(t, 1024),
                               IN_PROJ_COLS)
    p_meta, gb_meta = _in_proj(meta_tokens.astype(f32), vec(ln0_g), vec(ln0_b), w_main, w_ab, alog_row, dtb_row,
                               N_META, IN_PROJ_COLS)

    o_na = _na(p_real, p_meta, _na_bias_table(na_rpb[l]), batch, seq)
    o_gdn = _gdn(p_real, p_meta, gdn_conv_w[l].astype(f32), gb_real, gb_meta, vec(gdn_norm_w[l]), batch, seq)

    w_route = jnp.pad(jnp.concatenate([router_expert_w[l], router_group_w[l]], axis=1).astype(f32),
                      ((0, 0), (0, LANES - N_EXPERTS - N_GROUPS)))
    b_route = jnp.pad(jnp.concatenate([router_expert_b[l], router_group_b[l]]).astype(f32),
                      (0, LANES - N_EXPERTS - N_GROUPS)).reshape(1, LANES)
    h1, h1_packed, route = _merge(x2d, o_na, o_gdn, p_real, vec(ln0_g), vec(ln0_b), w_branch_na[l].astype(bf16),
                       w_branch_gdn[l].astype(bf16), w_out[l].astype(bf16), vec(ln1_g[l]), vec(ln1_b[l]),
                       w_route, b_route, _row_tile(t, MERGE_ROWS))

    e_idx = route[:, :TOP_K].astype(jnp.int32)
    dest, block_e, block_valid = _dispatch_plan(e_idx, t)
    x_pad = _sc_dispatch(h1_packed, dest[0], dest[1], block_e.shape[0] * EXPERT_ROWS)
    y_pad = _experts(x_pad, block_e, block_valid, expert_w_gate[l], expert_w_up[l], expert_w_down[l])
    y_pairs = _sc_gather(y_pad, dest.reshape(-1)).reshape(TOP_K, t, HALF)
    out = _combine(h1, y_pairs, route, vec(ln2_g[l]), vec(ln2_b[l]), _row_tile(t, 512))
    return out.reshape(batch, seq, d)
```

```python
import functools

import numpy as np
import jax
import jax.numpy as jnp
from jax import lax
from jax.experimental import pallas as pl
from jax.experimental.pallas import tpu as pltpu
from jax.experimental.pallas import tpu_sc as plsc

D_MODEL = 1024
DEPTH = 1
GRID_W = 64
N_META = 16
NA_HEADS = 8
NA_HEAD_DIM = 64
NA_W = NA_HEADS * NA_HEAD_DIM
NA_WIN_ROWS = 8
NA_WIN_COLS = 16
GDN_HEADS = 8
GDN_HEAD_DIM = 128
GDN_W = GDN_HEADS * GDN_HEAD_DIM
GDN_CONV = 5
GDN_CHUNK = 64
N_GROUPS = 4
EXPERTS_PER_GROUP = 8
N_EXPERTS = N_GROUPS * EXPERTS_PER_GROUP
TOP_K = 2
D_EXPERT = 256
LN_EPS = 1e-5
NORM_EPS = 1e-6
DN_ALPHA = (2 * DEPTH) ** 0.25

LANES = 128
VMEM_LIMIT = 48 * 1024 * 1024
NEG = -1e30

CB_G_Q, CB_G_K, CB_G_V, CB_G_Z = 0, 8, 16, 24
CB_GATE_NA, CB_GATE_GDN = 32, 40
CB_NA_Q, CB_NA_K, CB_NA_V = 48, 52, 56
P_COLS = 60 * LANES
AB_OFF = 3 * NA_W + 4 * GDN_W
N_AB = 4 * GDN_HEADS

IN_PROJ_COLS = 2560
NA_ROWS_PER_ITER = 4
MERGE_ROWS = 1024
MERGE_SPLIT = 2
EXPERT_ROWS = 512
EXPERT_SPLIT = 2
GATHER_WINDOW = 128

f32 = jnp.float32
bf16 = jnp.bfloat16


def _dot(a, b):
    return jnp.dot(a.astype(bf16), b.astype(bf16), preferred_element_type=f32)


def _dot_nt(a, b):
    return lax.dot_general(a.astype(bf16), b.astype(bf16), (((1,), (1,)), ((), ())),
                           preferred_element_type=f32)


def _dot_tn(a, b):
    return lax.dot_general(a.astype(bf16), b.astype(bf16), (((0,), (0,)), ((), ())),
                           preferred_element_type=f32)


def _split_bf16(x):
    hi = x.astype(bf16)
    lo = (x - hi.astype(f32)).astype(bf16)
    return hi, lo


def _dot3(x, w):
    xh, xl = _split_bf16(x)
    wh, wl = _split_bf16(w)
    d = lambda a, b: jnp.dot(a, b, preferred_element_type=f32)
    return d(xh, wh) + (d(xl, wh) + d(xh, wl))


def _layer_norm(x, g, b):
    xc = x - jnp.mean(x, -1, keepdims=True)
    var = jnp.mean(xc * xc, -1, keepdims=True)
    return xc * lax.rsqrt(var + LN_EPS) * g + b


def _sigmoid(x):
    return 0.5 * jnp.tanh(0.5 * x) + 0.5


def _silu(x):
    return x * _sigmoid(x)


HALF = D_MODEL // 2


def _pack_halves(x):
    def rounded(v):
        u = lax.bitcast_convert_type(v, jnp.uint32)
        return u + (jnp.uint32(0x7FFF) + ((u >> 16) & jnp.uint32(1)))

    word = (rounded(x[:, :HALF]) >> 16) | (rounded(x[:, HALF:]) & jnp.uint32(0xFFFF0000))
    return lax.bitcast_convert_type(word, f32)


def _unpack_halves(p):
    u = lax.bitcast_convert_type(p, jnp.uint32)
    return lax.bitcast_convert_type(u << 16, f32), lax.bitcast_convert_type(u & jnp.uint32(0xFFFF0000), f32)


def _params(*sem):
    return pltpu.CompilerParams(dimension_semantics=sem, vmem_limit_bytes=VMEM_LIMIT)


def _in_proj_kernel(x_ref, g_ref, b_ref, w_ref, wab_ref, alog_ref, dtb_ref, p_ref, gb_ref, hn_ref):
    @pl.when(pl.program_id(1) == 0)
    def _():
        h = _layer_norm(x_ref[...], g_ref[...], b_ref[...])
        hn_ref[...] = h.astype(bf16)
        ab = jnp.dot(hn_ref[...], wab_ref[...].astype(bf16), preferred_element_type=f32)
        x = ab + dtb_ref[...]
        softplus = jnp.maximum(x, 0.0) + jnp.log1p(jnp.exp(-jnp.abs(x)))
        lane = lax.broadcasted_iota(jnp.int32, ab.shape, 1)
        gb_ref[...] = jnp.where((lane & 3) < 2, -jnp.exp(alog_ref[...]) * softplus, _sigmoid(ab))

    p_ref[...] = jnp.dot(hn_ref[...], w_ref[...], preferred_element_type=f32).astype(bf16)


def _gate_lane_layout(w_in, a_log, dt_bias):
    w_ab = w_in[:, AB_OFF:AB_OFF + N_AB].astype(f32).reshape(D_MODEL, 4, GDN_HEADS).transpose(0, 2, 1)
    w_ab = jnp.pad(w_ab.reshape(D_MODEL, N_AB), ((0, 0), (0, LANES - N_AB)))
    row = lambda p: jnp.pad(jnp.pad(p.astype(f32).T, ((0, 0), (0, 2))).reshape(1, N_AB), ((0, 0), (0, LANES - N_AB)))
    return w_ab, row(a_log), row(dt_bias)


def _in_proj(x2d, ln_g, ln_b, w_main, w_ab, alog_row, dtb_row, tm, tn):
    t = x2d.shape[0]
    return pl.pallas_call(
        _in_proj_kernel,
        grid=(t // tm, P_COLS // tn),
        in_specs=[
            pl.BlockSpec((tm, D_MODEL), lambda i, j: (i, 0)),
            pl.BlockSpec((1, D_MODEL), lambda i, j: (0, 0)),
            pl.BlockSpec((1, D_MODEL), lambda i, j: (0, 0)),
            pl.BlockSpec((D_MODEL, tn), lambda i, j: (0, j)),
            pl.BlockSpec((D_MODEL, LANES), lambda i, j: (0, 0)),
            pl.BlockSpec((1, LANES), lambda i, j: (0, 0)),
            pl.BlockSpec((1, LANES), lambda i, j: (0, 0)),
        ],
        out_specs=[
            pl.BlockSpec((tm, tn), lambda i, j: (i, j)),
            pl.BlockSpec((tm, LANES), lambda i, j: (i, 0)),
        ],
        out_shape=[
            jax.ShapeDtypeStruct((t, P_COLS), bf16),
            jax.ShapeDtypeStruct((t, LANES), f32),
        ],
        scratch_shapes=[pltpu.VMEM((tm, D_MODEL), bf16)],
        compiler_params=_params("parallel", "arbitrary"),
        name="in_proj",
    )(x2d, ln_g, ln_b, w_main, w_ab, alog_row, dtb_row)


def _na_kernel(q_ref, k_ref, v_ref, km_ref, vm_ref, bias_ref, o_ref, *, rows):
    scale = NA_HEAD_DIM ** -0.5
    kr = NA_WIN_ROWS
    lane = lax.broadcasted_iota(jnp.int32, (GRID_W, LANES), 1)
    low = lane < NA_HEAD_DIM
    first = lax.broadcasted_iota(jnp.int32, (2 * GRID_W, LANES), 0) < GRID_W
    lane2 = lax.broadcasted_iota(jnp.int32, (2 * GRID_W, LANES), 1)
    pair_mask = jnp.where(first == (lane2 < NA_HEAD_DIM), scale, 0.0).astype(bf16)
    pairs = range(NA_HEADS // 2)
    sl = [slice(LANES * p, LANES * (p + 1)) for p in pairs]

    nr = NA_ROWS_PER_ITER
    assert rows % nr == 0

    def row_block(it, carry):
        units = [(i, p) for i in range(nr) for p in pairs]
        r = [it * nr + i for i in range(nr)]
        r0 = [jnp.clip(r[i] - kr // 2, 0, rows - kr) for i in range(nr)]
        d0 = [r0[i] - r[i] + (NA_WIN_ROWS - 1) for i in range(nr)]
        qoff = [pl.multiple_of(r[i] * GRID_W, GRID_W) for i in range(nr)]
        koff = [pl.multiple_of(r0[i] * GRID_W, GRID_W) for i in range(nr)]
        q2 = [jnp.concatenate([q_ref[pl.ds(qoff[i], GRID_W), sl[p]]] * 2, axis=0) * pair_mask for i, p in units]
        s = [_dot_nt(q, k_ref[pl.ds(koff[i], kr * GRID_W), sl[p]]) + bias_ref[p, d0[i]] for (i, p), q in zip(units, q2)]
        sm = [_dot_nt(q, km_ref[:, sl[p]]) for (i, p), q in zip(units, q2)]
        m = [jnp.maximum(jnp.max(a, -1, keepdims=True), jnp.max(b, -1, keepdims=True)) for a, b in zip(s, sm)]
        e = [jnp.exp(a - c) for a, c in zip(s, m)]
        em = [jnp.exp(b - c) for b, c in zip(sm, m)]
        den = [jnp.sum(a, -1, keepdims=True) + jnp.sum(b, -1, keepdims=True) for a, b in zip(e, em)]
        o = [_dot(a, v_ref[pl.ds(koff[i], kr * GRID_W), sl[p]]) + _dot(b, vm_ref[:, sl[p]])
             for (i, p), a, b in zip(units, e, em)]
        o = [a / c for a, c in zip(o, den)]
        for (i, p), a in zip(units, o):
            o_ref[pl.ds(qoff[i], GRID_W), sl[p]] = jnp.where(low, a[:GRID_W], a[GRID_W:]).astype(bf16)
        return carry

    lax.fori_loop(0, rows // nr, row_block, 0)


def _na_bias_table(rpb):
    col = np.arange(GRID_W)
    col_start = np.clip(col - NA_WIN_COLS // 2, 0, GRID_W - NA_WIN_COLS)
    col_mask = (col[None, :] >= col_start[:, None]) & (col[None, :] < col_start[:, None] + NA_WIN_COLS)
    dc_idx = np.clip(col[None, :] - col[:, None] + NA_WIN_COLS - 1, 0, 2 * NA_WIN_COLS - 2)
    pick = (dc_idx[:, :, None] == np.arange(2 * NA_WIN_COLS - 1)).astype(np.float32)
    by_col = jnp.einsum("hrc,qkc->hrqk", rpb.astype(f32), pick, precision=lax.Precision.HIGHEST)
    by_col = jnp.where(col_mask[None, None], by_col, NEG)
    tbl = jnp.stack([by_col[:, d0:d0 + NA_WIN_ROWS] for d0 in range(NA_WIN_ROWS)], axis=1)
    tbl = tbl.reshape(NA_HEADS // 2, 2, NA_WIN_ROWS, NA_WIN_ROWS, GRID_W, GRID_W)
    return tbl.transpose(0, 2, 1, 4, 3, 5).reshape(NA_HEADS // 2, NA_WIN_ROWS, 2 * GRID_W, NA_WIN_ROWS * GRID_W)


def _na(p_real, p_meta, bias_tbl, batch, seq):
    rows = seq // GRID_W
    assert rows >= NA_WIN_ROWS
    col = lambda cb: (lambda b: (b, cb // 4))
    colm = lambda cb: (lambda b: (0, cb // 4))
    return pl.pallas_call(
        functools.partial(_na_kernel, rows=rows),
        grid=(batch,),
        in_specs=[
            pl.BlockSpec((seq, NA_W), col(CB_NA_Q)),
            pl.BlockSpec((seq, NA_W), col(CB_NA_K)),
            pl.BlockSpec((seq, NA_W), col(CB_NA_V)),
            pl.BlockSpec((N_META, NA_W), colm(CB_NA_K)),
            pl.BlockSpec((N_META, NA_W), colm(CB_NA_V)),
            pl.BlockSpec(bias_tbl.shape, lambda b: (0, 0, 0, 0)),
        ],
        out_specs=pl.BlockSpec((seq, NA_W), lambda b: (b, 0)),
        out_shape=jax.ShapeDtypeStruct((batch * seq, NA_W), bf16),
        compiler_params=_params("parallel"),
        name="na",
    )(p_real, p_real, p_real, p_meta, p_meta, bias_tbl)


GDN_LEAD = GDN_CHUNK - N_META
RAW_PAD = 8


GDN_BLOCK = 128
GDN_HEADS_PER_STEP = 2
GDN_BLOCKS_PER_ITER = 4


INV_BASE = 8


def _tri_masks(c, lower):
    ii = lax.broadcasted_iota(jnp.int32, (c, c), 0)
    jj = lax.broadcasted_iota(jnp.int32, (c, c), 1)
    same = lambda s: jnp.right_shift(ii, s.bit_length() - 1) == jnp.right_shift(jj, s.bit_length() - 1)
    levels = [same(INV_BASE)]
    s = INV_BASE
    while s < c:
        levels.append(same(2 * s) & jnp.logical_not(same(s)))
        s *= 2
    if lower:
        return ii == jj, jj <= ii, jj < ii, ii <= jj, levels
    return ii == jj, jj >= ii, jj > ii, ii >= jj, levels


def _unit_tri_inverse(ms, eye, levels):
    pws = [jnp.where(levels[0], -m, 0.0) for m in ms]
    inv = [jnp.where(eye, 1.0, 0.0) + n for n in pws]
    for _ in range(INV_BASE.bit_length() - 2):
        pws = [_dot(p, p) for p in pws]
        inv = [a + _dot(a, p) for a, p in zip(inv, pws)]
    for pair in levels[1:]:
        ts = [_dot(a, jnp.where(pair, m, 0.0)) for a, m in zip(inv, ms)]
        inv = [a - _dot(t, a) for a, t in zip(inv, ts)]
    return inv


def _gdn_prepare(chains):
    pre = []
    for qc, kc, vc, kk, qk, g_col, beta_col, masks in chains:
        eye, incl, strict, incl_t, levels = masks
        g_row = jnp.sum(jnp.where(eye, g_col, 0.0), axis=0, keepdims=True)
        gc_col = jnp.sum(jnp.where(incl, g_row, 0.0), axis=1, keepdims=True)
        gc_row = jnp.sum(jnp.where(incl_t, g_col, 0.0), axis=0, keepdims=True)
        total = jnp.sum(g_col, axis=0, keepdims=True)
        decay = jnp.where(incl, jnp.exp(jnp.minimum(gc_col - gc_row, 0.0)), 0.0)
        pre.append((gc_col, beta_col, total, decay, jnp.where(strict, kk * beta_col * decay, 0.0)))
    eye, levels = chains[0][7][0], chains[0][7][4]
    inv = _unit_tri_inverse([p[4] for p in pre], eye, levels)
    sols = [_dot(a, jnp.concatenate([vc * beta, kc * (beta * jnp.exp(gc))], axis=1))
            for (qc, kc, vc, *_), (gc, beta, *_), a in zip(chains, pre, inv)]
    return [(sol[:, :GDN_HEAD_DIM], sol[:, GDN_HEAD_DIM:], qc * jnp.exp(gc), qk * decay, kc * jnp.exp(total - gc),
             jnp.exp(total))
            for (qc, kc, vc, kk, qk, *_), (gc, beta, total, decay, _), sol in zip(chains, pre, sols)]


def _gdn_kernel(q_ref, k_ref, v_ref, z_ref, qm_ref, km_ref, vm_ref,
                wq_ref, wk_ref, wv_ref, gt_ref, gtm_ref, nw_ref, o_ref,
                raw, u_s, wqd_s, aqk_s, kd_s, gl_s, st_s, *, seq, hg):
    c = GDN_BLOCK
    cm = GDN_CHUNK
    nblk = seq // c
    h0 = pl.program_id(1) * hg
    in_refs = (q_ref, k_ref, v_ref)
    meta_refs = (qm_ref, km_ref, vm_ref)
    w_refs = (wq_ref, wk_ref, wv_ref)
    cols = lambda hh: slice(LANES * hh, LANES * (hh + 1))
    halo = GDN_CONV // 2
    lead = RAW_PAD + GDN_LEAD

    for hh in range(hg):
        for t in range(3):
            raw[hh, t, pl.ds(0, lead), :] = jnp.zeros((lead, LANES), f32)
            raw[hh, t, pl.ds(lead, N_META), :] = meta_refs[t][:, cols(hh)].astype(f32)
            raw[hh, t, pl.ds(lead + N_META + seq, RAW_PAD), :] = jnp.zeros((RAW_PAD, LANES), f32)

    def copy_in(j, carry):
        src = pl.multiple_of(j * c, c)
        dst = pl.multiple_of(lead + N_META + j * c, 8)
        for hh in range(hg):
            for t in range(3):
                raw[hh, t, pl.ds(dst, c), :] = in_refs[t][pl.ds(src, c), cols(hh)].astype(f32)
        return carry

    lax.fori_loop(0, nblk, copy_in, 0)

    def conv_block(hh, p0, rows):
        out = []
        for t, kind in enumerate("qkv"):
            w = w_refs[t][:, cols(hh)]
            acc = None
            for tap in range(GDN_CONV):
                off = RAW_PAD - halo + tap
                term = raw[hh, t, pl.ds(p0 + off, rows), :] * w[tap:tap + 1, :]
                acc = term if acc is None else acc + term
            y = _silu(acc)
            if kind != "v":
                c = float(GDN_HEAD_DIM) if kind == "q" else 1.0
                ss = jnp.dot((y * y).astype(bf16), jnp.full((LANES, LANES), c, bf16), preferred_element_type=f32)
                y = y * lax.rsqrt(ss + c * NORM_EPS)
            out.append(y)
        return out

    def gates(gb, hh):
        return pltpu.roll(gb, jnp.bitwise_and(LANES - 4 * (h0 + hh), LANES - 1), 1)

    masks_meta = _tri_masks(cm, True)
    live_col = lax.broadcasted_iota(jnp.int32, (cm, 1), 0) >= GDN_LEAD
    heads = range(hg)
    metas = [[jnp.where(live_col, y, 0.0) for y in conv_block(hh, 0, cm)] for hh in heads]
    qkk = [_dot_nt(jnp.concatenate([q0, k0], axis=0), k0) for q0, k0, _ in metas]
    chain_in = []
    gb_meta = jnp.concatenate([jnp.zeros((GDN_LEAD, LANES), f32), gtm_ref[...]], axis=0)
    for hh in heads:
        q0, k0, v0 = metas[hh]
        gm = gates(gb_meta, hh)
        for d in range(2):
            chain_in.append((q0, k0, v0, qkk[hh][cm:], qkk[hh][:cm], gm[:, d:d + 1], gm[:, 2 + d:3 + d], masks_meta))
    s0 = [_dot_tn(r[4], r[0]) for r in _gdn_prepare(chain_in)]
    for i, (hh, d) in enumerate((hh, d) for hh in heads for d in range(2)):
        st_s[hh, d] = s0[i]

    masks = (_tri_masks(c, True), _tri_masks(c, False))
    nb = GDN_BLOCKS_PER_ITER
    assert nblk % nb == 0

    def prepare_blocks(jj, carry):
        units = [(hh, b) for hh in heads for b in range(nb)]
        rows = [pl.multiple_of((jj * nb + b) * c, c) for b in range(nb)]
        convs = [conv_block(hh, pl.multiple_of(cm + rows[j], cm), c) for hh, j in units]
        qkk = [_dot_nt(jnp.concatenate([qc, kc], axis=0), kc) for qc, kc, _ in convs]
        chain_in, chain_id = [], []
        for (hh, j), (qc, kc, vc), qk_kk in zip(units, convs, qkk):
            gt = gates(gt_ref[pl.ds(rows[j], c), :], hh)
            for d in range(2):
                chain_in.append((qc, kc, vc, qk_kk[c:], qk_kk[:c], gt[:, d:d + 1], gt[:, 2 + d:3 + d], masks[d]))
                chain_id.append((hh, d, j))
        for (hh, d, j), (u, w, qd, aqk, kd, g_last) in zip(chain_id, _gdn_prepare(chain_in)):
            r0 = rows[j]
            u_s[hh, d, pl.ds(r0, c), :] = u
            wqd_s[hh, d, pl.ds(pl.multiple_of(2 * r0, 2 * c), c), :] = w.astype(bf16)
            wqd_s[hh, d, pl.ds(pl.multiple_of(2 * r0 + c, c), c), :] = qd.astype(bf16)
            aqk_s[hh, d, pl.ds(r0, c), :] = aqk.astype(bf16)
            kd_s[hh, d, pl.ds(r0, c), :] = kd.astype(bf16)
            gl_s[hh, d, pl.ds(pl.multiple_of((jj * nb + j) * 8, 8), 8), :] = jnp.broadcast_to(g_last, (8, LANES))
        return carry

    lax.fori_loop(0, nblk // nb, prepare_blocks, 0)

    def scan_step(j, carry):
        rows = (pl.multiple_of(j * c, c), pl.multiple_of((nblk - 1 - j) * c, c))
        grow = (pl.multiple_of(j * 8, 8), pl.multiple_of((nblk - 1 - j) * 8, 8))
        chains = [(hh, d) for hh in heads for d in range(2)]
        states = [st_s[hh, d] for hh, d in chains]
        ws = [jnp.dot(wqd_s[hh, d, pl.ds(pl.multiple_of(2 * rows[d], 2 * c), 2 * c), :], s.astype(bf16),
                      preferred_element_type=f32) for (hh, d), s in zip(chains, states)]
        v_new = [(u_s[hh, d, pl.ds(rows[d], c), :] - w[:c]).astype(bf16) for (hh, d), w in zip(chains, ws)]
        outs = [w[c:] + jnp.dot(aqk_s[hh, d, pl.ds(rows[d], c), :], v, preferred_element_type=f32)
                for (hh, d), w, v in zip(chains, ws, v_new)]
        new_states = [s * gl_s[hh, d, pl.ds(grow[d], 1), :] + _dot_tn(kd_s[hh, d, pl.ds(rows[d], c), :], v)
                      for (hh, d), s, v in zip(chains, states, v_new)]
        for (hh, d), s in zip(chains, new_states):
            st_s[hh, d] = s
        for (hh, d), o in zip(chains, outs):
            u_s[hh, d, pl.ds(rows[d], c), :] = o
        return carry

    lax.fori_loop(0, nblk, scan_step, 0)

    def finish(j, carry):
        r0 = pl.multiple_of(j * c, c)
        for hh in range(hg):
            o = u_s[hh, 0, pl.ds(r0, c), :] + u_s[hh, 1, pl.ds(r0, c), :]
            o = o * lax.rsqrt(jnp.mean(o * o, -1, keepdims=True) + NORM_EPS) * nw_ref[...]
            o_ref[pl.ds(r0, c), cols(hh)] = (o * _silu(z_ref[pl.ds(r0, c), cols(hh)].astype(f32))).astype(bf16)
        return carry

    lax.fori_loop(0, nblk, finish, 0)


def _gdn(p_real, p_meta, conv_w, gates_real, gates_meta, norm_w, batch, seq):
    hg = GDN_HEADS_PER_STEP
    assert seq % GDN_BLOCK == 0 and GDN_HEADS % hg == 0
    nblk = seq // GDN_BLOCK
    wide = hg * LANES
    col = lambda cb: (lambda b, h: (b, cb // hg + h))
    colm = lambda cb: (lambda b, h: (0, cb // hg + h))
    colw = lambda k: (lambda b, h: (0, k * (GDN_HEADS // hg) + h))
    return pl.pallas_call(
        functools.partial(_gdn_kernel, seq=seq, hg=hg),
        grid=(batch, GDN_HEADS // hg),
        in_specs=[
            pl.BlockSpec((seq, wide), col(CB_G_Q)),
            pl.BlockSpec((seq, wide), col(CB_G_K)),
            pl.BlockSpec((seq, wide), col(CB_G_V)),
            pl.BlockSpec((seq, wide), col(CB_G_Z)),
            pl.BlockSpec((N_META, wide), colm(CB_G_Q)),
            pl.BlockSpec((N_META, wide), colm(CB_G_K)),
            pl.BlockSpec((N_META, wide), colm(CB_G_V)),
            pl.BlockSpec((GDN_CONV, wide), colw(0)),
            pl.BlockSpec((GDN_CONV, wide), colw(1)),
            pl.BlockSpec((GDN_CONV, wide), colw(2)),
            pl.BlockSpec((seq, LANES), lambda b, h: (b, 0)),
            pl.BlockSpec((N_META, LANES), lambda b, h: (0, 0)),
            pl.BlockSpec((1, LANES), lambda b, h: (0, 0)),
        ],
        out_specs=pl.BlockSpec((seq, wide), lambda b, h: (b, h)),
        out_shape=jax.ShapeDtypeStruct((batch * seq, GDN_W), bf16),
        scratch_shapes=[
            pltpu.VMEM((hg, 3, seq + GDN_CHUNK + 2 * RAW_PAD, LANES), f32),
            pltpu.VMEM((hg, 2, seq, LANES), f32),
            pltpu.VMEM((hg, 2, 2 * seq, LANES), bf16),
            pltpu.VMEM((hg, 2, seq, LANES), bf16),
            pltpu.VMEM((hg, 2, seq, LANES), bf16),
            pltpu.VMEM((hg, 2, 8 * nblk, LANES), f32),
            pltpu.VMEM((hg, 2, GDN_HEAD_DIM, GDN_HEAD_DIM), f32),
        ],
        compiler_params=_params("parallel", "parallel"),
        name="gdn",
    )(p_real, p_real, p_real, p_real, p_meta, p_meta, p_meta, conv_w, conv_w, conv_w, gates_real, gates_meta, norm_w)


def _route(logits):
    lane = lax.broadcasted_iota(jnp.int32, logits.shape, 1)
    big = jnp.int32(1 << 20)
    is_g = (lane >= N_EXPERTS) & (lane < N_EXPERTS + N_GROUPS)
    gl = jnp.where(is_g, logits, -jnp.inf)
    gmax = jnp.max(gl, -1, keepdims=True)
    g_idx = jnp.min(jnp.where(gl == gmax, lane - N_EXPERTS, big), -1, keepdims=True)
    g_w = 1.0 / jnp.sum(jnp.where(is_g, jnp.exp(gl - gmax), 0.0), -1, keepdims=True)
    in_grp = (lane < N_EXPERTS) & (jnp.right_shift(lane, 3) == g_idx)
    el = jnp.where(in_grp, logits, -jnp.inf)
    m1 = jnp.max(el, -1, keepdims=True)
    i1 = jnp.min(jnp.where(el == m1, lane, big), -1, keepdims=True)
    el2 = jnp.where(lane == i1, -jnp.inf, el)
    m2 = jnp.max(el2, -1, keepdims=True)
    i2 = jnp.min(jnp.where(el2 == m2, lane, big), -1, keepdims=True)
    e2 = jnp.exp(m2 - m1)
    w1 = g_w / (1.0 + e2)
    w2 = g_w * e2 / (1.0 + e2)
    out = jnp.where(lane == 0, i1.astype(f32), 0.0)
    out = jnp.where(lane == 1, i2.astype(f32), out)
    out = jnp.where(lane == 2, w1, out)
    return jnp.where(lane == 3, w2, out)


def _merge_kernel(x_ref, ona_ref, ogdn_ref, gna_ref, ggdn_ref, ln0g_ref, ln0b_ref, wna_ref, wgdn_ref,
                  wout_ref, ln1g_ref, ln1b_ref, wr_ref, br_ref, h1_ref, h1p_ref, route_ref):
    tm = x_ref.shape[0]
    rows = [pl.ds(i * (tm // MERGE_SPLIT), tm // MERGE_SPLIT) for i in range(MERGE_SPLIT)]
    y_na = [jnp.dot(ona_ref[r, :], wna_ref[...], preferred_element_type=f32) for r in rows]
    y_gdn = [jnp.dot(ogdn_ref[r, :], wgdn_ref[...], preferred_element_type=f32) for r in rows]
    merged = [_sigmoid(gna_ref[r, :].astype(f32)) * a + _sigmoid(ggdn_ref[r, :].astype(f32)) * b
              for r, a, b in zip(rows, y_na, y_gdn)]
    mix = [jnp.dot(m.astype(bf16), wout_ref[...], preferred_element_type=f32) for m in merged]
    h = [_layer_norm(x_ref[r, :], ln0g_ref[...], ln0b_ref[...]) for r in rows]
    h1 = [_layer_norm(DN_ALPHA * a + b, ln1g_ref[...], ln1b_ref[...]) for a, b in zip(h, mix)]
    logits = [_dot3(a, wr_ref[...]) + br_ref[...] for a in h1]
    for r, a, lg in zip(rows, h1, logits):
        h1_ref[r, :] = a
        h1p_ref[r, :] = _pack_halves(a)
        route_ref[r, :] = _route(lg)


def _merge(x2d, o_na, o_gdn, p_real, ln0_g, ln0_b, w_na, w_gdn, w_out, ln1_g, ln1_b, w_route, b_route, tm):
    t = x2d.shape[0]
    row = lambda i: (i, 0)
    const = lambda i: (0, 0)
    return pl.pallas_call(
        _merge_kernel,
        grid=(t // tm,),
        in_specs=[
            pl.BlockSpec((tm, D_MODEL), row),
            pl.BlockSpec((tm, NA_W), row),
            pl.BlockSpec((tm, GDN_W), row),
            pl.BlockSpec((tm, D_MODEL), lambda i: (i, CB_GATE_NA // 8)),
            pl.BlockSpec((tm, D_MODEL), lambda i: (i, CB_GATE_GDN // 8)),
            pl.BlockSpec((1, D_MODEL), const),
            pl.BlockSpec((1, D_MODEL), const),
            pl.BlockSpec((NA_W, D_MODEL), const),
            pl.BlockSpec((GDN_W, D_MODEL), const),
            pl.BlockSpec((D_MODEL, D_MODEL), const),
            pl.BlockSpec((1, D_MODEL), const),
            pl.BlockSpec((1, D_MODEL), const),
            pl.BlockSpec((D_MODEL, LANES), const),
            pl.BlockSpec((1, LANES), const),
        ],
        out_specs=[pl.BlockSpec((tm, D_MODEL), row), pl.BlockSpec((tm, HALF), row), pl.BlockSpec((tm, LANES), row)],
        out_shape=[jax.ShapeDtypeStruct((t, D_MODEL), f32), jax.ShapeDtypeStruct((t, HALF), f32),
                   jax.ShapeDtypeStruct((t, LANES), f32)],
        compiler_params=_params("parallel"),
        name="merge",
    )(x2d, o_na, o_gdn, p_real, p_real, ln0_g, ln0_b, w_na, w_gdn, w_out, ln1_g, ln1_b, w_route, b_route)


def _sc_dispatch(rows, dest0, dest1, n_slots):
    t, width = rows.shape
    mesh = plsc.VectorSubcoreMesh(core_axis_name="core", subcore_axis_name="subcore")
    n_workers = mesh.num_cores * mesh.num_subcores
    per_worker = t // n_workers
    assert t % n_workers == 0 and per_worker % GATHER_WINDOW == 0

    @functools.partial(
        pl.kernel, out_type=jax.ShapeDtypeStruct((n_slots, width), rows.dtype), mesh=mesh,
        scratch_types=[pltpu.VMEM((GATHER_WINDOW,), jnp.int32), pltpu.VMEM((GATHER_WINDOW,), jnp.int32),
                       pltpu.VMEM((GATHER_WINDOW, width), rows.dtype)],
        name="row_dispatch")
    def dispatch(rows_hbm, d0_hbm, d1_hbm, out_hbm, d0_vmem, d1_vmem, rows_vmem):
        worker = lax.axis_index("subcore") * mesh.num_cores + lax.axis_index("core")

        @pl.loop(0, per_worker // GATHER_WINDOW)
        def _(step):
            base = pl.multiple_of(worker * per_worker + step * GATHER_WINDOW, GATHER_WINDOW)
            pltpu.sync_copy(d0_hbm.at[pl.ds(base, GATHER_WINDOW)], d0_vmem)
            pltpu.sync_copy(d1_hbm.at[pl.ds(base, GATHER_WINDOW)], d1_vmem)
            pltpu.sync_copy(rows_hbm.at[pl.ds(base, GATHER_WINDOW)], rows_vmem)
            pltpu.sync_copy(rows_vmem, out_hbm.at[d0_vmem])
            pltpu.sync_copy(rows_vmem, out_hbm.at[d1_vmem])

    return dispatch(rows, dest0, dest1)


def _sc_gather(table, idx):
    n = idx.shape[0]
    width = table.shape[1]
    mesh = plsc.VectorSubcoreMesh(core_axis_name="core", subcore_axis_name="subcore")
    n_workers = mesh.num_cores * mesh.num_subcores
    per_worker = n // n_workers
    assert n % n_workers == 0 and per_worker % GATHER_WINDOW == 0

    @functools.partial(
        pl.kernel, out_type=jax.ShapeDtypeStruct((n, width), table.dtype), mesh=mesh,
        scratch_types=[pltpu.VMEM((GATHER_WINDOW,), jnp.int32), pltpu.VMEM((GATHER_WINDOW, width), table.dtype)],
        name="row_gather")
    def gather(tbl_hbm, idx_hbm, out_hbm, idx_vmem, rows_vmem):
        worker = lax.axis_index("subcore") * mesh.num_cores + lax.axis_index("core")

        @pl.loop(0, per_worker // GATHER_WINDOW)
        def _(step):
            base = pl.multiple_of(worker * per_worker + step * GATHER_WINDOW, GATHER_WINDOW)
            pltpu.sync_copy(idx_hbm.at[pl.ds(base, GATHER_WINDOW)], idx_vmem)
            pltpu.sync_copy(tbl_hbm.at[idx_vmem], rows_vmem)
            pltpu.sync_copy(rows_vmem, out_hbm.at[pl.ds(base, GATHER_WINDOW)])

    return gather(table, idx)


def _expert_kernel(be_ref, valid_ref, x_ref, wg_ref, wu_ref, wd_ref, y_ref, wgu_s, wd_s):
    i = pl.program_id(0)
    valid = valid_ref[i]

    @pl.when(valid == 0)
    def _():
        y_ref[...] = jnp.zeros_like(y_ref)

    @pl.when(valid > 0)
    def _():
        prev = jnp.maximum(i, 1) - 1
        fresh = jnp.logical_or(i == 0, jnp.logical_or(be_ref[i] != be_ref[prev], valid_ref[prev] == 0))

        @pl.when(fresh)
        def _():
            wgu_s[:, :D_EXPERT] = wg_ref[...].astype(bf16)
            wgu_s[:, D_EXPERT:] = wu_ref[...].astype(bf16)
            wd_s[...] = wd_ref[...].astype(bf16)

        n = EXPERT_ROWS // EXPERT_SPLIT
        groups = [pl.ds(g * n, n) for g in range(EXPERT_SPLIT)]
        row = lax.broadcasted_iota(jnp.int32, (n, HALF), 0)
        xs = [_unpack_halves(jnp.where(row + g * n < valid, x_ref[r, :], 0.0)) for g, r in enumerate(groups)]
        gu = [jnp.dot(xa.astype(bf16), wgu_s[:HALF, :], preferred_element_type=f32)
              + jnp.dot(xb.astype(bf16), wgu_s[HALF:, :], preferred_element_type=f32) for xa, xb in xs]
        hdn = [(_silu(a[:, :D_EXPERT]) * a[:, D_EXPERT:]).astype(bf16) for a in gu]
        ys = [jnp.dot(a, wd_s[...], preferred_element_type=f32) for a in hdn]
        for r, a in zip(groups, ys):
            y_ref[r, :] = _pack_halves(a)


def _experts(x_pad, block_e, block_valid, w_gate, w_up, w_down):
    nb = x_pad.shape[0] // EXPERT_ROWS
    grid_spec = pltpu.PrefetchScalarGridSpec(
        num_scalar_prefetch=2,
        grid=(nb,),
        in_specs=[
            pl.BlockSpec((EXPERT_ROWS, HALF), lambda i, be, nv: (i, 0)),
            pl.BlockSpec((None, D_MODEL, D_EXPERT), lambda i, be, nv: (be[i], 0, 0)),
            pl.BlockSpec((None, D_MODEL, D_EXPERT), lambda i, be, nv: (be[i], 0, 0)),
            pl.BlockSpec((None, D_EXPERT, D_MODEL), lambda i, be, nv: (be[i], 0, 0)),
        ],
        out_specs=pl.BlockSpec((EXPERT_ROWS, HALF), lambda i, be, nv: (i, 0)),
        scratch_shapes=[pltpu.VMEM((D_MODEL, 2 * D_EXPERT), bf16), pltpu.VMEM((D_EXPERT, D_MODEL), bf16)],
    )
    return pl.pallas_call(
        _expert_kernel,
        grid_spec=grid_spec,
        out_shape=jax.ShapeDtypeStruct(x_pad.shape, f32),
        compiler_params=_params("arbitrary"),
        name="experts",
    )(block_e, block_valid, x_pad, w_gate, w_up, w_down)


def _combine_kernel(h1_ref, y1_ref, y2_ref, route_ref, g_ref, b_ref, o_ref):
    r = route_ref[...]
    y1a, y1b = _unpack_halves(y1_ref[...])
    y2a, y2b = _unpack_halves(y2_ref[...])
    w1, w2 = r[:, 2:3], r[:, 3:4]
    ffn = jnp.concatenate([y1a * w1 + y2a * w2, y1b * w1 + y2b * w2], axis=1)
    o_ref[...] = _layer_norm(DN_ALPHA * h1_ref[...] + ffn, g_ref[...], b_ref[...])


def _combine(h1, y_pairs, route, ln_g, ln_b, tm):
    t = h1.shape[0]
    row = lambda i: (i, 0)
    const = lambda i: (0, 0)
    return pl.pallas_call(
        _combine_kernel,
        grid=(t // tm,),
        in_specs=[
            pl.BlockSpec((tm, D_MODEL), row),
            pl.BlockSpec((None, tm, HALF), lambda i: (0, i, 0)),
            pl.BlockSpec((None, tm, HALF), lambda i: (1, i, 0)),
            pl.BlockSpec((tm, LANES), row),
            pl.BlockSpec((1, D_MODEL), const),
            pl.BlockSpec((1, D_MODEL), const),
        ],
        out_specs=pl.BlockSpec((tm, D_MODEL), row),
        out_shape=jax.ShapeDtypeStruct((t, D_MODEL), f32),
        compiler_params=_params("parallel"),
        name="combine",
    )(h1, y_pairs, y_pairs, route, ln_g, ln_b)


PLAN_ROWS = 512


def _choice_onehots(route):
    lane = lax.broadcasted_iota(jnp.int32, route.shape, 1).astype(f32)
    return lane == route[:, 0:1], lane == route[:, 1:2]


def _rank_kernel(route_ref, rank_ref, counts_ref, carry_ref):
    @pl.when(pl.program_id(0) == 0)
    def _():
        carry_ref[...] = jnp.zeros_like(carry_ref)

    tm = route_ref.shape[0]
    oh0, oh1 = _choice_onehots(route_ref[...])
    both = jnp.where(oh0, 1.0, 0.0) + jnp.where(oh1, 1.0, 0.0)
    earlier = lax.broadcasted_iota(jnp.int32, (tm, tm), 1) < lax.broadcasted_iota(jnp.int32, (tm, tm), 0)
    before = jnp.dot(jnp.where(earlier, 1.0, 0.0).astype(bf16), both.astype(bf16), preferred_element_type=f32)
    before = before + carry_ref[...]
    rank0 = jnp.sum(jnp.where(oh0, before, 0.0), axis=1, keepdims=True)
    rank1 = jnp.sum(jnp.where(oh1, before, 0.0), axis=1, keepdims=True)
    lane = lax.broadcasted_iota(jnp.int32, (tm, LANES), 1)
    rank_ref[...] = jnp.where(lane == 0, rank0, jnp.where(lane == 1, rank1, 0.0))
    carry_ref[...] += jnp.sum(both, axis=0, keepdims=True)
    counts_ref[...] = carry_ref[...]


def _slot_kernel(route_ref, rank_ref, start_ref, dest_ref):
    oh0, oh1 = _choice_onehots(route_ref[...])
    rank = rank_ref[...]
    d0 = jnp.sum(jnp.where(oh0, start_ref[...], 0.0), axis=1, keepdims=True) + rank[:, 0:1]
    d1 = jnp.sum(jnp.where(oh1, start_ref[...], 0.0), axis=1, keepdims=True) + rank[:, 1:2]
    lane = lax.broadcasted_iota(jnp.int32, rank.shape, 1)
    by_lane = jnp.where(lane == 0, d0, jnp.where(lane == 1, d1, 0.0))
    dest_ref[...] = by_lane.T[:8, :].astype(jnp.int32)


def _dispatch_plan(route, t):
    nb = t * TOP_K // EXPERT_ROWS + N_EXPERTS
    tm = _row_tile(t, PLAN_ROWS)
    row = lambda i: (i, 0)
    rank, counts_row = pl.pallas_call(
        _rank_kernel,
        grid=(t // tm,),
        in_specs=[pl.BlockSpec((tm, LANES), row)],
        out_specs=[pl.BlockSpec((tm, LANES), row), pl.BlockSpec((1, LANES), lambda i: (0, 0))],
        out_shape=[jax.ShapeDtypeStruct((t, LANES), f32), jax.ShapeDtypeStruct((1, LANES), f32)],
        scratch_shapes=[pltpu.VMEM((1, LANES), f32)],
        compiler_params=_params("arbitrary"),
        name="expert_rank",
    )(route)
    counts = counts_row[0, :N_EXPERTS].astype(jnp.int32)
    padded = (counts + EXPERT_ROWS - 1) // EXPERT_ROWS * EXPERT_ROWS
    pad_end = jnp.cumsum(padded)
    pad_start = pad_end - padded
    start_row = jnp.pad(pad_start.astype(f32), (0, LANES - N_EXPERTS)).reshape(1, LANES)
    dest = pl.pallas_call(
        _slot_kernel,
        grid=(t // tm,),
        in_specs=[pl.BlockSpec((tm, LANES), row), pl.BlockSpec((tm, LANES), row),
                  pl.BlockSpec((1, LANES), lambda i: (0, 0))],
        out_specs=pl.BlockSpec((8, tm), lambda i: (0, i)),
        out_shape=jax.ShapeDtypeStruct((8, t), jnp.int32),
        compiler_params=_params("parallel"),
        name="expert_slot",
    )(route, rank, start_row)
    block_start = jnp.arange(nb, dtype=jnp.int32) * EXPERT_ROWS
    block_e = jnp.sum((pad_end[None, :] <= block_start[:, None]).astype(jnp.int32), axis=1)
    block_e = jnp.minimum(block_e, N_EXPERTS - 1)
    block_valid = jnp.clip(counts[block_e] - (block_start - pad_start[block_e]), 0, EXPERT_ROWS)
    return dest[:TOP_K], block_e.astype(jnp.int32), block_valid.astype(jnp.int32)


def _row_tile(t, want):
    tm = min(t, want)
    assert t % tm == 0
    return tm


def kernel(x, meta_tokens, ln0_g, ln0_b, w_in, na_rpb, gdn_conv_w, gdn_a_log, gdn_dt_bias, gdn_norm_w,
           w_branch_na, w_branch_gdn, w_out, ln1_g, ln1_b, router_group_w, router_group_b, router_expert_w,
           router_expert_b, expert_w_gate, expert_w_up, expert_w_down, ln2_g, ln2_b):
    batch, seq, d = x.shape
    assert d == D_MODEL and seq % GRID_W == 0 and DEPTH == 1
    t = batch * seq
    l = 0
    x2d = x.reshape(t, d)
    vec = lambda v: v.reshape(1, -1).astype(f32)

    w = w_in[l]
    w_main = jnp.concatenate([w[:, 3 * NA_W:AB_OFF], w[:, AB_OFF + N_AB:], w[:, :3 * NA_W]], axis=1).astype(bf16)
    w_ab, alog_row, dtb_row = _gate_lane_layout(w, gdn_a_log[l], gdn_dt_bias[l])

    p_real, gb_real = _in_proj(x2d, vec(ln0_g), vec(ln0_b), w_main, w_ab, alog_row, dtb_row, _row_tile(t, 1024),
                               IN_PROJ_COLS)
    p_meta, gb_meta = _in_proj(meta_tokens.astype(f32), vec(ln0_g), vec(ln0_b), w_main, w_ab, alog_row, dtb_row,
                               N_META, IN_PROJ_COLS)

    o_na = _na(p_real, p_meta, _na_bias_table(na_rpb[l]), batch, seq)
    o_gdn = _gdn(p_real, p_meta, gdn_conv_w[l].astype(f32), gb_real, gb_meta, vec(gdn_norm_w[l]), batch, seq)

    w_route = jnp.pad(jnp.concatenate([router_expert_w[l], router_group_w[l]], axis=1).astype(f32),
                      ((0, 0), (0, LANES - N_EXPERTS - N_GROUPS)))
    b_route = jnp.pad(jnp.concatenate([router_expert_b[l], router_group_b[l]]).astype(f32),
                      (0, LANES - N_EXPERTS - N_GROUPS)).reshape(1, LANES)
    h1, h1_packed, route = _merge(x2d, o_na, o_gdn, p_real, vec(ln0_g), vec(ln0_b), w_branch_na[l].astype(bf16),
                       w_branch_gdn[l].astype(bf16), w_out[l].astype(bf16), vec(ln1_g[l]), vec(ln1_b[l]),
                       w_route, b_route, _row_tile(t, MERGE_ROWS))

    dest, block_e, block_valid = _dispatch_plan(route, t)
    x_pad = _sc_dispatch(h1_packed, dest[0], dest[1], block_e.shape[0] * EXPERT_ROWS)
    y_pad = _experts(x_pad, block_e, block_valid, expert_w_gate[l], expert_w_up[l], expert_w_down[l])
    y_pairs = _sc_gather(y_pad, dest.reshape(-1)).reshape(TOP_K, t, HALF)
    out = _combine(h1, y_pairs, route, vec(ln2_g[l]), vec(ln2_b[l]), _row_tile(t, 512))
    return out.reshape(batch, seq, d)
```

```python
import functools

import numpy as np
import jax
import jax.numpy as jnp
from jax import lax
from jax.experimental import pallas as pl
from jax.experimental.pallas import tpu as pltpu
from jax.experimental.pallas import tpu_sc as plsc

D_MODEL = 1024
DEPTH = 1
GRID_W = 64
N_META = 16
NA_HEADS = 8
NA_HEAD_DIM = 64
NA_W = NA_HEADS * NA_HEAD_DIM
NA_WIN_ROWS = 8
NA_WIN_COLS = 16
GDN_HEADS = 8
GDN_HEAD_DIM = 128
GDN_W = GDN_HEADS * GDN_HEAD_DIM
GDN_CONV = 5
GDN_CHUNK = 64
N_GROUPS = 4
EXPERTS_PER_GROUP = 8
N_EXPERTS = N_GROUPS * EXPERTS_PER_GROUP
TOP_K = 2
D_EXPERT = 256
LN_EPS = 1e-5
NORM_EPS = 1e-6
DN_ALPHA = (2 * DEPTH) ** 0.25

LANES = 128
VMEM_LIMIT = 48 * 1024 * 1024
NEG = -1e30

CB_G_Q, CB_G_K, CB_G_V, CB_G_Z = 0, 8, 16, 24
CB_GATE_NA, CB_GATE_GDN = 32, 40
CB_NA_Q, CB_NA_K, CB_NA_V = 48, 52, 56
P_COLS = 60 * LANES
AB_OFF = 3 * NA_W + 4 * GDN_W
N_AB = 4 * GDN_HEADS

IN_PROJ_COLS = 2560
NA_ROWS_PER_ITER = 4
MERGE_ROWS = 1024
MERGE_SPLIT = 2
EXPERT_ROWS = 512
EXPERT_SPLIT = 2
GATHER_WINDOW = 128

f32 = jnp.float32
bf16 = jnp.bfloat16


def _dot(a, b):
    return jnp.dot(a.astype(bf16), b.astype(bf16), preferred_element_type=f32)


def _dot_nt(a, b):
    return lax.dot_general(a.astype(bf16), b.astype(bf16), (((1,), (1,)), ((), ())),
                           preferred_element_type=f32)


def _dot_tn(a, b):
    return lax.dot_general(a.astype(bf16), b.astype(bf16), (((0,), (0,)), ((), ())),
                           preferred_element_type=f32)


def _split_bf16(x):
    hi = x.astype(bf16)
    lo = (x - hi.astype(f32)).astype(bf16)
    return hi, lo


def _dot3(x, w):
    xh, xl = _split_bf16(x)
    wh, wl = _split_bf16(w)
    d = lambda a, b: jnp.dot(a, b, preferred_element_type=f32)
    return d(xh, wh) + (d(xl, wh) + d(xh, wl))


def _layer_norm(x, g, b):
    xc = x - jnp.mean(x, -1, keepdims=True)
    var = jnp.mean(xc * xc, -1, keepdims=True)
    return xc * lax.rsqrt(var + LN_EPS) * g + b


def _sigmoid(x):
    return 0.5 * jnp.tanh(0.5 * x) + 0.5


def _silu(x):
    return x * _sigmoid(x)


HALF = D_MODEL // 2


def _pack_halves(x):
    def rounded(v):
        u = lax.bitcast_convert_type(v, jnp.uint32)
        return u + (jnp.uint32(0x7FFF) + ((u >> 16) & jnp.uint32(1)))

    word = (rounded(x[:, :HALF]) >> 16) | (rounded(x[:, HALF:]) & jnp.uint32(0xFFFF0000))
    return lax.bitcast_convert_type(word, f32)


def _unpack_halves(p):
    u = lax.bitcast_convert_type(p, jnp.uint32)
    return lax.bitcast_convert_type(u << 16, f32), lax.bitcast_convert_type(u & jnp.uint32(0xFFFF0000), f32)


def _params(*sem, vmem_limit=VMEM_LIMIT):
    return pltpu.CompilerParams(dimension_semantics=sem, vmem_limit_bytes=vmem_limit)


def _in_proj_kernel(x_ref, g_ref, b_ref, w_ref, wab_ref, alog_ref, dtb_ref, p_ref, gb_ref, hn_ref):
    @pl.when(pl.program_id(1) == 0)
    def _():
        h = _layer_norm(x_ref[...], g_ref[...], b_ref[...])
        hn_ref[...] = h.astype(bf16)
        ab = jnp.dot(hn_ref[...], wab_ref[...].astype(bf16), preferred_element_type=f32)
        x = ab + dtb_ref[...]
        softplus = jnp.maximum(x, 0.0) + jnp.log1p(jnp.exp(-jnp.abs(x)))
        lane = lax.broadcasted_iota(jnp.int32, ab.shape, 1)
        gb_ref[...] = jnp.where((lane & 3) < 2, -jnp.exp(alog_ref[...]) * softplus, _sigmoid(ab))

    p_ref[...] = jnp.dot(hn_ref[...], w_ref[...], preferred_element_type=f32).astype(bf16)


def _gate_lane_layout(w_in, a_log, dt_bias):
    w_ab = w_in[:, AB_OFF:AB_OFF + N_AB].astype(f32).reshape(D_MODEL, 4, GDN_HEADS).transpose(0, 2, 1)
    w_ab = jnp.pad(w_ab.reshape(D_MODEL, N_AB), ((0, 0), (0, LANES - N_AB)))
    row = lambda p: jnp.pad(jnp.pad(p.astype(f32).T, ((0, 0), (0, 2))).reshape(1, N_AB), ((0, 0), (0, LANES - N_AB)))
    return w_ab, row(a_log), row(dt_bias)


def _in_proj(x2d, ln_g, ln_b, w_main, w_ab, alog_row, dtb_row, tm, tn):
    t = x2d.shape[0]
    return pl.pallas_call(
        _in_proj_kernel,
        grid=(t // tm, P_COLS // tn),
        in_specs=[
            pl.BlockSpec((tm, D_MODEL), lambda i, j: (i, 0)),
            pl.BlockSpec((1, D_MODEL), lambda i, j: (0, 0)),
            pl.BlockSpec((1, D_MODEL), lambda i, j: (0, 0)),
            pl.BlockSpec((D_MODEL, tn), lambda i, j: (0, j)),
            pl.BlockSpec((D_MODEL, LANES), lambda i, j: (0, 0)),
            pl.BlockSpec((1, LANES), lambda i, j: (0, 0)),
            pl.BlockSpec((1, LANES), lambda i, j: (0, 0)),
        ],
        out_specs=[
            pl.BlockSpec((tm, tn), lambda i, j: (i, j)),
            pl.BlockSpec((tm, LANES), lambda i, j: (i, 0)),
        ],
        out_shape=[
            jax.ShapeDtypeStruct((t, P_COLS), bf16),
            jax.ShapeDtypeStruct((t, LANES), f32),
        ],
        scratch_shapes=[pltpu.VMEM((tm, D_MODEL), bf16)],
        compiler_params=_params("parallel", "arbitrary"),
        name="in_proj",
    )(x2d, ln_g, ln_b, w_main, w_ab, alog_row, dtb_row)


def _na_kernel(q_ref, k_ref, v_ref, km_ref, vm_ref, bias_ref, o_ref, *, rows):
    scale = NA_HEAD_DIM ** -0.5
    kr = NA_WIN_ROWS
    lane = lax.broadcasted_iota(jnp.int32, (GRID_W, LANES), 1)
    low = lane < NA_HEAD_DIM
    first = lax.broadcasted_iota(jnp.int32, (2 * GRID_W, LANES), 0) < GRID_W
    lane2 = lax.broadcasted_iota(jnp.int32, (2 * GRID_W, LANES), 1)
    pair_mask = jnp.where(first == (lane2 < NA_HEAD_DIM), scale, 0.0).astype(bf16)
    pairs = range(NA_HEADS // 2)
    sl = [slice(LANES * p, LANES * (p + 1)) for p in pairs]

    nr = NA_ROWS_PER_ITER
    assert rows % nr == 0

    def row_block(it, carry):
        units = [(i, p) for i in range(nr) for p in pairs]
        r = [it * nr + i for i in range(nr)]
        r0 = [jnp.clip(r[i] - kr // 2, 0, rows - kr) for i in range(nr)]
        d0 = [r0[i] - r[i] + (NA_WIN_ROWS - 1) for i in range(nr)]
        qoff = [pl.multiple_of(r[i] * GRID_W, GRID_W) for i in range(nr)]
        koff = [pl.multiple_of(r0[i] * GRID_W, GRID_W) for i in range(nr)]
        q2 = [jnp.concatenate([q_ref[pl.ds(qoff[i], GRID_W), sl[p]]] * 2, axis=0) * pair_mask for i, p in units]
        s = [_dot_nt(q, k_ref[pl.ds(koff[i], kr * GRID_W), sl[p]]) + bias_ref[p, d0[i]] for (i, p), q in zip(units, q2)]
        sm = [_dot_nt(q, km_ref[:, sl[p]]) for (i, p), q in zip(units, q2)]
        m = [jnp.maximum(jnp.max(a, -1, keepdims=True), jnp.max(b, -1, keepdims=True)) for a, b in zip(s, sm)]
        e = [jnp.exp(a - c) for a, c in zip(s, m)]
        em = [jnp.exp(b - c) for b, c in zip(sm, m)]
        den = [jnp.sum(a, -1, keepdims=True) + jnp.sum(b, -1, keepdims=True) for a, b in zip(e, em)]
        o = [_dot(a, v_ref[pl.ds(koff[i], kr * GRID_W), sl[p]]) + _dot(b, vm_ref[:, sl[p]])
             for (i, p), a, b in zip(units, e, em)]
        o = [a / c for a, c in zip(o, den)]
        for (i, p), a in zip(units, o):
            o_ref[pl.ds(qoff[i], GRID_W), sl[p]] = jnp.where(low, a[:GRID_W], a[GRID_W:]).astype(bf16)
        return carry

    lax.fori_loop(0, rows // nr, row_block, 0)


def _na_bias_table(rpb):
    col = np.arange(GRID_W)
    col_start = np.clip(col - NA_WIN_COLS // 2, 0, GRID_W - NA_WIN_COLS)
    col_mask = (col[None, :] >= col_start[:, None]) & (col[None, :] < col_start[:, None] + NA_WIN_COLS)
    dc_idx = np.clip(col[None, :] - col[:, None] + NA_WIN_COLS - 1, 0, 2 * NA_WIN_COLS - 2)
    pick = (dc_idx[:, :, None] == np.arange(2 * NA_WIN_COLS - 1)).astype(np.float32)
    by_col = jnp.einsum("hrc,qkc->hrqk", rpb.astype(f32), pick, precision=lax.Precision.HIGHEST)
    by_col = jnp.where(col_mask[None, None], by_col, NEG)
    tbl = jnp.stack([by_col[:, d0:d0 + NA_WIN_ROWS] for d0 in range(NA_WIN_ROWS)], axis=1)
    tbl = tbl.reshape(NA_HEADS // 2, 2, NA_WIN_ROWS, NA_WIN_ROWS, GRID_W, GRID_W)
    return tbl.transpose(0, 2, 1, 4, 3, 5).reshape(NA_HEADS // 2, NA_WIN_ROWS, 2 * GRID_W, NA_WIN_ROWS * GRID_W)


def _na(p_real, p_meta, bias_tbl, batch, seq):
    rows = seq // GRID_W
    assert rows >= NA_WIN_ROWS
    col = lambda cb: (lambda b: (b, cb // 4))
    colm = lambda cb: (lambda b: (0, cb // 4))
    return pl.pallas_call(
        functools.partial(_na_kernel, rows=rows),
        grid=(batch,),
        in_specs=[
            pl.BlockSpec((seq, NA_W), col(CB_NA_Q)),
            pl.BlockSpec((seq, NA_W), col(CB_NA_K)),
            pl.BlockSpec((seq, NA_W), col(CB_NA_V)),
            pl.BlockSpec((N_META, NA_W), colm(CB_NA_K)),
            pl.BlockSpec((N_META, NA_W), colm(CB_NA_V)),
            pl.BlockSpec(bias_tbl.shape, lambda b: (0, 0, 0, 0)),
        ],
        out_specs=pl.BlockSpec((seq, NA_W), lambda b: (b, 0)),
        out_shape=jax.ShapeDtypeStruct((batch * seq, NA_W), bf16),
        compiler_params=_params("parallel"),
        name="na",
    )(p_real, p_real, p_real, p_meta, p_meta, bias_tbl)


GDN_LEAD = GDN_CHUNK - N_META
STAGE_PAD = N_META
GDN_VMEM_LIMIT = 58 * 1024 * 1024


GDN_BLOCK = 128
GDN_HEADS_PER_STEP = 4
GDN_BLOCKS_PER_ITER = 2


INV_BASE = 8


def _tri_masks(c, lower):
    ii = lax.broadcasted_iota(jnp.int32, (c, c), 0)
    jj = lax.broadcasted_iota(jnp.int32, (c, c), 1)
    same = lambda s: jnp.right_shift(ii, s.bit_length() - 1) == jnp.right_shift(jj, s.bit_length() - 1)
    levels = [same(INV_BASE)]
    s = INV_BASE
    while s < c:
        levels.append(same(2 * s) & jnp.logical_not(same(s)))
        s *= 2
    if lower:
        return ii == jj, jj <= ii, jj < ii, ii <= jj, levels
    return ii == jj, jj >= ii, jj > ii, ii >= jj, levels


def _unit_tri_inverse(ms, eye, levels):
    pws = [jnp.where(levels[0], -m, 0.0) for m in ms]
    inv = [jnp.where(eye, 1.0, 0.0) + n for n in pws]
    for _ in range(INV_BASE.bit_length() - 2):
        pws = [_dot(p, p) for p in pws]
        inv = [a + _dot(a, p) for a, p in zip(inv, pws)]
    for pair in levels[1:]:
        ts = [_dot(a, jnp.where(pair, m, 0.0)) for a, m in zip(inv, ms)]
        inv = [a - _dot(t, a) for a, t in zip(inv, ts)]
    return inv


def _gdn_prepare(chains):
    pre = []
    for qc, kc, vc, kk, qk, g_col, beta_col, masks in chains:
        eye, incl, strict, incl_t, levels = masks
        g_row = jnp.sum(jnp.where(eye, g_col, 0.0), axis=0, keepdims=True)
        gc_col = jnp.sum(jnp.where(incl, g_row, 0.0), axis=1, keepdims=True)
        gc_row = jnp.sum(jnp.where(incl_t, g_col, 0.0), axis=0, keepdims=True)
        total = jnp.sum(g_col, axis=0, keepdims=True)
        decay = jnp.where(incl, jnp.exp(jnp.minimum(gc_col - gc_row, 0.0)), 0.0)
        pre.append((gc_col, beta_col, total, decay, jnp.where(strict, kk * beta_col * decay, 0.0)))
    eye, levels = chains[0][7][0], chains[0][7][4]
    inv = _unit_tri_inverse([p[4] for p in pre], eye, levels)
    sols = [_dot(a, jnp.concatenate([vc * beta, kc * (beta * jnp.exp(gc))], axis=1))
            for (qc, kc, vc, *_), (gc, beta, *_), a in zip(chains, pre, inv)]
    return [(sol[:, :GDN_HEAD_DIM], sol[:, GDN_HEAD_DIM:], qc * jnp.exp(gc), qk * decay, kc * jnp.exp(total - gc),
             jnp.exp(total))
            for (qc, kc, vc, kk, qk, *_), (gc, beta, total, decay, _), sol in zip(chains, pre, sols)]


def _gdn_kernel(q_ref, k_ref, v_ref, z_ref, qm_ref, km_ref, vm_ref,
                wq_ref, wk_ref, wv_ref, gt_ref, gtm_ref, nw_ref, o_ref,
                stage, u_s, wqd_s, aqk_s, kd_s, gl_s, st_s, *, seq, hg):
    c = GDN_BLOCK
    cm = GDN_CHUNK
    nblk = seq // c
    nb = GDN_BLOCKS_PER_ITER
    assert nblk % nb == 0
    h0 = pl.program_id(1) * hg
    in_refs = (q_ref, k_ref, v_ref)
    meta_refs = (qm_ref, km_ref, vm_ref)
    w_refs = (wq_ref, wk_ref, wv_ref)
    cols = lambda hh: slice(LANES * hh, LANES * (hh + 1))
    halo = GDN_CONV // 2
    pad = STAGE_PAD

    def conv_block(hh, p0, rows):
        out = []
        for t, kind in enumerate("qkv"):
            w = w_refs[t][:, cols(hh)]
            acc = None
            for tap in range(GDN_CONV):
                term = stage[hh, t, pl.ds(p0 - halo + tap, rows), :] * w[tap:tap + 1, :]
                acc = term if acc is None else acc + term
            y = _silu(acc)
            if kind != "v":
                dh = float(GDN_HEAD_DIM) if kind == "q" else 1.0
                ss = jnp.dot((y * y).astype(bf16), jnp.full((LANES, LANES), dh, bf16), preferred_element_type=f32)
                y = y * lax.rsqrt(ss + dh * NORM_EPS)
            out.append(y)
        return out

    def gates(gb, hh):
        return pltpu.roll(gb, jnp.bitwise_and(LANES - 4 * (h0 + hh), LANES - 1), 1)

    masks_meta = _tri_masks(cm, True)
    live_col = lax.broadcasted_iota(jnp.int32, (cm, 1), 0) >= GDN_LEAD
    heads = range(hg)
    for hh in heads:
        for t in range(3):
            stage[hh, t, pl.ds(0, pad + GDN_LEAD), :] = jnp.zeros((pad + GDN_LEAD, LANES), f32)
            stage[hh, t, pl.ds(pad + GDN_LEAD, N_META), :] = meta_refs[t][:, cols(hh)].astype(f32)
            stage[hh, t, pl.ds(pad + cm, pad), :] = in_refs[t][pl.ds(0, pad), cols(hh)].astype(f32)
    metas = [[jnp.where(live_col, y, 0.0) for y in conv_block(hh, pad, cm)] for hh in heads]
    qkk = [_dot_nt(jnp.concatenate([q0, k0], axis=0), k0) for q0, k0, _ in metas]
    chain_in = []
    gb_meta = jnp.concatenate([jnp.zeros((GDN_LEAD, LANES), f32), gtm_ref[...]], axis=0)
    for hh in heads:
        q0, k0, v0 = metas[hh]
        gm = gates(gb_meta, hh)
        for d in range(2):
            chain_in.append((q0, k0, v0, qkk[hh][cm:], qkk[hh][:cm], gm[:, d:d + 1], gm[:, 2 + d:3 + d], masks_meta))
    s0 = [_dot_tn(r[4], r[0]) for r in _gdn_prepare(chain_in)]
    for i, (hh, d) in enumerate((hh, d) for hh in heads for d in range(2)):
        st_s[hh, d] = s0[i]

    masks = (_tri_masks(c, True), _tri_masks(c, False))

    def prepare_blocks(jj, carry):
        units = [(hh, b) for hh in heads for b in range(nb)]
        rows = [pl.multiple_of((jj * nb + b) * c, c) for b in range(nb)]
        first = pl.multiple_of(jj * (nb * c), nb * c)
        before = pl.multiple_of(jnp.maximum(first - pad, 0), pad)
        after = pl.multiple_of(jnp.minimum(first + nb * c, seq - pad), pad)
        for hh in heads:
            for t in range(3):
                prev = in_refs[t][pl.ds(before, pad), cols(hh)]
                stage[hh, t, pl.ds(0, pad), :] = jnp.where(jj == 0, meta_refs[t][:, cols(hh)], prev).astype(f32)
                stage[hh, t, pl.ds(pad, nb * c), :] = in_refs[t][pl.ds(first, nb * c), cols(hh)].astype(f32)
                nxt = in_refs[t][pl.ds(after, pad), cols(hh)].astype(f32)
                stage[hh, t, pl.ds(pad + nb * c, pad), :] = jnp.where(jj == nblk // nb - 1, 0.0, nxt)
        convs = [conv_block(hh, pad + j * c, c) for hh, j in units]
        qkk = [_dot_nt(jnp.concatenate([qc, kc], axis=0), kc) for qc, kc, _ in convs]
        chain_in, chain_id = [], []
        for (hh, j), (qc, kc, vc), qk_kk in zip(units, convs, qkk):
            gt = gates(gt_ref[pl.ds(rows[j], c), :], hh)
            for d in range(2):
                chain_in.append((qc, kc, vc, qk_kk[c:], qk_kk[:c], gt[:, d:d + 1], gt[:, 2 + d:3 + d], masks[d]))
                chain_id.append((hh, d, j))
        for (hh, d, j), (u, w, qd, aqk, kd, g_last) in zip(chain_id, _gdn_prepare(chain_in)):
            r0 = rows[j]
            u_s[hh, d, pl.ds(r0, c), :] = u
            wqd_s[hh, d, pl.ds(pl.multiple_of(2 * r0, 2 * c), c), :] = w.astype(bf16)
            wqd_s[hh, d, pl.ds(pl.multiple_of(2 * r0 + c, c), c), :] = qd.astype(bf16)
            aqk_s[hh, d, pl.ds(r0, c), :] = aqk.astype(bf16)
            kd_s[hh, d, pl.ds(r0, c), :] = kd.astype(bf16)
            gl_s[hh, d, pl.ds(pl.multiple_of((jj * nb + j) * 8, 8), 8), :] = jnp.broadcast_to(g_last, (8, LANES))
        return carry

    lax.fori_loop(0, nblk // nb, prepare_blocks, 0)

    def scan_step(j, carry):
        rows = (pl.multiple_of(j * c, c), pl.multiple_of((nblk - 1 - j) * c, c))
        grow = (pl.multiple_of(j * 8, 8), pl.multiple_of((nblk - 1 - j) * 8, 8))
        chains = [(hh, d) for hh in heads for d in range(2)]
        states = [st_s[hh, d] for hh, d in chains]
        ws = [jnp.dot(wqd_s[hh, d, pl.ds(pl.multiple_of(2 * rows[d], 2 * c), 2 * c), :], s.astype(bf16),
                      preferred_element_type=f32) for (hh, d), s in zip(chains, states)]
        v_new = [(u_s[hh, d, pl.ds(rows[d], c), :] - w[:c]).astype(bf16) for (hh, d), w in zip(chains, ws)]
        outs = [w[c:] + jnp.dot(aqk_s[hh, d, pl.ds(rows[d], c), :], v, preferred_element_type=f32)
                for (hh, d), w, v in zip(chains, ws, v_new)]
        new_states = [s * gl_s[hh, d, pl.ds(grow[d], 1), :] + _dot_tn(kd_s[hh, d, pl.ds(rows[d], c), :], v)
                      for (hh, d), s, v in zip(chains, states, v_new)]
        for (hh, d), s in zip(chains, new_states):
            st_s[hh, d] = s
        for (hh, d), o in zip(chains, outs):
            u_s[hh, d, pl.ds(rows[d], c), :] = o
        return carry

    lax.fori_loop(0, nblk, scan_step, 0)

    def finish(j, carry):
        r0 = pl.multiple_of(j * c, c)
        for hh in range(hg):
            o = u_s[hh, 0, pl.ds(r0, c), :] + u_s[hh, 1, pl.ds(r0, c), :]
            o = o * lax.rsqrt(jnp.mean(o * o, -1, keepdims=True) + NORM_EPS) * nw_ref[...]
            o_ref[pl.ds(r0, c), cols(hh)] = (o * _silu(z_ref[pl.ds(r0, c), cols(hh)].astype(f32))).astype(bf16)
        return carry

    lax.fori_loop(0, nblk, finish, 0)


def _gdn(p_real, p_meta, conv_w, gates_real, gates_meta, norm_w, batch, seq):
    hg = GDN_HEADS_PER_STEP
    assert seq % GDN_BLOCK == 0 and GDN_HEADS % hg == 0
    nblk = seq // GDN_BLOCK
    wide = hg * LANES
    col = lambda cb: (lambda b, h: (b, cb // hg + h))
    colm = lambda cb: (lambda b, h: (0, cb // hg + h))
    colw = lambda k: (lambda b, h: (0, k * (GDN_HEADS // hg) + h))
    return pl.pallas_call(
        functools.partial(_gdn_kernel, seq=seq, hg=hg),
        grid=(batch, GDN_HEADS // hg),
        in_specs=[
            pl.BlockSpec((seq, wide), col(CB_G_Q)),
            pl.BlockSpec((seq, wide), col(CB_G_K)),
            pl.BlockSpec((seq, wide), col(CB_G_V)),
            pl.BlockSpec((seq, wide), col(CB_G_Z), pipeline_mode=pl.Buffered(1)),
            pl.BlockSpec((N_META, wide), colm(CB_G_Q)),
            pl.BlockSpec((N_META, wide), colm(CB_G_K)),
            pl.BlockSpec((N_META, wide), colm(CB_G_V)),
            pl.BlockSpec((GDN_CONV, wide), colw(0)),
            pl.BlockSpec((GDN_CONV, wide), colw(1)),
            pl.BlockSpec((GDN_CONV, wide), colw(2)),
            pl.BlockSpec((seq, LANES), lambda b, h: (b, 0)),
            pl.BlockSpec((N_META, LANES), lambda b, h: (0, 0)),
            pl.BlockSpec((1, LANES), lambda b, h: (0, 0)),
        ],
        out_specs=pl.BlockSpec((seq, wide), lambda b, h: (b, h)),
        out_shape=jax.ShapeDtypeStruct((batch * seq, GDN_W), bf16),
        scratch_shapes=[
            pltpu.VMEM((hg, 3, GDN_BLOCKS_PER_ITER * GDN_BLOCK + 2 * STAGE_PAD, LANES), f32),
            pltpu.VMEM((hg, 2, seq, LANES), f32),
            pltpu.VMEM((hg, 2, 2 * seq, LANES), bf16),
            pltpu.VMEM((hg, 2, seq, LANES), bf16),
            pltpu.VMEM((hg, 2, seq, LANES), bf16),
            pltpu.VMEM((hg, 2, 8 * nblk, LANES), f32),
            pltpu.VMEM((hg, 2, GDN_HEAD_DIM, GDN_HEAD_DIM), f32),
        ],
        compiler_params=_params("parallel", "parallel", vmem_limit=GDN_VMEM_LIMIT),
        name="gdn",
    )(p_real, p_real, p_real, p_real, p_meta, p_meta, p_meta, conv_w, conv_w, conv_w, gates_real, gates_meta, norm_w)


def _route(logits):
    lane = lax.broadcasted_iota(jnp.int32, logits.shape, 1)
    big = jnp.int32(1 << 20)
    is_g = (lane >= N_EXPERTS) & (lane < N_EXPERTS + N_GROUPS)
    gl = jnp.where(is_g, logits, -jnp.inf)
    gmax = jnp.max(gl, -1, keepdims=True)
    g_idx = jnp.min(jnp.where(gl == gmax, lane - N_EXPERTS, big), -1, keepdims=True)
    g_w = 1.0 / jnp.sum(jnp.where(is_g, jnp.exp(gl - gmax), 0.0), -1, keepdims=True)
    in_grp = (lane < N_EXPERTS) & (jnp.right_shift(lane, 3) == g_idx)
    el = jnp.where(in_grp, logits, -jnp.inf)
    m1 = jnp.max(el, -1, keepdims=True)
    i1 = jnp.min(jnp.where(el == m1, lane, big), -1, keepdims=True)
    el2 = jnp.where(lane == i1, -jnp.inf, el)
    m2 = jnp.max(el2, -1, keepdims=True)
    i2 = jnp.min(jnp.where(el2 == m2, lane, big), -1, keepdims=True)
    e2 = jnp.exp(m2 - m1)
    w1 = g_w / (1.0 + e2)
    w2 = g_w * e2 / (1.0 + e2)
    out = jnp.where(lane == 0, i1.astype(f32), 0.0)
    out = jnp.where(lane == 1, i2.astype(f32), out)
    out = jnp.where(lane == 2, w1, out)
    return jnp.where(lane == 3, w2, out)


def _merge_kernel(x_ref, ona_ref, ogdn_ref, gna_ref, ggdn_ref, ln0g_ref, ln0b_ref, wna_ref, wgdn_ref,
                  wout_ref, ln1g_ref, ln1b_ref, wr_ref, br_ref, h1_ref, h1p_ref, route_ref):
    tm = x_ref.shape[0]
    rows = [pl.ds(i * (tm // MERGE_SPLIT), tm // MERGE_SPLIT) for i in range(MERGE_SPLIT)]
    y_na = [jnp.dot(ona_ref[r, :], wna_ref[...], preferred_element_type=f32) for r in rows]
    y_gdn = [jnp.dot(ogdn_ref[r, :], wgdn_ref[...], preferred_element_type=f32) for r in rows]
    merged = [_sigmoid(gna_ref[r, :].astype(f32)) * a + _sigmoid(ggdn_ref[r, :].astype(f32)) * b
              for r, a, b in zip(rows, y_na, y_gdn)]
    mix = [jnp.dot(m.astype(bf16), wout_ref[...], preferred_element_type=f32) for m in merged]
    h = [_layer_norm(x_ref[r, :], ln0g_ref[...], ln0b_ref[...]) for r in rows]
    h1 = [_layer_norm(DN_ALPHA * a + b, ln1g_ref[...], ln1b_ref[...]) for a, b in zip(h, mix)]
    logits = [_dot3(a, wr_ref[...]) + br_ref[...] for a in h1]
    for r, a, lg in zip(rows, h1, logits):
        h1_ref[r, :] = a
        h1p_ref[r, :] = _pack_halves(a)
        route_ref[r, :] = _route(lg)


def _merge(x2d, o_na, o_gdn, p_real, ln0_g, ln0_b, w_na, w_gdn, w_out, ln1_g, ln1_b, w_route, b_route, tm):
    t = x2d.shape[0]
    row = lambda i: (i, 0)
    const = lambda i: (0, 0)
    return pl.pallas_call(
        _merge_kernel,
        grid=(t // tm,),
        in_specs=[
            pl.BlockSpec((tm, D_MODEL), row),
            pl.BlockSpec((tm, NA_W), row),
            pl.BlockSpec((tm, GDN_W), row),
            pl.BlockSpec((tm, D_MODEL), lambda i: (i, CB_GATE_NA // 8)),
            pl.BlockSpec((tm, D_MODEL), lambda i: (i, CB_GATE_GDN // 8)),
            pl.BlockSpec((1, D_MODEL), const),
            pl.BlockSpec((1, D_MODEL), const),
            pl.BlockSpec((NA_W, D_MODEL), const),
            pl.BlockSpec((GDN_W, D_MODEL), const),
            pl.BlockSpec((D_MODEL, D_MODEL), const),
            pl.BlockSpec((1, D_MODEL), const),
            pl.BlockSpec((1, D_MODEL), const),
            pl.BlockSpec((D_MODEL, LANES), const),
            pl.BlockSpec((1, LANES), const),
        ],
        out_specs=[pl.BlockSpec((tm, D_MODEL), row), pl.BlockSpec((tm, HALF), row), pl.BlockSpec((tm, LANES), row)],
        out_shape=[jax.ShapeDtypeStruct((t, D_MODEL), f32), jax.ShapeDtypeStruct((t, HALF), f32),
                   jax.ShapeDtypeStruct((t, LANES), f32)],
        compiler_params=_params("parallel"),
        name="merge",
    )(x2d, o_na, o_gdn, p_real, p_real, ln0_g, ln0_b, w_na, w_gdn, w_out, ln1_g, ln1_b, w_route, b_route)


def _sc_dispatch(rows, dest0, dest1, n_slots):
    t, width = rows.shape
    mesh = plsc.VectorSubcoreMesh(core_axis_name="core", subcore_axis_name="subcore")
    n_workers = mesh.num_cores * mesh.num_subcores
    per_worker = t // n_workers
    assert t % n_workers == 0 and per_worker % GATHER_WINDOW == 0

    @functools.partial(
        pl.kernel, out_type=jax.ShapeDtypeStruct((n_slots, width), rows.dtype), mesh=mesh,
        scratch_types=[pltpu.VMEM((GATHER_WINDOW,), jnp.int32), pltpu.VMEM((GATHER_WINDOW,), jnp.int32),
                       pltpu.VMEM((GATHER_WINDOW, width), rows.dtype)],
        name="row_dispatch")
    def dispatch(rows_hbm, d0_hbm, d1_hbm, out_hbm, d0_vmem, d1_vmem, rows_vmem):
        worker = lax.axis_index("subcore") * mesh.num_cores + lax.axis_index("core")

        @pl.loop(0, per_worker // GATHER_WINDOW)
        def _(step):
            base = pl.multiple_of(worker * per_worker + step * GATHER_WINDOW, GATHER_WINDOW)
            pltpu.sync_copy(d0_hbm.at[pl.ds(base, GATHER_WINDOW)], d0_vmem)
            pltpu.sync_copy(d1_hbm.at[pl.ds(base, GATHER_WINDOW)], d1_vmem)
            pltpu.sync_copy(rows_hbm.at[pl.ds(base, GATHER_WINDOW)], rows_vmem)
            pltpu.sync_copy(rows_vmem, out_hbm.at[d0_vmem])
            pltpu.sync_copy(rows_vmem, out_hbm.at[d1_vmem])

    return dispatch(rows, dest0, dest1)


def _sc_gather(table, idx):
    n = idx.shape[0]
    width = table.shape[1]
    mesh = plsc.VectorSubcoreMesh(core_axis_name="core", subcore_axis_name="subcore")
    n_workers = mesh.num_cores * mesh.num_subcores
    per_worker = n // n_workers
    assert n % n_workers == 0 and per_worker % GATHER_WINDOW == 0

    @functools.partial(
        pl.kernel, out_type=jax.ShapeDtypeStruct((n, width), table.dtype), mesh=mesh,
        scratch_types=[pltpu.VMEM((GATHER_WINDOW,), jnp.int32), pltpu.VMEM((GATHER_WINDOW, width), table.dtype)],
        name="row_gather")
    def gather(tbl_hbm, idx_hbm, out_hbm, idx_vmem, rows_vmem):
        worker = lax.axis_index("subcore") * mesh.num_cores + lax.axis_index("core")

        @pl.loop(0, per_worker // GATHER_WINDOW)
        def _(step):
            base = pl.multiple_of(worker * per_worker + step * GATHER_WINDOW, GATHER_WINDOW)
            pltpu.sync_copy(idx_hbm.at[pl.ds(base, GATHER_WINDOW)], idx_vmem)
            pltpu.sync_copy(tbl_hbm.at[idx_vmem], rows_vmem)
            pltpu.sync_copy(rows_vmem, out_hbm.at[pl.ds(base, GATHER_WINDOW)])

    return gather(table, idx)


def _expert_kernel(be_ref, valid_ref, x_ref, wg_ref, wu_ref, wd_ref, y_ref, wgu_s, wd_s):
    i = pl.program_id(0)
    valid = valid_ref[i]

    @pl.when(valid == 0)
    def _():
        y_ref[...] = jnp.zeros_like(y_ref)

    @pl.when(valid > 0)
    def _():
        prev = jnp.maximum(i, 1) - 1
        fresh = jnp.logical_or(i == 0, jnp.logical_or(be_ref[i] != be_ref[prev], valid_ref[prev] == 0))

        @pl.when(fresh)
        def _():
            wgu_s[:, :D_EXPERT] = wg_ref[...].astype(bf16)
            wgu_s[:, D_EXPERT:] = wu_ref[...].astype(bf16)
            wd_s[...] = wd_ref[...].astype(bf16)

        n = EXPERT_ROWS // EXPERT_SPLIT
        groups = [pl.ds(g * n, n) for g in range(EXPERT_SPLIT)]
        row = lax.broadcasted_iota(jnp.int32, (n, HALF), 0)
        xs = [_unpack_halves(jnp.where(row + g * n < valid, x_ref[r, :], 0.0)) for g, r in enumerate(groups)]
        gu = [jnp.dot(xa.astype(bf16), wgu_s[:HALF, :], preferred_element_type=f32)
              + jnp.dot(xb.astype(bf16), wgu_s[HALF:, :], preferred_element_type=f32) for xa, xb in xs]
        hdn = [(_silu(a[:, :D_EXPERT]) * a[:, D_EXPERT:]).astype(bf16) for a in gu]
        ys = [jnp.dot(a, wd_s[...], preferred_element_type=f32) for a in hdn]
        for r, a in zip(groups, ys):
            y_ref[r, :] = _pack_halves(a)


def _experts(x_pad, block_e, block_valid, w_gate, w_up, w_down):
    nb = x_pad.shape[0] // EXPERT_ROWS
    grid_spec = pltpu.PrefetchScalarGridSpec(
        num_scalar_prefetch=2,
        grid=(nb,),
        in_specs=[
            pl.BlockSpec((EXPERT_ROWS, HALF), lambda i, be, nv: (i, 0)),
            pl.BlockSpec((None, D_MODEL, D_EXPERT), lambda i, be, nv: (be[i], 0, 0)),
            pl.BlockSpec((None, D_MODEL, D_EXPERT), lambda i, be, nv: (be[i], 0, 0)),
            pl.BlockSpec((None, D_EXPERT, D_MODEL), lambda i, be, nv: (be[i], 0, 0)),
        ],
        out_specs=pl.BlockSpec((EXPERT_ROWS, HALF), lambda i, be, nv: (i, 0)),
        scratch_shapes=[pltpu.VMEM((D_MODEL, 2 * D_EXPERT), bf16), pltpu.VMEM((D_EXPERT, D_MODEL), bf16)],
    )
    return pl.pallas_call(
        _expert_kernel,
        grid_spec=grid_spec,
        out_shape=jax.ShapeDtypeStruct(x_pad.shape, f32),
        compiler_params=_params("arbitrary"),
        name="experts",
    )(block_e, block_valid, x_pad, w_gate, w_up, w_down)


def _combine_kernel(h1_ref, y1_ref, y2_ref, route_ref, g_ref, b_ref, o_ref):
    r = route_ref[...]
    y1a, y1b = _unpack_halves(y1_ref[...])
    y2a, y2b = _unpack_halves(y2_ref[...])
    w1, w2 = r[:, 2:3], r[:, 3:4]
    ffn = jnp.concatenate([y1a * w1 + y2a * w2, y1b * w1 + y2b * w2], axis=1)
    o_ref[...] = _layer_norm(DN_ALPHA * h1_ref[...] + ffn, g_ref[...], b_ref[...])


def _combine(h1, y_pairs, route, ln_g, ln_b, tm):
    t = h1.shape[0]
    row = lambda i: (i, 0)
    const = lambda i: (0, 0)
    return pl.pallas_call(
        _combine_kernel,
        grid=(t // tm,),
        in_specs=[
            pl.BlockSpec((tm, D_MODEL), row),
            pl.BlockSpec((None, tm, HALF), lambda i: (0, i, 0)),
            pl.BlockSpec((None, tm, HALF), lambda i: (1, i, 0)),
            pl.BlockSpec((tm, LANES), row),
            pl.BlockSpec((1, D_MODEL), const),
            pl.BlockSpec((1, D_MODEL), const),
        ],
        out_specs=pl.BlockSpec((tm, D_MODEL), row),
        out_shape=jax.ShapeDtypeStruct((t, D_MODEL), f32),
        compiler_params=_params("parallel"),
        name="combine",
    )(h1, y_pairs, y_pairs, route, ln_g, ln_b)


PLAN_ROWS = 512


def _choice_onehots(route):
    lane = lax.broadcasted_iota(jnp.int32, route.shape, 1).astype(f32)
    return lane == route[:, 0:1], lane == route[:, 1:2]


def _rank_kernel(route_ref, rank_ref, counts_ref, carry_ref):
    @pl.when(pl.program_id(0) == 0)
    def _():
        carry_ref[...] = jnp.zeros_like(carry_ref)

    tm = route_ref.shape[0]
    oh0, oh1 = _choice_onehots(route_ref[...])
    both = jnp.where(oh0, 1.0, 0.0) + jnp.where(oh1, 1.0, 0.0)
    earlier = lax.broadcasted_iota(jnp.int32, (tm, tm), 1) < lax.broadcasted_iota(jnp.int32, (tm, tm), 0)
    before = jnp.dot(jnp.where(earlier, 1.0, 0.0).astype(bf16), both.astype(bf16), preferred_element_type=f32)
    before = before + carry_ref[...]
    rank0 = jnp.sum(jnp.where(oh0, before, 0.0), axis=1, keepdims=True)
    rank1 = jnp.sum(jnp.where(oh1, before, 0.0), axis=1, keepdims=True)
    lane = lax.broadcasted_iota(jnp.int32, (tm, LANES), 1)
    rank_ref[...] = jnp.where(lane == 0, rank0, jnp.where(lane == 1, rank1, 0.0))
    carry_ref[...] += jnp.sum(both, axis=0, keepdims=True)
    counts_ref[...] = carry_ref[...]


def _slot_kernel(route_ref, rank_ref, start_ref, dest_ref):
    oh0, oh1 = _choice_onehots(route_ref[...])
    rank = rank_ref[...]
    d0 = jnp.sum(jnp.where(oh0, start_ref[...], 0.0), axis=1, keepdims=True) + rank[:, 0:1]
    d1 = jnp.sum(jnp.where(oh1, start_ref[...], 0.0), axis=1, keepdims=True) + rank[:, 1:2]
    lane = lax.broadcasted_iota(jnp.int32, rank.shape, 1)
    by_lane = jnp.where(lane == 0, d0, jnp.where(lane == 1, d1, 0.0))
    dest_ref[...] = by_lane.T[:8, :].astype(jnp.int32)


def _dispatch_plan(route, t):
    nb = t * TOP_K // EXPERT_ROWS + N_EXPERTS
    tm = _row_tile(t, PLAN_ROWS)
    row = lambda i: (i, 0)
    rank, counts_row = pl.pallas_call(
        _rank_kernel,
        grid=(t // tm,),
        in_specs=[pl.BlockSpec((tm, LANES), row)],
        out_specs=[pl.BlockSpec((tm, LANES), row), pl.BlockSpec((1, LANES), lambda i: (0, 0))],
        out_shape=[jax.ShapeDtypeStruct((t, LANES), f32), jax.ShapeDtypeStruct((1, LANES), f32)],
        scratch_shapes=[pltpu.VMEM((1, LANES), f32)],
        compiler_params=_params("arbitrary"),
        name="expert_rank",
    )(route)
    counts = counts_row[0, :N_EXPERTS].astype(jnp.int32)
    padded = (counts + EXPERT_ROWS - 1) // EXPERT_ROWS * EXPERT_ROWS
    pad_end = jnp.cumsum(padded)
    pad_start = pad_end - padded
    start_row = jnp.pad(pad_start.astype(f32), (0, LANES - N_EXPERTS)).reshape(1, LANES)
    dest = pl.pallas_call(
        _slot_kernel,
        grid=(t // tm,),
        in_specs=[pl.BlockSpec((tm, LANES), row), pl.BlockSpec((tm, LANES), row),
                  pl.BlockSpec((1, LANES), lambda i: (0, 0))],
        out_specs=pl.BlockSpec((8, tm), lambda i: (0, i)),
        out_shape=jax.ShapeDtypeStruct((8, t), jnp.int32),
        compiler_params=_params("parallel"),
        name="expert_slot",
    )(route, rank, start_row)
    block_start = jnp.arange(nb, dtype=jnp.int32) * EXPERT_ROWS
    block_e = jnp.sum((pad_end[None, :] <= block_start[:, None]).astype(jnp.int32), axis=1)
    block_e = jnp.minimum(block_e, N_EXPERTS - 1)
    block_valid = jnp.clip(counts[block_e] - (block_start - pad_start[block_e]), 0, EXPERT_ROWS)
    return dest[:TOP_K], block_e.astype(jnp.int32), block_valid.astype(jnp.int32)


def _row_tile(t, want):
    tm = min(t, want)
    assert t % tm == 0
    return tm


def kernel(x, meta_tokens, ln0_g, ln0_b, w_in, na_rpb, gdn_conv_w, gdn_a_log, gdn_dt_bias, gdn_norm_w,
           w_branch_na, w_branch_gdn, w_out, ln1_g, ln1_b, router_group_w, router_group_b, router_expert_w,
           router_expert_b, expert_w_gate, expert_w_up, expert_w_down, ln2_g, ln2_b):
    batch, seq, d = x.shape
    assert d == D_MODEL and seq % GRID_W == 0 and DEPTH == 1
    t = batch * seq
    l = 0
    x2d = x.reshape(t, d)
    vec = lambda v: v.reshape(1, -1).astype(f32)

    w = w_in[l]
    w_main = jnp.concatenate([w[:, 3 * NA_W:AB_OFF], w[:, AB_OFF + N_AB:], w[:, :3 * NA_W]], axis=1).astype(bf16)
    w_ab, alog_row, dtb_row = _gate_lane_layout(w, gdn_a_log[l], gdn_dt_bias[l])

    p_real, gb_real = _in_proj(x2d, vec(ln0_g), vec(ln0_b), w_main, w_ab, alog_row, dtb_row, _row_tile(t, 1024),
                               IN_PROJ_COLS)
    p_meta, gb_meta = _in_proj(meta_tokens.astype(f32), vec(ln0_g), vec(ln0_b), w_main, w_ab, alog_row, dtb_row,
                               N_META, IN_PROJ_COLS)

    o_na = _na(p_real, p_meta, _na_bias_table(na_rpb[l]), batch, seq)
    o_gdn = _gdn(p_real, p_meta, gdn_conv_w[l].astype(f32), gb_real, gb_meta, vec(gdn_norm_w[l]), batch, seq)

    w_route = jnp.pad(jnp.concatenate([router_expert_w[l], router_group_w[l]], axis=1).astype(f32),
                      ((0, 0), (0, LANES - N_EXPERTS - N_GROUPS)))
    b_route = jnp.pad(jnp.concatenate([router_expert_b[l], router_group_b[l]]).astype(f32),
                      (0, LANES - N_EXPERTS - N_GROUPS)).reshape(1, LANES)
    h1, h1_packed, route = _merge(x2d, o_na, o_gdn, p_real, vec(ln0_g), vec(ln0_b), w_branch_na[l].astype(bf16),
                       w_branch_gdn[l].astype(bf16), w_out[l].astype(bf16), vec(ln1_g[l]), vec(ln1_b[l]),
                       w_route, b_route, _row_tile(t, MERGE_ROWS))

    dest, block_e, block_valid = _dispatch_plan(route, t)
    x_pad = _sc_dispatch(h1_packed, dest[0], dest[1], block_e.shape[0] * EXPERT_ROWS)
    y_pad = _experts(x_pad, block_e, block_valid, expert_w_gate[l], expert_w_up[l], expert_w_down[l])
    y_pairs = _sc_gather(y_pad, dest.reshape(-1)).reshape(TOP_K, t, HALF)
    out = _combine(h1, y_pairs, route, vec(ln2_g[l]), vec(ln2_b[l]), _row_tile(t, 512))
    return out.reshape(batch, seq, d)
```

```python
import functools

import numpy as np
import jax
import jax.numpy as jnp
from jax import lax
from jax.experimental import pallas as pl
from jax.experimental.pallas import tpu as pltpu
from jax.experimental.pallas import tpu_sc as plsc

D_MODEL = 1024
DEPTH = 1
GRID_W = 64
N_META = 16
NA_HEADS = 8
NA_HEAD_DIM = 64
NA_W = NA_HEADS * NA_HEAD_DIM
NA_WIN_ROWS = 8
NA_WIN_COLS = 16
GDN_HEADS = 8
GDN_HEAD_DIM = 128
GDN_W = GDN_HEADS * GDN_HEAD_DIM
GDN_CONV = 5
GDN_CHUNK = 64
N_GROUPS = 4
EXPERTS_PER_GROUP = 8
N_EXPERTS = N_GROUPS * EXPERTS_PER_GROUP
TOP_K = 2
D_EXPERT = 256
LN_EPS = 1e-5
NORM_EPS = 1e-6
DN_ALPHA = (2 * DEPTH) ** 0.25

LANES = 128
VMEM_LIMIT = 48 * 1024 * 1024
NEG = -1e30

CB_G_Q, CB_G_K, CB_G_V, CB_G_Z = 0, 8, 16, 24
CB_GATE_NA, CB_GATE_GDN = 32, 40
CB_NA_Q, CB_NA_K, CB_NA_V = 48, 52, 56
P_COLS = 60 * LANES
AB_OFF = 3 * NA_W + 4 * GDN_W
N_AB = 4 * GDN_HEADS

IN_PROJ_COLS = 2560
COMBINE_ROWS = 1024
IN_PROJ_SPLIT = 2
NA_ROWS_PER_ITER = 4
MERGE_ROWS = 1024
MERGE_SPLIT = 2
EXPERT_ROWS = 512
EXPERT_SPLIT = 2
GATHER_WINDOW = 128

f32 = jnp.float32
bf16 = jnp.bfloat16


def _dot(a, b):
    return jnp.dot(a.astype(bf16), b.astype(bf16), preferred_element_type=f32)


def _dot_nt(a, b):
    return lax.dot_general(a.astype(bf16), b.astype(bf16), (((1,), (1,)), ((), ())),
                           preferred_element_type=f32)


def _dot_tn(a, b):
    return lax.dot_general(a.astype(bf16), b.astype(bf16), (((0,), (0,)), ((), ())),
                           preferred_element_type=f32)


def _split_bf16(x):
    hi = x.astype(bf16)
    lo = (x - hi.astype(f32)).astype(bf16)
    return hi, lo


def _dot3(x, w):
    xh, xl = _split_bf16(x)
    wh, wl = _split_bf16(w)
    d = lambda a, b: jnp.dot(a, b, preferred_element_type=f32)
    return d(xh, wh) + (d(xl, wh) + d(xh, wl))


def _layer_norm(x, g, b):
    xc = x - jnp.mean(x, -1, keepdims=True)
    var = jnp.mean(xc * xc, -1, keepdims=True)
    return xc * lax.rsqrt(var + LN_EPS) * g + b


def _sigmoid(x):
    return 0.5 * jnp.tanh(0.5 * x) + 0.5


def _silu(x):
    return x * _sigmoid(x)


HALF = D_MODEL // 2


def _pack_halves(x):
    def rounded(v):
        u = lax.bitcast_convert_type(v, jnp.uint32)
        return u + (jnp.uint32(0x7FFF) + ((u >> 16) & jnp.uint32(1)))

    word = (rounded(x[:, :HALF]) >> 16) | (rounded(x[:, HALF:]) & jnp.uint32(0xFFFF0000))
    return lax.bitcast_convert_type(word, f32)


def _unpack_halves(p):
    u = lax.bitcast_convert_type(p, jnp.uint32)
    return lax.bitcast_convert_type(u << 16, f32), lax.bitcast_convert_type(u & jnp.uint32(0xFFFF0000), f32)


def _params(*sem, vmem_limit=VMEM_LIMIT):
    return pltpu.CompilerParams(dimension_semantics=sem, vmem_limit_bytes=vmem_limit)


def _in_proj_kernel(x_ref, g_ref, b_ref, w_ref, wab_ref, alog_ref, dtb_ref, p_ref, gb_ref, hn_ref):
    first = pl.program_id(1) == 0

    @pl.when(first)
    def _():
        tm = x_ref.shape[0]
        n = tm // IN_PROJ_SPLIT if tm % (16 * IN_PROJ_SPLIT) == 0 else tm
        for g in range(tm // n):
            r = pl.ds(g * n, n)
            hb = _layer_norm(x_ref[r, :], g_ref[...], b_ref[...]).astype(bf16)
            hn_ref[r, :] = hb
            p_ref[r, :] = jnp.dot(hb, w_ref[...], preferred_element_type=f32).astype(bf16)
            ab = jnp.dot(hb, wab_ref[...].astype(bf16), preferred_element_type=f32)
            x = ab + dtb_ref[...]
            softplus = jnp.maximum(x, 0.0) + jnp.log1p(jnp.exp(-jnp.abs(x)))
            lane = lax.broadcasted_iota(jnp.int32, ab.shape, 1)
            gb_ref[r, :] = jnp.where((lane & 3) < 2, -jnp.exp(alog_ref[...]) * softplus, _sigmoid(ab))

    @pl.when(jnp.logical_not(first))
    def _():
        p_ref[...] = jnp.dot(hn_ref[...], w_ref[...], preferred_element_type=f32).astype(bf16)


def _gate_lane_layout(w_in, a_log, dt_bias):
    w_ab = w_in[:, AB_OFF:AB_OFF + N_AB].astype(f32).reshape(D_MODEL, 4, GDN_HEADS).transpose(0, 2, 1)
    w_ab = jnp.pad(w_ab.reshape(D_MODEL, N_AB), ((0, 0), (0, LANES - N_AB)))
    row = lambda p: jnp.pad(jnp.pad(p.astype(f32).T, ((0, 0), (0, 2))).reshape(1, N_AB), ((0, 0), (0, LANES - N_AB)))
    return w_ab, row(a_log), row(dt_bias)


def _in_proj(x2d, ln_g, ln_b, w_main, w_ab, alog_row, dtb_row, tm, tn):
    t = x2d.shape[0]
    return pl.pallas_call(
        _in_proj_kernel,
        grid=(t // tm, P_COLS // tn),
        in_specs=[
            pl.BlockSpec((tm, D_MODEL), lambda i, j: (i, 0)),
            pl.BlockSpec((1, D_MODEL), lambda i, j: (0, 0)),
            pl.BlockSpec((1, D_MODEL), lambda i, j: (0, 0)),
            pl.BlockSpec((D_MODEL, tn), lambda i, j: (0, j)),
            pl.BlockSpec((D_MODEL, LANES), lambda i, j: (0, 0)),
            pl.BlockSpec((1, LANES), lambda i, j: (0, 0)),
            pl.BlockSpec((1, LANES), lambda i, j: (0, 0)),
        ],
        out_specs=[
            pl.BlockSpec((tm, tn), lambda i, j: (i, j)),
            pl.BlockSpec((tm, LANES), lambda i, j: (i, 0)),
        ],
        out_shape=[
            jax.ShapeDtypeStruct((t, P_COLS), bf16),
            jax.ShapeDtypeStruct((t, LANES), f32),
        ],
        scratch_shapes=[pltpu.VMEM((tm, D_MODEL), bf16)],
        compiler_params=_params("parallel", "arbitrary"),
        name="in_proj",
    )(x2d, ln_g, ln_b, w_main, w_ab, alog_row, dtb_row)


def _na_kernel(q_ref, k_ref, v_ref, km_ref, vm_ref, bias_ref, o_ref, *, rows):
    scale = NA_HEAD_DIM ** -0.5
    kr = NA_WIN_ROWS
    lane = lax.broadcasted_iota(jnp.int32, (GRID_W, LANES), 1)
    low = lane < NA_HEAD_DIM
    first = lax.broadcasted_iota(jnp.int32, (2 * GRID_W, LANES), 0) < GRID_W
    lane2 = lax.broadcasted_iota(jnp.int32, (2 * GRID_W, LANES), 1)
    pair_mask = jnp.where(first == (lane2 < NA_HEAD_DIM), scale, 0.0).astype(bf16)
    pairs = range(NA_HEADS // 2)
    sl = [slice(LANES * p, LANES * (p + 1)) for p in pairs]

    nr = NA_ROWS_PER_ITER
    assert rows % nr == 0

    def row_block(it, carry):
        units = [(i, p) for i in range(nr) for p in pairs]
        r = [it * nr + i for i in range(nr)]
        r0 = [jnp.clip(r[i] - kr // 2, 0, rows - kr) for i in range(nr)]
        d0 = [r0[i] - r[i] + (NA_WIN_ROWS - 1) for i in range(nr)]
        qoff = [pl.multiple_of(r[i] * GRID_W, GRID_W) for i in range(nr)]
        koff = [pl.multiple_of(r0[i] * GRID_W, GRID_W) for i in range(nr)]
        q2 = [jnp.concatenate([q_ref[pl.ds(qoff[i], GRID_W), sl[p]]] * 2, axis=0) * pair_mask for i, p in units]
        s = [_dot_nt(q, k_ref[pl.ds(koff[i], kr * GRID_W), sl[p]]) + bias_ref[p, d0[i]] for (i, p), q in zip(units, q2)]
        sm = [_dot_nt(q, km_ref[:, sl[p]]) for (i, p), q in zip(units, q2)]
        m = [jnp.maximum(jnp.max(a, -1, keepdims=True), jnp.max(b, -1, keepdims=True)) for a, b in zip(s, sm)]
        e = [jnp.exp(a - c) for a, c in zip(s, m)]
        em = [jnp.exp(b - c) for b, c in zip(sm, m)]
        den = [jnp.sum(a, -1, keepdims=True) + jnp.sum(b, -1, keepdims=True) for a, b in zip(e, em)]
        o = [_dot(a, v_ref[pl.ds(koff[i], kr * GRID_W), sl[p]]) + _dot(b, vm_ref[:, sl[p]])
             for (i, p), a, b in zip(units, e, em)]
        o = [a / c for a, c in zip(o, den)]
        for (i, p), a in zip(units, o):
            o_ref[pl.ds(qoff[i], GRID_W), sl[p]] = jnp.where(low, a[:GRID_W], a[GRID_W:]).astype(bf16)
        return carry

    lax.fori_loop(0, rows // nr, row_block, 0)


def _na_bias_table(rpb):
    col = np.arange(GRID_W)
    col_start = np.clip(col - NA_WIN_COLS // 2, 0, GRID_W - NA_WIN_COLS)
    col_mask = (col[None, :] >= col_start[:, None]) & (col[None, :] < col_start[:, None] + NA_WIN_COLS)
    dc_idx = np.clip(col[None, :] - col[:, None] + NA_WIN_COLS - 1, 0, 2 * NA_WIN_COLS - 2)
    pick = (dc_idx[:, :, None] == np.arange(2 * NA_WIN_COLS - 1)).astype(np.float32)
    by_col = jnp.einsum("hrc,qkc->hrqk", rpb.astype(f32), pick, precision=lax.Precision.HIGHEST)
    by_col = jnp.where(col_mask[None, None], by_col, NEG)
    tbl = jnp.stack([by_col[:, d0:d0 + NA_WIN_ROWS] for d0 in range(NA_WIN_ROWS)], axis=1)
    tbl = tbl.reshape(NA_HEADS // 2, 2, NA_WIN_ROWS, NA_WIN_ROWS, GRID_W, GRID_W)
    return tbl.transpose(0, 2, 1, 4, 3, 5).reshape(NA_HEADS // 2, NA_WIN_ROWS, 2 * GRID_W, NA_WIN_ROWS * GRID_W)


def _na(p_real, p_meta, bias_tbl, batch, seq):
    rows = seq // GRID_W
    assert rows >= NA_WIN_ROWS
    col = lambda cb: (lambda b: (b, cb // 4))
    colm = lambda cb: (lambda b: (0, cb // 4))
    return pl.pallas_call(
        functools.partial(_na_kernel, rows=rows),
        grid=(batch,),
        in_specs=[
            pl.BlockSpec((seq, NA_W), col(CB_NA_Q)),
            pl.BlockSpec((seq, NA_W), col(CB_NA_K)),
            pl.BlockSpec((seq, NA_W), col(CB_NA_V)),
            pl.BlockSpec((N_META, NA_W), colm(CB_NA_K)),
            pl.BlockSpec((N_META, NA_W), colm(CB_NA_V)),
            pl.BlockSpec(bias_tbl.shape, lambda b: (0, 0, 0, 0)),
        ],
        out_specs=pl.BlockSpec((seq, NA_W), lambda b: (b, 0)),
        out_shape=jax.ShapeDtypeStruct((batch * seq, NA_W), bf16),
        compiler_params=_params("parallel"),
        name="na",
    )(p_real, p_real, p_real, p_meta, p_meta, bias_tbl)


GDN_LEAD = GDN_CHUNK - N_META
STAGE_PAD = N_META
GDN_VMEM_LIMIT = 58 * 1024 * 1024


GDN_BLOCK = 128
GDN_HEADS_PER_STEP = 4
GDN_BLOCKS_PER_ITER = 2


INV_BASE = 8


def _tri_masks(c, lower):
    ii = lax.broadcasted_iota(jnp.int32, (c, c), 0)
    jj = lax.broadcasted_iota(jnp.int32, (c, c), 1)
    same = lambda s: jnp.right_shift(ii, s.bit_length() - 1) == jnp.right_shift(jj, s.bit_length() - 1)
    levels = [same(INV_BASE)]
    s = INV_BASE
    while s < c:
        levels.append(same(2 * s) & jnp.logical_not(same(s)))
        s *= 2
    if lower:
        return ii == jj, jj <= ii, jj < ii, ii <= jj, levels
    return ii == jj, jj >= ii, jj > ii, ii >= jj, levels


def _unit_tri_inverse(ms, eye, levels):
    pws = [jnp.where(levels[0], -m, 0.0) for m in ms]
    inv = [jnp.where(eye, 1.0, 0.0) + n for n in pws]
    for _ in range(INV_BASE.bit_length() - 2):
        pws = [_dot(p, p) for p in pws]
        inv = [a + _dot(a, p) for a, p in zip(inv, pws)]
    for pair in levels[1:]:
        ts = [_dot(a, jnp.where(pair, m, 0.0)) for a, m in zip(inv, ms)]
        inv = [a - _dot(t, a) for a, t in zip(inv, ts)]
    return inv


def _gdn_prepare(chains):
    pre = []
    for qc, kc, vc, kk, qk, g_col, beta_col, masks in chains:
        eye, incl, strict, incl_t, levels = masks
        g_row = jnp.sum(jnp.where(eye, g_col, 0.0), axis=0, keepdims=True)
        gc_col = jnp.sum(jnp.where(incl, g_row, 0.0), axis=1, keepdims=True)
        gc_row = jnp.sum(jnp.where(incl_t, g_col, 0.0), axis=0, keepdims=True)
        total = jnp.sum(g_col, axis=0, keepdims=True)
        decay = jnp.where(incl, jnp.exp(jnp.minimum(gc_col - gc_row, 0.0)), 0.0)
        pre.append((gc_col, beta_col, total, decay, jnp.where(strict, kk * beta_col * decay, 0.0)))
    eye, levels = chains[0][7][0], chains[0][7][4]
    inv = _unit_tri_inverse([p[4] for p in pre], eye, levels)
    sols = [_dot(a, jnp.concatenate([vc * beta, kc * (beta * jnp.exp(gc))], axis=1))
            for (qc, kc, vc, *_), (gc, beta, *_), a in zip(chains, pre, inv)]
    return [(sol[:, :GDN_HEAD_DIM], sol[:, GDN_HEAD_DIM:], qc * jnp.exp(gc), qk * decay, kc * jnp.exp(total - gc),
             jnp.exp(total))
            for (qc, kc, vc, kk, qk, *_), (gc, beta, total, decay, _), sol in zip(chains, pre, sols)]


def _gdn_kernel(q_ref, k_ref, v_ref, z_ref, qm_ref, km_ref, vm_ref,
                wq_ref, wk_ref, wv_ref, gt_ref, gtm_ref, nw_ref, o_ref,
                stage, u_s, wqd_s, aqk_s, kd_s, gl_s, st_s, *, seq, hg):
    c = GDN_BLOCK
    cm = GDN_CHUNK
    nblk = seq // c
    nb = GDN_BLOCKS_PER_ITER
    assert nblk % nb == 0
    h0 = pl.program_id(1) * hg
    in_refs = (q_ref, k_ref, v_ref)
    meta_refs = (qm_ref, km_ref, vm_ref)
    w_refs = (wq_ref, wk_ref, wv_ref)
    cols = lambda hh: slice(LANES * hh, LANES * (hh + 1))
    halo = GDN_CONV // 2
    pad = STAGE_PAD

    def conv_block(hh, p0, rows):
        out = []
        for t, kind in enumerate("qkv"):
            w = w_refs[t][:, cols(hh)]
            acc = None
            for tap in range(GDN_CONV):
                term = stage[hh, t, pl.ds(p0 - halo + tap, rows), :] * w[tap:tap + 1, :]
                acc = term if acc is None else acc + term
            y = _silu(acc)
            if kind != "v":
                dh = float(GDN_HEAD_DIM) if kind == "q" else 1.0
                ss = jnp.dot((y * y).astype(bf16), jnp.full((LANES, LANES), dh, bf16), preferred_element_type=f32)
                y = y * lax.rsqrt(ss + dh * NORM_EPS)
            out.append(y)
        return out

    def gates(gb, hh):
        return pltpu.roll(gb, jnp.bitwise_and(LANES - 4 * (h0 + hh), LANES - 1), 1)

    masks_meta = _tri_masks(cm, True)
    live_col = lax.broadcasted_iota(jnp.int32, (cm, 1), 0) >= GDN_LEAD
    heads = range(hg)
    for hh in heads:
        for t in range(3):
            stage[hh, t, pl.ds(0, pad + GDN_LEAD), :] = jnp.zeros((pad + GDN_LEAD, LANES), f32)
            stage[hh, t, pl.ds(pad + GDN_LEAD, N_META), :] = meta_refs[t][:, cols(hh)].astype(f32)
            stage[hh, t, pl.ds(pad + cm, pad), :] = in_refs[t][pl.ds(0, pad), cols(hh)].astype(f32)
    metas = [[jnp.where(live_col, y, 0.0) for y in conv_block(hh, pad, cm)] for hh in heads]
    qkk = [_dot_nt(jnp.concatenate([q0, k0], axis=0), k0) for q0, k0, _ in metas]
    chain_in = []
    gb_meta = jnp.concatenate([jnp.zeros((GDN_LEAD, LANES), f32), gtm_ref[...]], axis=0)
    for hh in heads:
        q0, k0, v0 = metas[hh]
        gm = gates(gb_meta, hh)
        for d in range(2):
            chain_in.append((q0, k0, v0, qkk[hh][cm:], qkk[hh][:cm], gm[:, d:d + 1], gm[:, 2 + d:3 + d], masks_meta))
    s0 = [_dot_tn(r[4], r[0]) for r in _gdn_prepare(chain_in)]
    for i, (hh, d) in enumerate((hh, d) for hh in heads for d in range(2)):
        st_s[hh, d] = s0[i]

    masks = (_tri_masks(c, True), _tri_masks(c, False))

    def prepare_blocks(jj, carry):
        units = [(hh, b) for hh in heads for b in range(nb)]
        rows = [pl.multiple_of((jj * nb + b) * c, c) for b in range(nb)]
        first = pl.multiple_of(jj * (nb * c), nb * c)
        before = pl.multiple_of(jnp.maximum(first - pad, 0), pad)
        after = pl.multiple_of(jnp.minimum(first + nb * c, seq - pad), pad)
        for hh in heads:
            for t in range(3):
                prev = in_refs[t][pl.ds(before, pad), cols(hh)]
                stage[hh, t, pl.ds(0, pad), :] = jnp.where(jj == 0, meta_refs[t][:, cols(hh)], prev).astype(f32)
                stage[hh, t, pl.ds(pad, nb * c), :] = in_refs[t][pl.ds(first, nb * c), cols(hh)].astype(f32)
                nxt = in_refs[t][pl.ds(after, pad), cols(hh)].astype(f32)
                stage[hh, t, pl.ds(pad + nb * c, pad), :] = jnp.where(jj == nblk // nb - 1, 0.0, nxt)
        convs = [conv_block(hh, pad + j * c, c) for hh, j in units]
        qkk = [_dot_nt(jnp.concatenate([qc, kc], axis=0), kc) for qc, kc, _ in convs]
        chain_in, chain_id = [], []
        for (hh, j), (qc, kc, vc), qk_kk in zip(units, convs, qkk):
            gt = gates(gt_ref[pl.ds(rows[j], c), :], hh)
            for d in range(2):
                chain_in.append((qc, kc, vc, qk_kk[c:], qk_kk[:c], gt[:, d:d + 1], gt[:, 2 + d:3 + d], masks[d]))
                chain_id.append((hh, d, j))
        for (hh, d, j), (u, w, qd, aqk, kd, g_last) in zip(chain_id, _gdn_prepare(chain_in)):
            r0 = rows[j]
            u_s[hh, d, pl.ds(r0, c), :] = u
            wqd_s[hh, d, pl.ds(pl.multiple_of(2 * r0, 2 * c), c), :] = w.astype(bf16)
            wqd_s[hh, d, pl.ds(pl.multiple_of(2 * r0 + c, c), c), :] = qd.astype(bf16)
            aqk_s[hh, d, pl.ds(r0, c), :] = aqk.astype(bf16)
            kd_s[hh, d, pl.ds(r0, c), :] = kd.astype(bf16)
            gl_s[hh, d, pl.ds(pl.multiple_of((jj * nb + j) * 8, 8), 8), :] = jnp.broadcast_to(g_last, (8, LANES))
        return carry

    lax.fori_loop(0, nblk // nb, prepare_blocks, 0)

    def scan_step(j, carry):
        rows = (pl.multiple_of(j * c, c), pl.multiple_of((nblk - 1 - j) * c, c))
        grow = (pl.multiple_of(j * 8, 8), pl.multiple_of((nblk - 1 - j) * 8, 8))
        chains = [(hh, d) for hh in heads for d in range(2)]
        states = [st_s[hh, d] for hh, d in chains]
        ws = [jnp.dot(wqd_s[hh, d, pl.ds(pl.multiple_of(2 * rows[d], 2 * c), 2 * c), :], s.astype(bf16),
                      preferred_element_type=f32) for (hh, d), s in zip(chains, states)]
        v_new = [(u_s[hh, d, pl.ds(rows[d], c), :] - w[:c]).astype(bf16) for (hh, d), w in zip(chains, ws)]
        outs = [w[c:] + jnp.dot(aqk_s[hh, d, pl.ds(rows[d], c), :], v, preferred_element_type=f32)
                for (hh, d), w, v in zip(chains, ws, v_new)]
        new_states = [s * gl_s[hh, d, pl.ds(grow[d], 1), :] + _dot_tn(kd_s[hh, d, pl.ds(rows[d], c), :], v)
                      for (hh, d), s, v in zip(chains, states, v_new)]
        for (hh, d), s in zip(chains, new_states):
            st_s[hh, d] = s
        for (hh, d), o in zip(chains, outs):
            u_s[hh, d, pl.ds(rows[d], c), :] = o
        return carry

    lax.fori_loop(0, nblk, scan_step, 0)

    def finish(j, carry):
        r0 = pl.multiple_of(j * c, c)
        for hh in range(hg):
            o = u_s[hh, 0, pl.ds(r0, c), :] + u_s[hh, 1, pl.ds(r0, c), :]
            o = o * lax.rsqrt(jnp.mean(o * o, -1, keepdims=True) + NORM_EPS) * nw_ref[...]
            o_ref[pl.ds(r0, c), cols(hh)] = (o * _silu(z_ref[pl.ds(r0, c), cols(hh)].astype(f32))).astype(bf16)
        return carry

    lax.fori_loop(0, nblk, finish, 0)


def _gdn(p_real, p_meta, conv_w, gates_real, gates_meta, norm_w, batch, seq):
    hg = GDN_HEADS_PER_STEP
    assert seq % GDN_BLOCK == 0 and GDN_HEADS % hg == 0
    nblk = seq // GDN_BLOCK
    wide = hg * LANES
    col = lambda cb: (lambda b, h: (b, cb // hg + h))
    colm = lambda cb: (lambda b, h: (0, cb // hg + h))
    colw = lambda k: (lambda b, h: (0, k * (GDN_HEADS // hg) + h))
    return pl.pallas_call(
        functools.partial(_gdn_kernel, seq=seq, hg=hg),
        grid=(batch, GDN_HEADS // hg),
        in_specs=[
            pl.BlockSpec((seq, wide), col(CB_G_Q)),
            pl.BlockSpec((seq, wide), col(CB_G_K)),
            pl.BlockSpec((seq, wide), col(CB_G_V)),
            pl.BlockSpec((seq, wide), col(CB_G_Z), pipeline_mode=pl.Buffered(1)),
            pl.BlockSpec((N_META, wide), colm(CB_G_Q)),
            pl.BlockSpec((N_META, wide), colm(CB_G_K)),
            pl.BlockSpec((N_META, wide), colm(CB_G_V)),
            pl.BlockSpec((GDN_CONV, wide), colw(0)),
            pl.BlockSpec((GDN_CONV, wide), colw(1)),
            pl.BlockSpec((GDN_CONV, wide), colw(2)),
            pl.BlockSpec((seq, LANES), lambda b, h: (b, 0)),
            pl.BlockSpec((N_META, LANES), lambda b, h: (0, 0)),
            pl.BlockSpec((1, LANES), lambda b, h: (0, 0)),
        ],
        out_specs=pl.BlockSpec((seq, wide), lambda b, h: (b, h)),
        out_shape=jax.ShapeDtypeStruct((batch * seq, GDN_W), bf16),
        scratch_shapes=[
            pltpu.VMEM((hg, 3, GDN_BLOCKS_PER_ITER * GDN_BLOCK + 2 * STAGE_PAD, LANES), f32),
            pltpu.VMEM((hg, 2, seq, LANES), f32),
            pltpu.VMEM((hg, 2, 2 * seq, LANES), bf16),
            pltpu.VMEM((hg, 2, seq, LANES), bf16),
            pltpu.VMEM((hg, 2, seq, LANES), bf16),
            pltpu.VMEM((hg, 2, 8 * nblk, LANES), f32),
            pltpu.VMEM((hg, 2, GDN_HEAD_DIM, GDN_HEAD_DIM), f32),
        ],
        compiler_params=_params("parallel", "parallel", vmem_limit=GDN_VMEM_LIMIT),
        name="gdn",
    )(p_real, p_real, p_real, p_real, p_meta, p_meta, p_meta, conv_w, conv_w, conv_w, gates_real, gates_meta, norm_w)


def _route(logits):
    lane = lax.broadcasted_iota(jnp.int32, logits.shape, 1)
    big = jnp.int32(1 << 20)
    is_g = (lane >= N_EXPERTS) & (lane < N_EXPERTS + N_GROUPS)
    gl = jnp.where(is_g, logits, -jnp.inf)
    gmax = jnp.max(gl, -1, keepdims=True)
    g_idx = jnp.min(jnp.where(gl == gmax, lane - N_EXPERTS, big), -1, keepdims=True)
    g_w = 1.0 / jnp.sum(jnp.where(is_g, jnp.exp(gl - gmax), 0.0), -1, keepdims=True)
    in_grp = (lane < N_EXPERTS) & (jnp.right_shift(lane, 3) == g_idx)
    el = jnp.where(in_grp, logits, -jnp.inf)
    m1 = jnp.max(el, -1, keepdims=True)
    i1 = jnp.min(jnp.where(el == m1, lane, big), -1, keepdims=True)
    el2 = jnp.where(lane == i1, -jnp.inf, el)
    m2 = jnp.max(el2, -1, keepdims=True)
    i2 = jnp.min(jnp.where(el2 == m2, lane, big), -1, keepdims=True)
    e2 = jnp.exp(m2 - m1)
    w1 = g_w / (1.0 + e2)
    w2 = g_w * e2 / (1.0 + e2)
    out = jnp.where(lane == 0, i1.astype(f32), 0.0)
    out = jnp.where(lane == 1, i2.astype(f32), out)
    out = jnp.where(lane == 2, w1, out)
    return jnp.where(lane == 3, w2, out)


def _merge_kernel(x_ref, ona_ref, ogdn_ref, gna_ref, ggdn_ref, ln0g_ref, ln0b_ref, wna_ref, wgdn_ref,
                  wout_ref, ln1g_ref, ln1b_ref, wr_ref, br_ref, h1_ref, h1p_ref, route_ref):
    tm = x_ref.shape[0]
    rows = [pl.ds(i * (tm // MERGE_SPLIT), tm // MERGE_SPLIT) for i in range(MERGE_SPLIT)]
    y_na = [jnp.dot(ona_ref[r, :], wna_ref[...], preferred_element_type=f32) for r in rows]
    y_gdn = [jnp.dot(ogdn_ref[r, :], wgdn_ref[...], preferred_element_type=f32) for r in rows]
    merged = [_sigmoid(gna_ref[r, :].astype(f32)) * a + _sigmoid(ggdn_ref[r, :].astype(f32)) * b
              for r, a, b in zip(rows, y_na, y_gdn)]
    mix = [jnp.dot(m.astype(bf16), wout_ref[...], preferred_element_type=f32) for m in merged]
    h = [_layer_norm(x_ref[r, :], ln0g_ref[...], ln0b_ref[...]) for r in rows]
    h1 = [_layer_norm(DN_ALPHA * a + b, ln1g_ref[...], ln1b_ref[...]) for a, b in zip(h, mix)]
    logits = [_dot3(a, wr_ref[...]) + br_ref[...] for a in h1]
    for r, a, lg in zip(rows, h1, logits):
        h1_ref[r, :] = a
        h1p_ref[r, :] = _pack_halves(a)
        route_ref[r, :] = _route(lg)


def _merge(x2d, o_na, o_gdn, p_real, ln0_g, ln0_b, w_na, w_gdn, w_out, ln1_g, ln1_b, w_route, b_route, tm):
    t = x2d.shape[0]
    row = lambda i: (i, 0)
    const = lambda i: (0, 0)
    return pl.pallas_call(
        _merge_kernel,
        grid=(t // tm,),
        in_specs=[
            pl.BlockSpec((tm, D_MODEL), row),
            pl.BlockSpec((tm, NA_W), row),
            pl.BlockSpec((tm, GDN_W), row),
            pl.BlockSpec((tm, D_MODEL), lambda i: (i, CB_GATE_NA // 8)),
            pl.BlockSpec((tm, D_MODEL), lambda i: (i, CB_GATE_GDN // 8)),
            pl.BlockSpec((1, D_MODEL), const),
            pl.BlockSpec((1, D_MODEL), const),
            pl.BlockSpec((NA_W, D_MODEL), const),
            pl.BlockSpec((GDN_W, D_MODEL), const),
            pl.BlockSpec((D_MODEL, D_MODEL), const),
            pl.BlockSpec((1, D_MODEL), const),
            pl.BlockSpec((1, D_MODEL), const),
            pl.BlockSpec((D_MODEL, LANES), const),
            pl.BlockSpec((1, LANES), const),
        ],
        out_specs=[pl.BlockSpec((tm, D_MODEL), row), pl.BlockSpec((tm, HALF), row), pl.BlockSpec((tm, LANES), row)],
        out_shape=[jax.ShapeDtypeStruct((t, D_MODEL), f32), jax.ShapeDtypeStruct((t, HALF), f32),
                   jax.ShapeDtypeStruct((t, LANES), f32)],
        compiler_params=_params("parallel"),
        name="merge",
    )(x2d, o_na, o_gdn, p_real, p_real, ln0_g, ln0_b, w_na, w_gdn, w_out, ln1_g, ln1_b, w_route, b_route)


def _sc_dispatch(rows, dest0, dest1, n_slots):
    t, width = rows.shape
    mesh = plsc.VectorSubcoreMesh(core_axis_name="core", subcore_axis_name="subcore")
    n_workers = mesh.num_cores * mesh.num_subcores
    per_worker = t // n_workers
    assert t % n_workers == 0 and per_worker % GATHER_WINDOW == 0

    @functools.partial(
        pl.kernel, out_type=jax.ShapeDtypeStruct((n_slots, width), rows.dtype), mesh=mesh,
        scratch_types=[pltpu.VMEM((GATHER_WINDOW,), jnp.int32), pltpu.VMEM((GATHER_WINDOW,), jnp.int32),
                       pltpu.VMEM((GATHER_WINDOW, width), rows.dtype)],
        name="row_dispatch")
    def dispatch(rows_hbm, d0_hbm, d1_hbm, out_hbm, d0_vmem, d1_vmem, rows_vmem):
        worker = lax.axis_index("subcore") * mesh.num_cores + lax.axis_index("core")

        @pl.loop(0, per_worker // GATHER_WINDOW)
        def _(step):
            base = pl.multiple_of(worker * per_worker + step * GATHER_WINDOW, GATHER_WINDOW)
            pltpu.sync_copy(d0_hbm.at[pl.ds(base, GATHER_WINDOW)], d0_vmem)
            pltpu.sync_copy(d1_hbm.at[pl.ds(base, GATHER_WINDOW)], d1_vmem)
            pltpu.sync_copy(rows_hbm.at[pl.ds(base, GATHER_WINDOW)], rows_vmem)
            pltpu.sync_copy(rows_vmem, out_hbm.at[d0_vmem])
            pltpu.sync_copy(rows_vmem, out_hbm.at[d1_vmem])

    return dispatch(rows, dest0, dest1)


def _sc_gather(table, idx):
    n = idx.shape[0]
    width = table.shape[1]
    mesh = plsc.VectorSubcoreMesh(core_axis_name="core", subcore_axis_name="subcore")
    n_workers = mesh.num_cores * mesh.num_subcores
    per_worker = n // n_workers
    assert n % n_workers == 0 and per_worker % GATHER_WINDOW == 0

    @functools.partial(
        pl.kernel, out_type=jax.ShapeDtypeStruct((n, width), table.dtype), mesh=mesh,
        scratch_types=[pltpu.VMEM((GATHER_WINDOW,), jnp.int32), pltpu.VMEM((GATHER_WINDOW, width), table.dtype)],
        name="row_gather")
    def gather(tbl_hbm, idx_hbm, out_hbm, idx_vmem, rows_vmem):
        worker = lax.axis_index("subcore") * mesh.num_cores + lax.axis_index("core")

        @pl.loop(0, per_worker // GATHER_WINDOW)
        def _(step):
            base = pl.multiple_of(worker * per_worker + step * GATHER_WINDOW, GATHER_WINDOW)
            pltpu.sync_copy(idx_hbm.at[pl.ds(base, GATHER_WINDOW)], idx_vmem)
            pltpu.sync_copy(tbl_hbm.at[idx_vmem], rows_vmem)
            pltpu.sync_copy(rows_vmem, out_hbm.at[pl.ds(base, GATHER_WINDOW)])

    return gather(table, idx)


def _expert_kernel(be_ref, valid_ref, x_ref, wg_ref, wu_ref, wd_ref, y_ref, wgu_s, wd_s):
    i = pl.program_id(0)
    valid = valid_ref[i]

    @pl.when(valid == 0)
    def _():
        y_ref[...] = jnp.zeros_like(y_ref)

    @pl.when(valid > 0)
    def _():
        prev = jnp.maximum(i, 1) - 1
        fresh = jnp.logical_or(i == 0, jnp.logical_or(be_ref[i] != be_ref[prev], valid_ref[prev] == 0))

        @pl.when(fresh)
        def _():
            wgu_s[:, :D_EXPERT] = wg_ref[...].astype(bf16)
            wgu_s[:, D_EXPERT:] = wu_ref[...].astype(bf16)
            wd_s[...] = wd_ref[...].astype(bf16)

        n = EXPERT_ROWS // EXPERT_SPLIT
        groups = [pl.ds(g * n, n) for g in range(EXPERT_SPLIT)]
        row = lax.broadcasted_iota(jnp.int32, (n, HALF), 0)
        xs = [_unpack_halves(jnp.where(row + g * n < valid, x_ref[r, :], 0.0)) for g, r in enumerate(groups)]
        gu = [jnp.dot(xa.astype(bf16), wgu_s[:HALF, :], preferred_element_type=f32)
              + jnp.dot(xb.astype(bf16), wgu_s[HALF:, :], preferred_element_type=f32) for xa, xb in xs]
        hdn = [(_silu(a[:, :D_EXPERT]) * a[:, D_EXPERT:]).astype(bf16) for a in gu]
        ys = [jnp.dot(a, wd_s[...], preferred_element_type=f32) for a in hdn]
        for r, a in zip(groups, ys):
            y_ref[r, :] = _pack_halves(a)


def _experts(x_pad, block_e, block_valid, w_gate, w_up, w_down):
    nb = x_pad.shape[0] // EXPERT_ROWS
    grid_spec = pltpu.PrefetchScalarGridSpec(
        num_scalar_prefetch=2,
        grid=(nb,),
        in_specs=[
            pl.BlockSpec((EXPERT_ROWS, HALF), lambda i, be, nv: (i, 0)),
            pl.BlockSpec((None, D_MODEL, D_EXPERT), lambda i, be, nv: (be[i], 0, 0)),
            pl.BlockSpec((None, D_MODEL, D_EXPERT), lambda i, be, nv: (be[i], 0, 0)),
            pl.BlockSpec((None, D_EXPERT, D_MODEL), lambda i, be, nv: (be[i], 0, 0)),
        ],
        out_specs=pl.BlockSpec((EXPERT_ROWS, HALF), lambda i, be, nv: (i, 0)),
        scratch_shapes=[pltpu.VMEM((D_MODEL, 2 * D_EXPERT), bf16), pltpu.VMEM((D_EXPERT, D_MODEL), bf16)],
    )
    return pl.pallas_call(
        _expert_kernel,
        grid_spec=grid_spec,
        out_shape=jax.ShapeDtypeStruct(x_pad.shape, f32),
        compiler_params=_params("arbitrary"),
        name="experts",
    )(block_e, block_valid, x_pad, w_gate, w_up, w_down)


def _combine_kernel(h1_ref, y1_ref, y2_ref, route_ref, g_ref, b_ref, o_ref):
    r = route_ref[...]
    y1a, y1b = _unpack_halves(y1_ref[...])
    y2a, y2b = _unpack_halves(y2_ref[...])
    w1, w2 = r[:, 2:3], r[:, 3:4]
    ffn = jnp.concatenate([y1a * w1 + y2a * w2, y1b * w1 + y2b * w2], axis=1)
    o_ref[...] = _layer_norm(DN_ALPHA * h1_ref[...] + ffn, g_ref[...], b_ref[...])


def _combine(h1, y_pairs, route, ln_g, ln_b, tm):
    t = h1.shape[0]
    row = lambda i: (i, 0)
    const = lambda i: (0, 0)
    return pl.pallas_call(
        _combine_kernel,
        grid=(t // tm,),
        in_specs=[
            pl.BlockSpec((tm, D_MODEL), row),
            pl.BlockSpec((None, tm, HALF), lambda i: (0, i, 0)),
            pl.BlockSpec((None, tm, HALF), lambda i: (1, i, 0)),
            pl.BlockSpec((tm, LANES), row),
            pl.BlockSpec((1, D_MODEL), const),
            pl.BlockSpec((1, D_MODEL), const),
        ],
        out_specs=pl.BlockSpec((tm, D_MODEL), row),
        out_shape=jax.ShapeDtypeStruct((t, D_MODEL), f32),
        compiler_params=_params("parallel"),
        name="combine",
    )(h1, y_pairs, y_pairs, route, ln_g, ln_b)


PLAN_ROWS = 512


def _choice_onehots(route):
    lane = lax.broadcasted_iota(jnp.int32, route.shape, 1).astype(f32)
    return lane == route[:, 0:1], lane == route[:, 1:2]


def _rank_kernel(route_ref, rank_ref, counts_ref, carry_ref):
    @pl.when(pl.program_id(0) == 0)
    def _():
        carry_ref[...] = jnp.zeros_like(carry_ref)

    tm = route_ref.shape[0]
    oh0, oh1 = _choice_onehots(route_ref[...])
    both = jnp.where(oh0, 1.0, 0.0) + jnp.where(oh1, 1.0, 0.0)
    earlier = lax.broadcasted_iota(jnp.int32, (tm, tm), 1) < lax.broadcasted_iota(jnp.int32, (tm, tm), 0)
    before = jnp.dot(jnp.where(earlier, 1.0, 0.0).astype(bf16), both.astype(bf16), preferred_element_type=f32)
    before = before + carry_ref[...]
    rank0 = jnp.sum(jnp.where(oh0, before, 0.0), axis=1, keepdims=True)
    rank1 = jnp.sum(jnp.where(oh1, before, 0.0), axis=1, keepdims=True)
    lane = lax.broadcasted_iota(jnp.int32, (tm, LANES), 1)
    rank_ref[...] = jnp.where(lane == 0, rank0, jnp.where(lane == 1, rank1, 0.0))
    carry_ref[...] += jnp.sum(both, axis=0, keepdims=True)
    counts_ref[...] = carry_ref[...]


def _slot_kernel(route_ref, rank_ref, start_ref, dest_ref):
    oh0, oh1 = _choice_onehots(route_ref[...])
    rank = rank_ref[...]
    d0 = jnp.sum(jnp.where(oh0, start_ref[...], 0.0), axis=1, keepdims=True) + rank[:, 0:1]
    d1 = jnp.sum(jnp.where(oh1, start_ref[...], 0.0), axis=1, keepdims=True) + rank[:, 1:2]
    lane = lax.broadcasted_iota(jnp.int32, rank.shape, 1)
    by_lane = jnp.where(lane == 0, d0, jnp.where(lane == 1, d1, 0.0))
    dest_ref[...] = by_lane.T[:8, :].astype(jnp.int32)


def _dispatch_plan(route, t):
    nb = t * TOP_K // EXPERT_ROWS + N_EXPERTS
    tm = _row_tile(t, PLAN_ROWS)
    row = lambda i: (i, 0)
    rank, counts_row = pl.pallas_call(
        _rank_kernel,
        grid=(t // tm,),
        in_specs=[pl.BlockSpec((tm, LANES), row)],
        out_specs=[pl.BlockSpec((tm, LANES), row), pl.BlockSpec((1, LANES), lambda i: (0, 0))],
        out_shape=[jax.ShapeDtypeStruct((t, LANES), f32), jax.ShapeDtypeStruct((1, LANES), f32)],
        scratch_shapes=[pltpu.VMEM((1, LANES), f32)],
        compiler_params=_params("arbitrary"),
        name="expert_rank",
    )(route)
    counts = counts_row[0, :N_EXPERTS].astype(jnp.int32)
    padded = (counts + EXPERT_ROWS - 1) // EXPERT_ROWS * EXPERT_ROWS
    pad_end = jnp.cumsum(padded)
    pad_start = pad_end - padded
    start_row = jnp.pad(pad_start.astype(f32), (0, LANES - N_EXPERTS)).reshape(1, LANES)
    dest = pl.pallas_call(
        _slot_kernel,
        grid=(t // tm,),
        in_specs=[pl.BlockSpec((tm, LANES), row), pl.BlockSpec((tm, LANES), row),
                  pl.BlockSpec((1, LANES), lambda i: (0, 0))],
        out_specs=pl.BlockSpec((8, tm), lambda i: (0, i)),
        out_shape=jax.ShapeDtypeStruct((8, t), jnp.int32),
        compiler_params=_params("parallel"),
        name="expert_slot",
    )(route, rank, start_row)
    block_start = jnp.arange(nb, dtype=jnp.int32) * EXPERT_ROWS
    block_e = jnp.sum((pad_end[None, :] <= block_start[:, None]).astype(jnp.int32), axis=1)
    block_e = jnp.minimum(block_e, N_EXPERTS - 1)
    of_block = block_e[:, None] == jnp.arange(N_EXPERTS, dtype=jnp.int32)[None, :]
    end_of_block = jnp.sum(jnp.where(of_block, (pad_start + counts)[None, :], 0), axis=1)
    block_valid = jnp.clip(end_of_block - block_start, 0, EXPERT_ROWS)
    return dest[:TOP_K], block_e.astype(jnp.int32), block_valid.astype(jnp.int32)


def _row_tile(t, want):
    tm = min(t, want)
    assert t % tm == 0
    return tm


def kernel(x, meta_tokens, ln0_g, ln0_b, w_in, na_rpb, gdn_conv_w, gdn_a_log, gdn_dt_bias, gdn_norm_w,
           w_branch_na, w_branch_gdn, w_out, ln1_g, ln1_b, router_group_w, router_group_b, router_expert_w,
           router_expert_b, expert_w_gate, expert_w_up, expert_w_down, ln2_g, ln2_b):
    batch, seq, d = x.shape
    assert d == D_MODEL and seq % GRID_W == 0 and DEPTH == 1
    t = batch * seq
    l = 0
    x2d = x.reshape(t, d)
    vec = lambda v: v.reshape(1, -1).astype(f32)

    w = w_in[l]
    w_main = jnp.concatenate([w[:, 3 * NA_W:AB_OFF], w[:, AB_OFF + N_AB:], w[:, :3 * NA_W]], axis=1).astype(bf16)
    w_ab, alog_row, dtb_row = _gate_lane_layout(w, gdn_a_log[l], gdn_dt_bias[l])

    p_real, gb_real = _in_proj(x2d, vec(ln0_g), vec(ln0_b), w_main, w_ab, alog_row, dtb_row, _row_tile(t, 1024),
                               IN_PROJ_COLS)
    p_meta, gb_meta = _in_proj(meta_tokens.astype(f32), vec(ln0_g), vec(ln0_b), w_main, w_ab, alog_row, dtb_row,
                               N_META, IN_PROJ_COLS)

    o_na = _na(p_real, p_meta, _na_bias_table(na_rpb[l]), batch, seq)
    o_gdn = _gdn(p_real, p_meta, gdn_conv_w[l].astype(f32), gb_real, gb_meta, vec(gdn_norm_w[l]), batch, seq)

    w_route = jnp.pad(jnp.concatenate([router_expert_w[l], router_group_w[l]], axis=1).astype(f32),
                      ((0, 0), (0, LANES - N_EXPERTS - N_GROUPS)))
    b_route = jnp.pad(jnp.concatenate([router_expert_b[l], router_group_b[l]]).astype(f32),
                      (0, LANES - N_EXPERTS - N_GROUPS)).reshape(1, LANES)
    h1, h1_packed, route = _merge(x2d, o_na, o_gdn, p_real, vec(ln0_g), vec(ln0_b), w_branch_na[l].astype(bf16),
                       w_branch_gdn[l].astype(bf16), w_out[l].astype(bf16), vec(ln1_g[l]), vec(ln1_b[l]),
                       w_route, b_route, _row_tile(t, MERGE_ROWS))

    dest, block_e, block_valid = _dispatch_plan(route, t)
    x_pad = _sc_dispatch(h1_packed, dest[0], dest[1], block_e.shape[0] * EXPERT_ROWS)
    y_pad = _experts(x_pad, block_e, block_valid, expert_w_gate[l], expert_w_up[l], expert_w_down[l])
    y_pairs = _sc_gather(y_pad, dest.reshape(-1)).reshape(TOP_K, t, HALF)
    out = _combine(h1, y_pairs, route, vec(ln2_g[l]), vec(ln2_b[l]), _row_tile(t, COMBINE_ROWS))
    return out.reshape(batch, seq, d)
```

```python
import functools

import numpy as np
import jax
import jax.numpy as jnp
from jax import lax
from jax.experimental import pallas as pl
from jax.experimental.pallas import tpu as pltpu
from jax.experimental.pallas import tpu_sc as plsc

D_MODEL = 1024
DEPTH = 1
GRID_W = 64
N_META = 16
NA_HEADS = 8
NA_HEAD_DIM = 64
NA_W = NA_HEADS * NA_HEAD_DIM
NA_WIN_ROWS = 8
NA_WIN_COLS = 16
GDN_HEADS = 8
GDN_HEAD_DIM = 128
GDN_W = GDN_HEADS * GDN_HEAD_DIM
GDN_CONV = 5
GDN_CHUNK = 64
N_GROUPS = 4
EXPERTS_PER_GROUP = 8
N_EXPERTS = N_GROUPS * EXPERTS_PER_GROUP
TOP_K = 2
D_EXPERT = 256
LN_EPS = 1e-5
NORM_EPS = 1e-6
DN_ALPHA = (2 * DEPTH) ** 0.25

LANES = 128
VMEM_LIMIT = 48 * 1024 * 1024
NEG = -1e30

CB_G_Q, CB_G_K, CB_G_V, CB_G_Z = 0, 8, 16, 24
CB_GATE_NA, CB_GATE_GDN = 32, 40
CB_NA_Q, CB_NA_K, CB_NA_V = 48, 52, 56
P_COLS = 60 * LANES
AB_OFF = 3 * NA_W + 4 * GDN_W
N_AB = 4 * GDN_HEADS

IN_PROJ_ROWS = 1024
IN_PROJ_COLS = 2560
COMBINE_ROWS = 1024
IN_PROJ_SPLIT = 2
NA_ROWS_PER_ITER = 4
MERGE_ROWS = 1024
MERGE_SPLIT = 2
EXPERT_ROWS = 512
EXPERT_SPLIT = 2
GATHER_WINDOW = 128

f32 = jnp.float32
bf16 = jnp.bfloat16


def _dot(a, b):
    return jnp.dot(a.astype(bf16), b.astype(bf16), preferred_element_type=f32)


def _dot_nt(a, b):
    return lax.dot_general(a.astype(bf16), b.astype(bf16), (((1,), (1,)), ((), ())),
                           preferred_element_type=f32)


def _dot_tn(a, b):
    return lax.dot_general(a.astype(bf16), b.astype(bf16), (((0,), (0,)), ((), ())),
                           preferred_element_type=f32)


def _split_bf16(x):
    hi = x.astype(bf16)
    lo = (x - hi.astype(f32)).astype(bf16)
    return hi, lo


def _dot3(x, w):
    xh, xl = _split_bf16(x)
    wh, wl = _split_bf16(w)
    d = lambda a, b: jnp.dot(a, b, preferred_element_type=f32)
    return d(xh, wh) + (d(xl, wh) + d(xh, wl))


def _layer_norm(x, g, b):
    xc = x - jnp.mean(x, -1, keepdims=True)
    var = jnp.mean(xc * xc, -1, keepdims=True)
    return xc * lax.rsqrt(var + LN_EPS) * g + b


def _sigmoid(x):
    return 0.5 * jnp.tanh(0.5 * x) + 0.5


def _silu(x):
    return x * _sigmoid(x)


HALF = D_MODEL // 2


def _pack_halves(x):
    def rounded(v):
        u = lax.bitcast_convert_type(v, jnp.uint32)
        return u + (jnp.uint32(0x7FFF) + ((u >> 16) & jnp.uint32(1)))

    word = (rounded(x[:, :HALF]) >> 16) | (rounded(x[:, HALF:]) & jnp.uint32(0xFFFF0000))
    return lax.bitcast_convert_type(word, f32)


def _unpack_halves(p):
    u = lax.bitcast_convert_type(p, jnp.uint32)
    return lax.bitcast_convert_type(u << 16, f32), lax.bitcast_convert_type(u & jnp.uint32(0xFFFF0000), f32)


def _params(*sem, vmem_limit=VMEM_LIMIT):
    return pltpu.CompilerParams(dimension_semantics=sem, vmem_limit_bytes=vmem_limit)


def _in_proj_kernel(x_ref, g_ref, b_ref, w_ref, wab_ref, alog_ref, dtb_ref, p_ref, gb_ref, hn_ref):
    first = pl.program_id(1) == 0

    @pl.when(first)
    def _():
        tm = x_ref.shape[0]
        n = tm // IN_PROJ_SPLIT if tm % (16 * IN_PROJ_SPLIT) == 0 else tm
        for g in range(tm // n):
            r = pl.ds(g * n, n)
            hb = _layer_norm(x_ref[r, :], g_ref[...], b_ref[...]).astype(bf16)
            hn_ref[r, :] = hb
            p_ref[r, :] = jnp.dot(hb, w_ref[...], preferred_element_type=f32).astype(bf16)
            ab = jnp.dot(hb, wab_ref[...].astype(bf16), preferred_element_type=f32)
            x = ab + dtb_ref[...]
            softplus = jnp.maximum(x, 0.0) + jnp.log1p(jnp.exp(-jnp.abs(x)))
            lane = lax.broadcasted_iota(jnp.int32, ab.shape, 1)
            gb_ref[r, :] = jnp.where((lane & 3) < 2, -jnp.exp(alog_ref[...]) * softplus, _sigmoid(ab))

    @pl.when(jnp.logical_not(first))
    def _():
        p_ref[...] = jnp.dot(hn_ref[...], w_ref[...], preferred_element_type=f32).astype(bf16)


def _gate_lane_layout(w_in, a_log, dt_bias):
    w_ab = w_in[:, AB_OFF:AB_OFF + N_AB].astype(f32).reshape(D_MODEL, 4, GDN_HEADS).transpose(0, 2, 1)
    w_ab = jnp.pad(w_ab.reshape(D_MODEL, N_AB), ((0, 0), (0, LANES - N_AB)))
    row = lambda p: jnp.pad(jnp.pad(p.astype(f32).T, ((0, 0), (0, 2))).reshape(1, N_AB), ((0, 0), (0, LANES - N_AB)))
    return w_ab, row(a_log), row(dt_bias)


def _in_proj(x2d, ln_g, ln_b, w_main, w_ab, alog_row, dtb_row, tm, tn):
    t = x2d.shape[0]
    return pl.pallas_call(
        _in_proj_kernel,
        grid=(t // tm, P_COLS // tn),
        in_specs=[
            pl.BlockSpec((tm, D_MODEL), lambda i, j: (i, 0)),
            pl.BlockSpec((1, D_MODEL), lambda i, j: (0, 0)),
            pl.BlockSpec((1, D_MODEL), lambda i, j: (0, 0)),
            pl.BlockSpec((D_MODEL, tn), lambda i, j: (0, j)),
            pl.BlockSpec((D_MODEL, LANES), lambda i, j: (0, 0)),
            pl.BlockSpec((1, LANES), lambda i, j: (0, 0)),
            pl.BlockSpec((1, LANES), lambda i, j: (0, 0)),
        ],
        out_specs=[
            pl.BlockSpec((tm, tn), lambda i, j: (i, j)),
            pl.BlockSpec((tm, LANES), lambda i, j: (i, 0)),
        ],
        out_shape=[
            jax.ShapeDtypeStruct((t, P_COLS), bf16),
            jax.ShapeDtypeStruct((t, LANES), f32),
        ],
        scratch_shapes=[pltpu.VMEM((tm, D_MODEL), bf16)],
        compiler_params=_params("parallel", "arbitrary"),
        name="in_proj",
    )(x2d, ln_g, ln_b, w_main, w_ab, alog_row, dtb_row)


def _na_kernel(q_ref, k_ref, v_ref, km_ref, vm_ref, bias_ref, o_ref, *, rows):
    scale = NA_HEAD_DIM ** -0.5
    kr = NA_WIN_ROWS
    lane = lax.broadcasted_iota(jnp.int32, (GRID_W, LANES), 1)
    low = lane < NA_HEAD_DIM
    first = lax.broadcasted_iota(jnp.int32, (2 * GRID_W, LANES), 0) < GRID_W
    lane2 = lax.broadcasted_iota(jnp.int32, (2 * GRID_W, LANES), 1)
    pair_mask = jnp.where(first == (lane2 < NA_HEAD_DIM), scale, 0.0).astype(bf16)
    pairs = range(NA_HEADS // 2)
    sl = [slice(LANES * p, LANES * (p + 1)) for p in pairs]

    nr = NA_ROWS_PER_ITER
    assert rows % nr == 0

    def row_block(it, carry):
        units = [(i, p) for i in range(nr) for p in pairs]
        r = [it * nr + i for i in range(nr)]
        r0 = [jnp.clip(r[i] - kr // 2, 0, rows - kr) for i in range(nr)]
        d0 = [r0[i] - r[i] + (NA_WIN_ROWS - 1) for i in range(nr)]
        qoff = [pl.multiple_of(r[i] * GRID_W, GRID_W) for i in range(nr)]
        koff = [pl.multiple_of(r0[i] * GRID_W, GRID_W) for i in range(nr)]
        q2 = [jnp.concatenate([q_ref[pl.ds(qoff[i], GRID_W), sl[p]]] * 2, axis=0) * pair_mask for i, p in units]
        s = [_dot_nt(q, k_ref[pl.ds(koff[i], kr * GRID_W), sl[p]]) + bias_ref[p, d0[i]] for (i, p), q in zip(units, q2)]
        sm = [_dot_nt(q, km_ref[:, sl[p]]) for (i, p), q in zip(units, q2)]
        m = [jnp.maximum(jnp.max(a, -1, keepdims=True), jnp.max(b, -1, keepdims=True)) for a, b in zip(s, sm)]
        e = [jnp.exp(a - c) for a, c in zip(s, m)]
        em = [jnp.exp(b - c) for b, c in zip(sm, m)]
        den = [jnp.sum(a, -1, keepdims=True) + jnp.sum(b, -1, keepdims=True) for a, b in zip(e, em)]
        o = [_dot(a, v_ref[pl.ds(koff[i], kr * GRID_W), sl[p]]) + _dot(b, vm_ref[:, sl[p]])
             for (i, p), a, b in zip(units, e, em)]
        o = [a / c for a, c in zip(o, den)]
        for (i, p), a in zip(units, o):
            o_ref[pl.ds(qoff[i], GRID_W), sl[p]] = jnp.where(low, a[:GRID_W], a[GRID_W:]).astype(bf16)
        return carry

    lax.fori_loop(0, rows // nr, row_block, 0)


def _na_bias_table(rpb):
    col = np.arange(GRID_W)
    col_start = np.clip(col - NA_WIN_COLS // 2, 0, GRID_W - NA_WIN_COLS)
    col_mask = (col[None, :] >= col_start[:, None]) & (col[None, :] < col_start[:, None] + NA_WIN_COLS)
    dc_idx = np.clip(col[None, :] - col[:, None] + NA_WIN_COLS - 1, 0, 2 * NA_WIN_COLS - 2)
    pick = (dc_idx[:, :, None] == np.arange(2 * NA_WIN_COLS - 1)).astype(np.float32)
    by_col = jnp.einsum("hrc,qkc->hrqk", rpb.astype(f32), pick, precision=lax.Precision.HIGHEST)
    by_col = jnp.where(col_mask[None, None], by_col, NEG)
    tbl = jnp.stack([by_col[:, d0:d0 + NA_WIN_ROWS] for d0 in range(NA_WIN_ROWS)], axis=1)
    tbl = tbl.reshape(NA_HEADS // 2, 2, NA_WIN_ROWS, NA_WIN_ROWS, GRID_W, GRID_W)
    return tbl.transpose(0, 2, 1, 4, 3, 5).reshape(NA_HEADS // 2, NA_WIN_ROWS, 2 * GRID_W, NA_WIN_ROWS * GRID_W)


def _na(p_real, p_meta, bias_tbl, batch, seq):
    rows = seq // GRID_W
    assert rows >= NA_WIN_ROWS
    col = lambda cb: (lambda b: (b, cb // 4))
    colm = lambda cb: (lambda b: (0, cb // 4))
    return pl.pallas_call(
        functools.partial(_na_kernel, rows=rows),
        grid=(batch,),
        in_specs=[
            pl.BlockSpec((seq, NA_W), col(CB_NA_Q)),
            pl.BlockSpec((seq, NA_W), col(CB_NA_K)),
            pl.BlockSpec((seq, NA_W), col(CB_NA_V)),
            pl.BlockSpec((N_META, NA_W), colm(CB_NA_K)),
            pl.BlockSpec((N_META, NA_W), colm(CB_NA_V)),
            pl.BlockSpec(bias_tbl.shape, lambda b: (0, 0, 0, 0)),
        ],
        out_specs=pl.BlockSpec((seq, NA_W), lambda b: (b, 0)),
        out_shape=jax.ShapeDtypeStruct((batch * seq, NA_W), bf16),
        compiler_params=_params("parallel"),
        name="na",
    )(p_real, p_real, p_real, p_meta, p_meta, bias_tbl)


GDN_LEAD = GDN_CHUNK - N_META
STAGE_PAD = N_META
GDN_VMEM_LIMIT = 58 * 1024 * 1024


GDN_BLOCK = 128
GDN_HEADS_PER_STEP = 4
GDN_BLOCKS_PER_ITER = 2


INV_BASE = 8


def _tri_masks(c, lower):
    ii = lax.broadcasted_iota(jnp.int32, (c, c), 0)
    jj = lax.broadcasted_iota(jnp.int32, (c, c), 1)
    same = lambda s: jnp.right_shift(ii, s.bit_length() - 1) == jnp.right_shift(jj, s.bit_length() - 1)
    levels = [same(INV_BASE)]
    s = INV_BASE
    while s < c:
        levels.append(same(2 * s) & jnp.logical_not(same(s)))
        s *= 2
    if lower:
        return ii == jj, jj <= ii, jj < ii, ii <= jj, levels
    return ii == jj, jj >= ii, jj > ii, ii >= jj, levels


def _unit_tri_inverse(ms, eye, levels):
    pws = [jnp.where(levels[0], -m, 0.0) for m in ms]
    inv = [jnp.where(eye, 1.0, 0.0) + n for n in pws]
    for _ in range(INV_BASE.bit_length() - 2):
        pws = [_dot(p, p) for p in pws]
        inv = [a + _dot(a, p) for a, p in zip(inv, pws)]
    for pair in levels[1:]:
        ts = [_dot(a, jnp.where(pair, m, 0.0)) for a, m in zip(inv, ms)]
        inv = [a - _dot(t, a) for a, t in zip(inv, ts)]
    return inv


def _gdn_prepare(chains):
    pre = []
    for qc, kc, vc, kk, qk, g_col, beta_col, masks in chains:
        eye, incl, strict, incl_t, levels = masks
        g_row = jnp.sum(jnp.where(eye, g_col, 0.0), axis=0, keepdims=True)
        gc_col = jnp.sum(jnp.where(incl, g_row, 0.0), axis=1, keepdims=True)
        gc_row = jnp.sum(jnp.where(incl_t, g_col, 0.0), axis=0, keepdims=True)
        total = jnp.sum(g_col, axis=0, keepdims=True)
        decay = jnp.where(incl, jnp.exp(jnp.minimum(gc_col - gc_row, 0.0)), 0.0)
        pre.append((gc_col, beta_col, total, decay, jnp.where(strict, kk * beta_col * decay, 0.0)))
    eye, levels = chains[0][7][0], chains[0][7][4]
    inv = _unit_tri_inverse([p[4] for p in pre], eye, levels)
    sols = [_dot(a, jnp.concatenate([vc * beta, kc * (beta * jnp.exp(gc))], axis=1))
            for (qc, kc, vc, *_), (gc, beta, *_), a in zip(chains, pre, inv)]
    return [(sol[:, :GDN_HEAD_DIM], sol[:, GDN_HEAD_DIM:], qc * jnp.exp(gc), qk * decay, kc * jnp.exp(total - gc),
             jnp.exp(total))
            for (qc, kc, vc, kk, qk, *_), (gc, beta, total, decay, _), sol in zip(chains, pre, sols)]


def _gdn_kernel(q_ref, k_ref, v_ref, z_ref, qm_ref, km_ref, vm_ref,
                wq_ref, wk_ref, wv_ref, gt_ref, gtm_ref, nw_ref, o_ref,
                stage, u_s, wqd_s, aqk_s, kd_s, gl_s, st_s, *, seq, hg):
    c = GDN_BLOCK
    cm = GDN_CHUNK
    nblk = seq // c
    nb = GDN_BLOCKS_PER_ITER
    assert nblk % nb == 0
    h0 = pl.program_id(1) * hg
    in_refs = (q_ref, k_ref, v_ref)
    meta_refs = (qm_ref, km_ref, vm_ref)
    w_refs = (wq_ref, wk_ref, wv_ref)
    cols = lambda hh: slice(LANES * hh, LANES * (hh + 1))
    halo = GDN_CONV // 2
    pad = STAGE_PAD

    def conv_block(hh, p0, rows):
        out = []
        for t, kind in enumerate("qkv"):
            w = w_refs[t][:, cols(hh)]
            acc = None
            for tap in range(GDN_CONV):
                term = stage[hh, t, pl.ds(p0 - halo + tap, rows), :] * w[tap:tap + 1, :]
                acc = term if acc is None else acc + term
            y = _silu(acc)
            if kind != "v":
                dh = float(GDN_HEAD_DIM) if kind == "q" else 1.0
                ss = jnp.dot((y * y).astype(bf16), jnp.full((LANES, LANES), dh, bf16), preferred_element_type=f32)
                y = y * lax.rsqrt(ss + dh * NORM_EPS)
            out.append(y)
        return out

    def gates(gb, hh):
        return pltpu.roll(gb, jnp.bitwise_and(LANES - 4 * (h0 + hh), LANES - 1), 1)

    masks_meta = _tri_masks(cm, True)
    live_col = lax.broadcasted_iota(jnp.int32, (cm, 1), 0) >= GDN_LEAD
    heads = range(hg)
    for hh in heads:
        for t in range(3):
            stage[hh, t, pl.ds(0, pad + GDN_LEAD), :] = jnp.zeros((pad + GDN_LEAD, LANES), f32)
            stage[hh, t, pl.ds(pad + GDN_LEAD, N_META), :] = meta_refs[t][:, cols(hh)].astype(f32)
            stage[hh, t, pl.ds(pad + cm, pad), :] = in_refs[t][pl.ds(0, pad), cols(hh)].astype(f32)
    metas = [[jnp.where(live_col, y, 0.0) for y in conv_block(hh, pad, cm)] for hh in heads]
    qkk = [_dot_nt(jnp.concatenate([q0, k0], axis=0), k0) for q0, k0, _ in metas]
    chain_in = []
    gb_meta = jnp.concatenate([jnp.zeros((GDN_LEAD, LANES), f32), gtm_ref[...]], axis=0)
    for hh in heads:
        q0, k0, v0 = metas[hh]
        gm = gates(gb_meta, hh)
        for d in range(2):
            chain_in.append((q0, k0, v0, qkk[hh][cm:], qkk[hh][:cm], gm[:, d:d + 1], gm[:, 2 + d:3 + d], masks_meta))
    s0 = [_dot_tn(r[4], r[0]) for r in _gdn_prepare(chain_in)]
    for i, (hh, d) in enumerate((hh, d) for hh in heads for d in range(2)):
        st_s[hh, d] = s0[i]

    masks = (_tri_masks(c, True), _tri_masks(c, False))

    def prepare_blocks(jj, carry):
        units = [(hh, b) for hh in heads for b in range(nb)]
        rows = [pl.multiple_of((jj * nb + b) * c, c) for b in range(nb)]
        first = pl.multiple_of(jj * (nb * c), nb * c)
        before = pl.multiple_of(jnp.maximum(first - pad, 0), pad)
        after = pl.multiple_of(jnp.minimum(first + nb * c, seq - pad), pad)
        for hh in heads:
            for t in range(3):
                prev = in_refs[t][pl.ds(before, pad), cols(hh)]
                stage[hh, t, pl.ds(0, pad), :] = jnp.where(jj == 0, meta_refs[t][:, cols(hh)], prev).astype(f32)
                stage[hh, t, pl.ds(pad, nb * c), :] = in_refs[t][pl.ds(first, nb * c), cols(hh)].astype(f32)
                nxt = in_refs[t][pl.ds(after, pad), cols(hh)].astype(f32)
                stage[hh, t, pl.ds(pad + nb * c, pad), :] = jnp.where(jj == nblk // nb - 1, 0.0, nxt)
        convs = [conv_block(hh, pad + j * c, c) for hh, j in units]
        qkk = [_dot_nt(jnp.concatenate([qc, kc], axis=0), kc) for qc, kc, _ in convs]
        chain_in, chain_id = [], []
        for (hh, j), (qc, kc, vc), qk_kk in zip(units, convs, qkk):
            gt = gates(gt_ref[pl.ds(rows[j], c), :], hh)
            for d in range(2):
                chain_in.append((qc, kc, vc, qk_kk[c:], qk_kk[:c], gt[:, d:d + 1], gt[:, 2 + d:3 + d], masks[d]))
                chain_id.append((hh, d, j))
        for (hh, d, j), (u, w, qd, aqk, kd, g_last) in zip(chain_id, _gdn_prepare(chain_in)):
            r0 = rows[j]
            u_s[hh, d, pl.ds(r0, c), :] = u
            wqd_s[hh, d, pl.ds(pl.multiple_of(2 * r0, 2 * c), c), :] = w.astype(bf16)
            wqd_s[hh, d, pl.ds(pl.multiple_of(2 * r0 + c, c), c), :] = qd.astype(bf16)
            aqk_s[hh, d, pl.ds(r0, c), :] = aqk.astype(bf16)
            kd_s[hh, d, pl.ds(r0, c), :] = kd.astype(bf16)
            gl_s[hh, d, pl.ds(pl.multiple_of((jj * nb + j) * 8, 8), 8), :] = jnp.broadcast_to(g_last, (8, LANES))
        return carry

    lax.fori_loop(0, nblk // nb, prepare_blocks, 0)

    def scan_step(j, carry):
        rows = (pl.multiple_of(j * c, c), pl.multiple_of((nblk - 1 - j) * c, c))
        grow = (pl.multiple_of(j * 8, 8), pl.multiple_of((nblk - 1 - j) * 8, 8))
        chains = [(hh, d) for hh in heads for d in range(2)]
        states = [st_s[hh, d] for hh, d in chains]
        ws = [jnp.dot(wqd_s[hh, d, pl.ds(pl.multiple_of(2 * rows[d], 2 * c), 2 * c), :], s.astype(bf16),
                      preferred_element_type=f32) for (hh, d), s in zip(chains, states)]
        v_new = [(u_s[hh, d, pl.ds(rows[d], c), :] - w[:c]).astype(bf16) for (hh, d), w in zip(chains, ws)]
        outs = [w[c:] + jnp.dot(aqk_s[hh, d, pl.ds(rows[d], c), :], v, preferred_element_type=f32)
                for (hh, d), w, v in zip(chains, ws, v_new)]
        new_states = [s * gl_s[hh, d, pl.ds(grow[d], 1), :] + _dot_tn(kd_s[hh, d, pl.ds(rows[d], c), :], v)
                      for (hh, d), s, v in zip(chains, states, v_new)]
        for (hh, d), s in zip(chains, new_states):
            st_s[hh, d] = s
        for (hh, d), o in zip(chains, outs):
            u_s[hh, d, pl.ds(rows[d], c), :] = o
        return carry

    lax.fori_loop(0, nblk, scan_step, 0)

    def finish(j, carry):
        r0 = pl.multiple_of(j * c, c)
        for hh in range(hg):
            o = u_s[hh, 0, pl.ds(r0, c), :] + u_s[hh, 1, pl.ds(r0, c), :]
            o = o * lax.rsqrt(jnp.mean(o * o, -1, keepdims=True) + NORM_EPS) * nw_ref[...]
            o_ref[pl.ds(r0, c), cols(hh)] = (o * _silu(z_ref[pl.ds(r0, c), cols(hh)].astype(f32))).astype(bf16)
        return carry

    lax.fori_loop(0, nblk, finish, 0)


def _gdn(p_real, p_meta, conv_w, gates_real, gates_meta, norm_w, batch, seq):
    hg = GDN_HEADS_PER_STEP
    assert seq % GDN_BLOCK == 0 and GDN_HEADS % hg == 0
    nblk = seq // GDN_BLOCK
    wide = hg * LANES
    col = lambda cb: (lambda b, h: (b, cb // hg + h))
    colm = lambda cb: (lambda b, h: (0, cb // hg + h))
    colw = lambda k: (lambda b, h: (0, k * (GDN_HEADS // hg) + h))
    return pl.pallas_call(
        functools.partial(_gdn_kernel, seq=seq, hg=hg),
        grid=(batch, GDN_HEADS // hg),
        in_specs=[
            pl.BlockSpec((seq, wide), col(CB_G_Q)),
            pl.BlockSpec((seq, wide), col(CB_G_K)),
            pl.BlockSpec((seq, wide), col(CB_G_V)),
            pl.BlockSpec((seq, wide), col(CB_G_Z), pipeline_mode=pl.Buffered(1)),
            pl.BlockSpec((N_META, wide), colm(CB_G_Q)),
            pl.BlockSpec((N_META, wide), colm(CB_G_K)),
            pl.BlockSpec((N_META, wide), colm(CB_G_V)),
            pl.BlockSpec((GDN_CONV, wide), colw(0)),
            pl.BlockSpec((GDN_CONV, wide), colw(1)),
            pl.BlockSpec((GDN_CONV, wide), colw(2)),
            pl.BlockSpec((seq, LANES), lambda b, h: (b, 0)),
            pl.BlockSpec((N_META, LANES), lambda b, h: (0, 0)),
            pl.BlockSpec((1, LANES), lambda b, h: (0, 0)),
        ],
        out_specs=pl.BlockSpec((seq, wide), lambda b, h: (b, h)),
        out_shape=jax.ShapeDtypeStruct((batch * seq, GDN_W), bf16),
        scratch_shapes=[
            pltpu.VMEM((hg, 3, GDN_BLOCKS_PER_ITER * GDN_BLOCK + 2 * STAGE_PAD, LANES), f32),
            pltpu.VMEM((hg, 2, seq, LANES), f32),
            pltpu.VMEM((hg, 2, 2 * seq, LANES), bf16),
            pltpu.VMEM((hg, 2, seq, LANES), bf16),
            pltpu.VMEM((hg, 2, seq, LANES), bf16),
            pltpu.VMEM((hg, 2, 8 * nblk, LANES), f32),
            pltpu.VMEM((hg, 2, GDN_HEAD_DIM, GDN_HEAD_DIM), f32),
        ],
        compiler_params=_params("parallel", "parallel", vmem_limit=GDN_VMEM_LIMIT),
        name="gdn",
    )(p_real, p_real, p_real, p_real, p_meta, p_meta, p_meta, conv_w, conv_w, conv_w, gates_real, gates_meta, norm_w)


def _route(logits):
    lane = lax.broadcasted_iota(jnp.int32, logits.shape, 1)
    big = jnp.int32(1 << 20)
    is_g = (lane >= N_EXPERTS) & (lane < N_EXPERTS + N_GROUPS)
    gl = jnp.where(is_g, logits, -jnp.inf)
    gmax = jnp.max(gl, -1, keepdims=True)
    g_idx = jnp.min(jnp.where(gl == gmax, lane - N_EXPERTS, big), -1, keepdims=True)
    g_w = 1.0 / jnp.sum(jnp.where(is_g, jnp.exp(gl - gmax), 0.0), -1, keepdims=True)
    in_grp = (lane < N_EXPERTS) & (jnp.right_shift(lane, 3) == g_idx)
    el = jnp.where(in_grp, logits, -jnp.inf)
    m1 = jnp.max(el, -1, keepdims=True)
    i1 = jnp.min(jnp.where(el == m1, lane, big), -1, keepdims=True)
    el2 = jnp.where(lane == i1, -jnp.inf, el)
    m2 = jnp.max(el2, -1, keepdims=True)
    i2 = jnp.min(jnp.where(el2 == m2, lane, big), -1, keepdims=True)
    e2 = jnp.exp(m2 - m1)
    w1 = g_w / (1.0 + e2)
    w2 = g_w * e2 / (1.0 + e2)
    out = jnp.where(lane == 0, i1.astype(f32), 0.0)
    out = jnp.where(lane == 1, i2.astype(f32), out)
    out = jnp.where(lane == 2, w1, out)
    return jnp.where(lane == 3, w2, out)


def _merge_kernel(x_ref, ona_ref, ogdn_ref, gna_ref, ggdn_ref, ln0g_ref, ln0b_ref, wna_ref, wgdn_ref,
                  wout_ref, ln1g_ref, ln1b_ref, wr_ref, br_ref, h1_ref, h1p_ref, route_ref):
    tm = x_ref.shape[0]
    rows = [pl.ds(i * (tm // MERGE_SPLIT), tm // MERGE_SPLIT) for i in range(MERGE_SPLIT)]
    y_na = [jnp.dot(ona_ref[r, :], wna_ref[...], preferred_element_type=f32) for r in rows]
    y_gdn = [jnp.dot(ogdn_ref[r, :], wgdn_ref[...], preferred_element_type=f32) for r in rows]
    merged = [_sigmoid(gna_ref[r, :].astype(f32)) * a + _sigmoid(ggdn_ref[r, :].astype(f32)) * b
              for r, a, b in zip(rows, y_na, y_gdn)]
    mix = [jnp.dot(m.astype(bf16), wout_ref[...], preferred_element_type=f32) for m in merged]
    h = [_layer_norm(x_ref[r, :], ln0g_ref[...], ln0b_ref[...]) for r in rows]
    h1 = [_layer_norm(DN_ALPHA * a + b, ln1g_ref[...], ln1b_ref[...]) for a, b in zip(h, mix)]
    logits = [_dot3(a, wr_ref[...]) + br_ref[...] for a in h1]
    for r, a, lg in zip(rows, h1, logits):
        h1_ref[r, :] = a
        h1p_ref[r, :] = _pack_halves(a)
        route_ref[r, :] = _route(lg)


def _merge(x2d, o_na, o_gdn, p_real, ln0_g, ln0_b, w_na, w_gdn, w_out, ln1_g, ln1_b, w_route, b_route, tm):
    t = x2d.shape[0]
    row = lambda i: (i, 0)
    const = lambda i: (0, 0)
    return pl.pallas_call(
        _merge_kernel,
        grid=(t // tm,),
        in_specs=[
            pl.BlockSpec((tm, D_MODEL), row),
            pl.BlockSpec((tm, NA_W), row),
            pl.BlockSpec((tm, GDN_W), row),
            pl.BlockSpec((tm, D_MODEL), lambda i: (i, CB_GATE_NA // 8)),
            pl.BlockSpec((tm, D_MODEL), lambda i: (i, CB_GATE_GDN // 8)),
            pl.BlockSpec((1, D_MODEL), const),
            pl.BlockSpec((1, D_MODEL), const),
            pl.BlockSpec((NA_W, D_MODEL), const),
            pl.BlockSpec((GDN_W, D_MODEL), const),
            pl.BlockSpec((D_MODEL, D_MODEL), const),
            pl.BlockSpec((1, D_MODEL), const),
            pl.BlockSpec((1, D_MODEL), const),
            pl.BlockSpec((D_MODEL, LANES), const),
            pl.BlockSpec((1, LANES), const),
        ],
        out_specs=[pl.BlockSpec((tm, D_MODEL), row), pl.BlockSpec((tm, HALF), row), pl.BlockSpec((tm, LANES), row)],
        out_shape=[jax.ShapeDtypeStruct((t, D_MODEL), f32), jax.ShapeDtypeStruct((t, HALF), f32),
                   jax.ShapeDtypeStruct((t, LANES), f32)],
        compiler_params=_params("parallel"),
        name="merge",
    )(x2d, o_na, o_gdn, p_real, p_real, ln0_g, ln0_b, w_na, w_gdn, w_out, ln1_g, ln1_b, w_route, b_route)


def _sc_dispatch(rows, dest0, dest1, n_slots):
    t, width = rows.shape
    mesh = plsc.VectorSubcoreMesh(core_axis_name="core", subcore_axis_name="subcore")
    n_workers = mesh.num_cores * mesh.num_subcores
    per_worker = t // n_workers
    assert t % n_workers == 0 and per_worker % GATHER_WINDOW == 0

    @functools.partial(
        pl.kernel, out_type=jax.ShapeDtypeStruct((n_slots, width), rows.dtype), mesh=mesh,
        scratch_types=[pltpu.VMEM((GATHER_WINDOW,), jnp.int32), pltpu.VMEM((GATHER_WINDOW,), jnp.int32),
                       pltpu.VMEM((GATHER_WINDOW, width), rows.dtype)],
        name="row_dispatch")
    def dispatch(rows_hbm, d0_hbm, d1_hbm, out_hbm, d0_vmem, d1_vmem, rows_vmem):
        worker = lax.axis_index("subcore") * mesh.num_cores + lax.axis_index("core")

        @pl.loop(0, per_worker // GATHER_WINDOW)
        def _(step):
            base = pl.multiple_of(worker * per_worker + step * GATHER_WINDOW, GATHER_WINDOW)
            pltpu.sync_copy(d0_hbm.at[pl.ds(base, GATHER_WINDOW)], d0_vmem)
            pltpu.sync_copy(d1_hbm.at[pl.ds(base, GATHER_WINDOW)], d1_vmem)
            pltpu.sync_copy(rows_hbm.at[pl.ds(base, GATHER_WINDOW)], rows_vmem)
            pltpu.sync_copy(rows_vmem, out_hbm.at[d0_vmem])
            pltpu.sync_copy(rows_vmem, out_hbm.at[d1_vmem])

    return dispatch(rows, dest0, dest1)


def _sc_gather(table, idx):
    n = idx.shape[0]
    width = table.shape[1]
    mesh = plsc.VectorSubcoreMesh(core_axis_name="core", subcore_axis_name="subcore")
    n_workers = mesh.num_cores * mesh.num_subcores
    per_worker = n // n_workers
    assert n % n_workers == 0 and per_worker % GATHER_WINDOW == 0

    @functools.partial(
        pl.kernel, out_type=jax.ShapeDtypeStruct((n, width), table.dtype), mesh=mesh,
        scratch_types=[pltpu.VMEM((GATHER_WINDOW,), jnp.int32), pltpu.VMEM((GATHER_WINDOW, width), table.dtype)],
        name="row_gather")
    def gather(tbl_hbm, idx_hbm, out_hbm, idx_vmem, rows_vmem):
        worker = lax.axis_index("subcore") * mesh.num_cores + lax.axis_index("core")

        @pl.loop(0, per_worker // GATHER_WINDOW)
        def _(step):
            base = pl.multiple_of(worker * per_worker + step * GATHER_WINDOW, GATHER_WINDOW)
            pltpu.sync_copy(idx_hbm.at[pl.ds(base, GATHER_WINDOW)], idx_vmem)
            pltpu.sync_copy(tbl_hbm.at[idx_vmem], rows_vmem)
            pltpu.sync_copy(rows_vmem, out_hbm.at[pl.ds(base, GATHER_WINDOW)])

    return gather(table, idx)


def _expert_kernel(be_ref, valid_ref, x_ref, wg_ref, wu_ref, wd_ref, y_ref, wgu_s, wd_s):
    i = pl.program_id(0)
    valid = valid_ref[i]

    @pl.when(valid == 0)
    def _():
        y_ref[...] = jnp.zeros_like(y_ref)

    @pl.when(valid > 0)
    def _():
        prev = jnp.maximum(i, 1) - 1
        fresh = jnp.logical_or(i == 0, jnp.logical_or(be_ref[i] != be_ref[prev], valid_ref[prev] == 0))

        @pl.when(fresh)
        def _():
            wgu_s[:, :D_EXPERT] = wg_ref[...].astype(bf16)
            wgu_s[:, D_EXPERT:] = wu_ref[...].astype(bf16)
            wd_s[...] = wd_ref[...].astype(bf16)

        n = EXPERT_ROWS // EXPERT_SPLIT
        groups = [pl.ds(g * n, n) for g in range(EXPERT_SPLIT)]
        row = lax.broadcasted_iota(jnp.int32, (n, HALF), 0)
        xs = [_unpack_halves(jnp.where(row + g * n < valid, x_ref[r, :], 0.0)) for g, r in enumerate(groups)]
        gu = [jnp.dot(xa.astype(bf16), wgu_s[:HALF, :], preferred_element_type=f32)
              + jnp.dot(xb.astype(bf16), wgu_s[HALF:, :], preferred_element_type=f32) for xa, xb in xs]
        hdn = [(_silu(a[:, :D_EXPERT]) * a[:, D_EXPERT:]).astype(bf16) for a in gu]
        ys = [jnp.dot(a, wd_s[...], preferred_element_type=f32) for a in hdn]
        for r, a in zip(groups, ys):
            y_ref[r, :] = _pack_halves(a)


def _experts(x_pad, block_e, block_valid, w_gate, w_up, w_down):
    nb = x_pad.shape[0] // EXPERT_ROWS
    grid_spec = pltpu.PrefetchScalarGridSpec(
        num_scalar_prefetch=2,
        grid=(nb,),
        in_specs=[
            pl.BlockSpec((EXPERT_ROWS, HALF), lambda i, be, nv: (i, 0)),
            pl.BlockSpec((None, D_MODEL, D_EXPERT), lambda i, be, nv: (be[i], 0, 0)),
            pl.BlockSpec((None, D_MODEL, D_EXPERT), lambda i, be, nv: (be[i], 0, 0)),
            pl.BlockSpec((None, D_EXPERT, D_MODEL), lambda i, be, nv: (be[i], 0, 0)),
        ],
        out_specs=pl.BlockSpec((EXPERT_ROWS, HALF), lambda i, be, nv: (i, 0)),
        scratch_shapes=[pltpu.VMEM((D_MODEL, 2 * D_EXPERT), bf16), pltpu.VMEM((D_EXPERT, D_MODEL), bf16)],
    )
    return pl.pallas_call(
        _expert_kernel,
        grid_spec=grid_spec,
        out_shape=jax.ShapeDtypeStruct(x_pad.shape, f32),
        compiler_params=_params("arbitrary"),
        name="experts",
    )(block_e, block_valid, x_pad, w_gate, w_up, w_down)


def _combine_kernel(h1_ref, y1_ref, y2_ref, route_ref, g_ref, b_ref, o_ref):
    r = route_ref[...]
    y1a, y1b = _unpack_halves(y1_ref[...])
    y2a, y2b = _unpack_halves(y2_ref[...])
    w1, w2 = r[:, 2:3], r[:, 3:4]
    ffn = jnp.concatenate([y1a * w1 + y2a * w2, y1b * w1 + y2b * w2], axis=1)
    o_ref[...] = _layer_norm(DN_ALPHA * h1_ref[...] + ffn, g_ref[...], b_ref[...])


def _combine(h1, y_pairs, route, ln_g, ln_b, tm):
    t = h1.shape[0]
    row = lambda i: (i, 0)
    const = lambda i: (0, 0)
    return pl.pallas_call(
        _combine_kernel,
        grid=(t // tm,),
        in_specs=[
            pl.BlockSpec((tm, D_MODEL), row),
            pl.BlockSpec((None, tm, HALF), lambda i: (0, i, 0)),
            pl.BlockSpec((None, tm, HALF), lambda i: (1, i, 0)),
            pl.BlockSpec((tm, LANES), row),
            pl.BlockSpec((1, D_MODEL), const),
            pl.BlockSpec((1, D_MODEL), const),
        ],
        out_specs=pl.BlockSpec((tm, D_MODEL), row),
        out_shape=jax.ShapeDtypeStruct((t, D_MODEL), f32),
        compiler_params=_params("parallel"),
        name="combine",
    )(h1, y_pairs, y_pairs, route, ln_g, ln_b)


PLAN_ROWS = 512


def _choice_onehots(route):
    lane = lax.broadcasted_iota(jnp.int32, route.shape, 1).astype(f32)
    return lane == route[:, 0:1], lane == route[:, 1:2]


def _rank_kernel(route_ref, rank_ref, counts_ref, carry_ref):
    @pl.when(pl.program_id(0) == 0)
    def _():
        carry_ref[...] = jnp.zeros_like(carry_ref)

    tm = route_ref.shape[0]
    oh0, oh1 = _choice_onehots(route_ref[...])
    both = jnp.where(oh0, 1.0, 0.0) + jnp.where(oh1, 1.0, 0.0)
    earlier = lax.broadcasted_iota(jnp.int32, (tm, tm), 1) < lax.broadcasted_iota(jnp.int32, (tm, tm), 0)
    before = jnp.dot(jnp.where(earlier, 1.0, 0.0).astype(bf16), both.astype(bf16), preferred_element_type=f32)
    before = before + carry_ref[...]
    rank0 = jnp.sum(jnp.where(oh0, before, 0.0), axis=1, keepdims=True)
    rank1 = jnp.sum(jnp.where(oh1, before, 0.0), axis=1, keepdims=True)
    lane = lax.broadcasted_iota(jnp.int32, (tm, LANES), 1)
    rank_ref[...] = jnp.where(lane == 0, rank0, jnp.where(lane == 1, rank1, 0.0))
    carry_ref[...] += jnp.sum(both, axis=0, keepdims=True)
    counts_ref[...] = carry_ref[...]


def _slot_kernel(route_ref, rank_ref, start_ref, dest_ref):
    oh0, oh1 = _choice_onehots(route_ref[...])
    rank = rank_ref[...]
    d0 = jnp.sum(jnp.where(oh0, start_ref[...], 0.0), axis=1, keepdims=True) + rank[:, 0:1]
    d1 = jnp.sum(jnp.where(oh1, start_ref[...], 0.0), axis=1, keepdims=True) + rank[:, 1:2]
    lane = lax.broadcasted_iota(jnp.int32, rank.shape, 1)
    by_lane = jnp.where(lane == 0, d0, jnp.where(lane == 1, d1, 0.0))
    dest_ref[...] = by_lane.T[:8, :].astype(jnp.int32)


def _dispatch_plan(route, t):
    nb = t * TOP_K // EXPERT_ROWS + N_EXPERTS
    tm = _row_tile(t, PLAN_ROWS)
    row = lambda i: (i, 0)
    rank, counts_row = pl.pallas_call(
        _rank_kernel,
        grid=(t // tm,),
        in_specs=[pl.BlockSpec((tm, LANES), row)],
        out_specs=[pl.BlockSpec((tm, LANES), row), pl.BlockSpec((1, LANES), lambda i: (0, 0))],
        out_shape=[jax.ShapeDtypeStruct((t, LANES), f32), jax.ShapeDtypeStruct((1, LANES), f32)],
        scratch_shapes=[pltpu.VMEM((1, LANES), f32)],
        compiler_params=_params("arbitrary"),
        name="expert_rank",
    )(route)
    counts = counts_row[0, :N_EXPERTS].astype(jnp.int32)
    padded = (counts + EXPERT_ROWS - 1) // EXPERT_ROWS * EXPERT_ROWS
    pad_end = jnp.cumsum(padded)
    pad_start = pad_end - padded
    start_row = jnp.pad(pad_start.astype(f32), (0, LANES - N_EXPERTS)).reshape(1, LANES)
    dest = pl.pallas_call(
        _slot_kernel,
        grid=(t // tm,),
        in_specs=[pl.BlockSpec((tm, LANES), row), pl.BlockSpec((tm, LANES), row),
                  pl.BlockSpec((1, LANES), lambda i: (0, 0))],
        out_specs=pl.BlockSpec((8, tm), lambda i: (0, i)),
        out_shape=jax.ShapeDtypeStruct((8, t), jnp.int32),
        compiler_params=_params("parallel"),
        name="expert_slot",
    )(route, rank, start_row)
    block_start = jnp.arange(nb, dtype=jnp.int32) * EXPERT_ROWS
    block_e = jnp.sum((pad_end[None, :] <= block_start[:, None]).astype(jnp.int32), axis=1)
    block_e = jnp.minimum(block_e, N_EXPERTS - 1)
    of_block = block_e[:, None] == jnp.arange(N_EXPERTS, dtype=jnp.int32)[None, :]
    end_of_block = jnp.sum(jnp.where(of_block, (pad_start + counts)[None, :], 0), axis=1)
    block_valid = jnp.clip(end_of_block - block_start, 0, EXPERT_ROWS)
    return dest[:TOP_K], block_e.astype(jnp.int32), block_valid.astype(jnp.int32)


def _row_tile(t, want):
    tm = min(t, want)
    assert t % tm == 0
    return tm


def kernel(x, meta_tokens, ln0_g, ln0_b, w_in, na_rpb, gdn_conv_w, gdn_a_log, gdn_dt_bias, gdn_norm_w,
           w_branch_na, w_branch_gdn, w_out, ln1_g, ln1_b, router_group_w, router_group_b, router_expert_w,
           router_expert_b, expert_w_gate, expert_w_up, expert_w_down, ln2_g, ln2_b):
    batch, seq, d = x.shape
    assert d == D_MODEL and seq % GRID_W == 0 and DEPTH == 1
    t = batch * seq
    l = 0
    x2d = x.reshape(t, d)
    vec = lambda v: v.reshape(1, -1).astype(f32)

    w = w_in[l]
    w_main = jnp.concatenate([w[:, 3 * NA_W:AB_OFF], w[:, AB_OFF + N_AB:], w[:, :3 * NA_W]], axis=1).astype(bf16)
    w_ab, alog_row, dtb_row = _gate_lane_layout(w, gdn_a_log[l], gdn_dt_bias[l])

    p_real, gb_real = _in_proj(x2d, vec(ln0_g), vec(ln0_b), w_main, w_ab, alog_row, dtb_row, _row_tile(t, IN_PROJ_ROWS),
                               IN_PROJ_COLS)
    p_meta, gb_meta = _in_proj(meta_tokens.astype(f32), vec(ln0_g), vec(ln0_b), w_main, w_ab, alog_row, dtb_row,
                               N_META, IN_PROJ_COLS)

    o_na = _na(p_real, p_meta, _na_bias_table(na_rpb[l]), batch, seq)
    o_gdn = _gdn(p_real, p_meta, gdn_conv_w[l].astype(f32), gb_real, gb_meta, vec(gdn_norm_w[l]), batch, seq)

    w_route = jnp.pad(jnp.concatenate([router_expert_w[l], router_group_w[l]], axis=1).astype(f32),
                      ((0, 0), (0, LANES - N_EXPERTS - N_GROUPS)))
    b_route = jnp.pad(jnp.concatenate([router_expert_b[l], router_group_b[l]]).astype(f32),
                      (0, LANES - N_EXPERTS - N_GROUPS)).reshape(1, LANES)
    h1, h1_packed, route = _merge(x2d, o_na, o_gdn, p_real, vec(ln0_g), vec(ln0_b), w_branch_na[l].astype(bf16),
                       w_branch_gdn[l].astype(bf16), w_out[l].astype(bf16), vec(ln1_g[l]), vec(ln1_b[l]),
                       w_route, b_route, _row_tile(t, MERGE_ROWS))

    dest, block_e, block_valid = _dispatch_plan(route, t)
    x_pad = _sc_dispatch(h1_packed, dest[0], dest[1], block_e.shape[0] * EXPERT_ROWS)
    y_pad = _experts(x_pad, block_e, block_valid, expert_w_gate[l], expert_w_up[l], expert_w_down[l])
    y_pairs = _sc_gather(y_pad, dest.reshape(-1)).reshape(TOP_K, t, HALF)
    out = _combine(h1, y_pairs, route, vec(ln2_g[l]), vec(ln2_b[l]), _row_tile(t, COMBINE_ROWS))
    return out.reshape(batch, seq, d)
```
